```python
import numpy as np
import jax
import jax.numpy as jnp
from jax import lax

D_MODEL = 1024
BATCH = 2
SEQ = 8192
DEPTH = 2
DEC_BATCH = 128
DEC_SEQ = 4
PAST_LEN = 2048
PAGE_SIZE = 128

N_HEADS_A = D_MODEL // 256
HEAD_DIM_A = 128
WIDTH_A = N_HEADS_A * HEAD_DIM_A
CONV_W = 4
DELTA_CHUNK = 64
WINDOWS = (128, 512, 2048)
DILATIONS = (1, 4, 16)
N_GROUPS = 3
HEADS_PER_GROUP = D_MODEL // 256
HEAD_DIM_B = 64
N_HEADS_B = N_GROUPS * HEADS_PER_GROUP
WIDTH_B = N_HEADS_B * HEAD_DIM_B
WIDTH_B_OUT = HEADS_PER_GROUP * HEAD_DIM_B
Q_BLOCK = 128
D_FF = 4 * D_MODEL
IN_SPLITS = (3 * WIDTH_A, WIDTH_A, N_HEADS_A, N_HEADS_A, WIDTH_B, WIDTH_B, WIDTH_B, D_MODEL, D_MODEL)
N_IN = 3 * WIDTH_A + WIDTH_A + 2 * N_HEADS_A + 3 * WIDTH_B + 2 * D_MODEL
EPS = 1e-6

kernel_name = 'dilated_delta_hybrid_step'


def rms_norm(x, g):
    x32 = x.astype(jnp.float32)
    y = x32 * lax.rsqrt(jnp.mean(x32 * x32, axis=-1, keepdims=True) + EPS)
    return (y * g.astype(jnp.float32)).astype(x.dtype)


def l2_normalize(x):
    return x * lax.rsqrt(jnp.sum(x * x, axis=-1, keepdims=True) + EPS)


def alibi_slopes():
    h = jnp.arange(1, N_HEADS_B + 1, dtype=jnp.float32)
    return jnp.exp2(-8.0 * h / N_HEADS_B).reshape(N_GROUPS, HEADS_PER_GROUP)


def gated_delta_rule(q, k, v, log_a, beta, s0):
    B, T, H, DK = q.shape
    DV = v.shape[-1]
    C = DELTA_CHUNK if T % DELTA_CHUNK == 0 else T
    N = T // C
    f32 = jnp.float32

    def blocks(a):
        a = a.astype(f32).reshape((B, N, C, H) + a.shape[3:])
        return jnp.moveaxis(a, (1, 3), (0, 2))

    q, k, v, g, bt = blocks(q), blocks(k), blocks(v), blocks(log_a), blocks(beta)
    G = jnp.cumsum(g, axis=-1)
    causal = jnp.tril(jnp.ones((C, C), dtype=bool))
    strict = jnp.tril(jnp.ones((C, C), dtype=bool), -1)
    decay = jnp.exp(jnp.where(causal, G[..., :, None] - G[..., None, :], -jnp.inf))
    kb = k * bt[..., None]
    a_mat = jnp.where(strict, jnp.einsum('nbhid,nbhjd->nbhij', kb, k) * decay, 0.0)
    eye = jnp.eye(C, dtype=f32)
    rhs = jnp.concatenate([kb * jnp.exp(G)[..., None], v * bt[..., None]], axis=-1)
    wu = lax.linalg.triangular_solve(eye + a_mat, rhs, left_side=True, lower=True)
    w, u = wu[..., :DK], wu[..., DK:]
    qk = jnp.einsum('nbhid,nbhjd->nbhij', q, k) * decay
    qg = q * jnp.exp(G)[..., None]
    kd = k * jnp.exp(G[..., -1:] - G)[..., None]
    g_last = jnp.exp(G[..., -1])

    def step(S, xs):
        w_c, u_c, qk_c, qg_c, kd_c, gl_c = xs
        v_new = u_c - jnp.einsum('bhck,bhkv->bhcv', w_c, S)
        o_c = jnp.einsum('bhck,bhkv->bhcv', qg_c, S) + jnp.einsum('bhij,bhjv->bhiv', qk_c, v_new)
        S = S * gl_c[..., None, None] + jnp.einsum('bhck,bhcv->bhkv', kd_c, v_new)
        return S, o_c

    S, o = lax.scan(step, s0.astype(f32), (w, u, qk, qg, kd, g_last))
    o = jnp.moveaxis(o, (0, 2), (1, 3)).reshape(B, T, H, DV)
    return o, S


def dilated_group_attention(q, kv_src, p0, window, dilation, slopes):
    B, T, H, D = q.shape
    qb = Q_BLOCK if T % Q_BLOCK == 0 else T
    n_off = window // dilation + 1
    offs = jnp.arange(n_off) * dilation
    rows = jnp.arange(qb)
    local_idx = window + rows[:, None] - offs[None, :]
    bias = -slopes[:, None] * offs[None, :].astype(jnp.float32)
    scale = D ** -0.5

    def one_block(blk):
        start = blk * qb
        q_blk = lax.dynamic_slice_in_dim(q, start, qb, axis=1).astype(jnp.float32)
        win = lax.dynamic_slice_in_dim(kv_src, start, qb + window, axis=1)
        kv = jnp.take(win, local_idx, axis=1).astype(jnp.float32)
        s = jnp.einsum('bqhd,bqkhd->bhqk', q_blk, kv[:, :, :, 0]) * scale + bias[None, :, None, :]
        valid = (p0 + start + rows[:, None] - offs[None, :]) >= 0
        s = jnp.where(valid, s, -jnp.inf)
        lse = jax.nn.logsumexp(s, axis=-1)
        p = jnp.exp(s - lse[..., None])
        o = jnp.einsum('bhqk,bqkhd->bqhd', p, kv[:, :, :, 1])
        return o, jnp.swapaxes(lse, 1, 2)

    o, lse = lax.map(one_block, jnp.arange(T // qb))
    o = jnp.moveaxis(o, 0, 1).reshape(B, T, H, D)
    lse = jnp.moveaxis(lse, 0, 1).reshape(B, T, H)
    return o, lse


def hybrid_layer(x, p0, win_past, conv_buf, s0, norm1, w_in, conv_w, a_log, dt_bias, gnorm_a,
                 w_br_a, w_br_b, w_out, norm2, w_up, w_down):
    B, T, _ = x.shape
    f32 = jnp.float32
    h = rms_norm(x, norm1)
    proj = jnp.einsum('btd,de->bte', h, w_in)
    split_at = [int(i) for i in np.cumsum(IN_SPLITS)[:-1]]
    qkv_a, z, a_raw, b_raw, q_b, k_b, v_b, gate_a, gate_b = jnp.split(proj, split_at, axis=-1)

    src = jnp.concatenate([conv_buf.astype(x.dtype), qkv_a], axis=1)
    conv = sum(src[:, j:j + T] * conv_w[j] for j in range(CONV_W))
    new_conv = src[:, T:]
    qa, ka, va = jnp.split(jax.nn.silu(conv), 3, axis=-1)
    head_shape = (B, T, N_HEADS_A, HEAD_DIM_A)
    qa = l2_normalize(qa.reshape(head_shape).astype(f32)) * HEAD_DIM_A ** -0.5
    ka = l2_normalize(ka.reshape(head_shape).astype(f32))
    va = va.reshape(head_shape)
    log_a = -jnp.exp(a_log.astype(f32)) * jax.nn.softplus(a_raw.astype(f32) + dt_bias.astype(f32))
    beta = jax.nn.sigmoid(b_raw.astype(f32))
    o_a, s_new = gated_delta_rule(qa, ka, va, log_a, beta, s0)
    o_a = rms_norm(o_a, gnorm_a) * jax.nn.silu(z.reshape(head_shape).astype(f32))
    o_a = o_a.reshape(B, T, WIDTH_A).astype(x.dtype)

    slopes = alibi_slopes()
    grp_shape = (B, T, N_GROUPS, HEADS_PER_GROUP, HEAD_DIM_B)
    qs, ks, vs = q_b.reshape(grp_shape), k_b.reshape(grp_shape), v_b.reshape(grp_shape)
    outs, lses, new_win = [], [], []
    for g in range(N_GROUPS):
        window = WINDOWS[g]
        past = win_past[g].astype(x.dtype)
        new = jnp.stack([ks[:, :, g], vs[:, :, g]], axis=2)
        pad = jnp.zeros((B, window - past.shape[1]) + new.shape[2:], x.dtype)
        kv_src = jnp.concatenate([pad, past, new], axis=1)
        o_g, lse_g = dilated_group_attention(qs[:, :, g], kv_src, p0, window, DILATIONS[g], slopes[g])
        outs.append(o_g)
        lses.append(lse_g)
        keep = min(window, p0 + T)
        new_win.append(kv_src[:, -keep:])
    wts = jax.nn.softmax(jnp.stack(lses), axis=0)
    o_b = jnp.sum(wts[..., None] * jnp.stack(outs), axis=0)
    o_b = o_b.reshape(B, T, WIDTH_B_OUT).astype(x.dtype)

    merged = jax.nn.sigmoid(gate_a) * (o_a @ w_br_a) + jax.nn.sigmoid(gate_b) * (o_b @ w_br_b)
    x = x + merged @ w_out
    h2 = rms_norm(x, norm2)
    x = x + jnp.square(jax.nn.relu(h2 @ w_up)) @ w_down
    return x, new_win, s_new.astype(x.dtype), new_conv


def run_trunk(x, p0, win_caches, delta_cache, conv_cache, layer_params, final_norm):
    win_out = [[] for _ in range(N_GROUPS)]
    delta_out, conv_out = [], []
    for l in range(DEPTH):
        past = [c[l] for c in win_caches]
        x, wins, s_new, c_new = hybrid_layer(x, p0, past, conv_cache[l], delta_cache[l],
                                             *[p[l] for p in layer_params])
        for g in range(N_GROUPS):
            win_out[g].append(wins[g])
        delta_out.append(s_new)
        conv_out.append(c_new)
    return (rms_norm(x, final_norm), [jnp.stack(w) for w in win_out],
            jnp.stack(delta_out), jnp.stack(conv_out))


def setup_inputs(seed: int = 0) -> dict:
    key = jax.random.key(seed)
    ks = jax.random.split(key, 20)
    f32 = jnp.float32

    def nrm(k, shape, s=1.0):
        return jax.random.normal(k, shape, f32) * s

    wl = [min(w, PAST_LEN) for w in WINDOWS]
    dt = jnp.exp(jax.random.uniform(ks[11], (DEPTH, N_HEADS_A), f32,
                                    float(np.log(1e-3)), float(np.log(1e-1))))
    return {
        'x_prompt': nrm(ks[0], (BATCH, SEQ, D_MODEL)),
        'x_sample': nrm(ks[1], (DEC_BATCH, DEC_SEQ, D_MODEL)),
        'cache_win0': nrm(ks[2], (DEPTH, DEC_BATCH, wl[0], 2, HEADS_PER_GROUP, HEAD_DIM_B)),
        'cache_win1': nrm(ks[3], (DEPTH, DEC_BATCH, wl[1], 2, HEADS_PER_GROUP, HEAD_DIM_B)),
        'cache_win2': nrm(ks[4], (DEPTH, DEC_BATCH, wl[2], 2, HEADS_PER_GROUP, HEAD_DIM_B)),
        'state_delta': nrm(ks[5], (DEPTH, DEC_BATCH, N_HEADS_A, HEAD_DIM_A, HEAD_DIM_A), 0.3),
        'state_conv': nrm(ks[6], (DEPTH, DEC_BATCH, CONV_W - 1, 3 * WIDTH_A)),
        'norm1': 1.0 + nrm(ks[7], (DEPTH, D_MODEL), 0.02),
        'w_in': nrm(ks[8], (DEPTH, D_MODEL, N_IN), D_MODEL ** -0.5),
        'conv_w': nrm(ks[9], (DEPTH, CONV_W, 3 * WIDTH_A), CONV_W ** -0.5),
        'a_log': jnp.log(jax.random.uniform(ks[10], (DEPTH, N_HEADS_A), f32, 1.0, 16.0)),
        'dt_bias': dt + jnp.log(-jnp.expm1(-dt)),
        'gnorm_a': 1.0 + nrm(ks[12], (DEPTH, HEAD_DIM_A), 0.02),
        'w_br_a': nrm(ks[13], (DEPTH, WIDTH_A, D_MODEL), WIDTH_A ** -0.5),
        'w_br_b': nrm(ks[14], (DEPTH, WIDTH_B_OUT, D_MODEL), WIDTH_B_OUT ** -0.5),
        'w_out': nrm(ks[15], (DEPTH, D_MODEL, D_MODEL), D_MODEL ** -0.5),
        'norm2': 1.0 + nrm(ks[16], (DEPTH, D_MODEL), 0.02),
        'w_up': nrm(ks[17], (DEPTH, D_MODEL, D_FF), D_MODEL ** -0.5),
        'w_down': nrm(ks[18], (DEPTH, D_FF, D_MODEL), D_FF ** -0.5),
        'final_norm': 1.0 + nrm(ks[19], (D_MODEL,), 0.02),
    }


def reference(x_prompt, x_sample, cache_win0, cache_win1, cache_win2, state_delta, state_conv,
              norm1, w_in, conv_w, a_log, dt_bias, gnorm_a, w_br_a, w_br_b, w_out, norm2,
              w_up, w_down, final_norm):
    layer_params = (norm1, w_in, conv_w, a_log, dt_bias, gnorm_a, w_br_a, w_br_b, w_out,
                    norm2, w_up, w_down)
    dt = x_prompt.dtype
    bp = x_prompt.shape[0]
    zero_win = jnp.zeros((DEPTH, bp, 0, 2, HEADS_PER_GROUP, HEAD_DIM_B), dt)
    zero_delta = jnp.zeros((DEPTH, bp, N_HEADS_A, HEAD_DIM_A, HEAD_DIM_A), dt)
    zero_conv = jnp.zeros((DEPTH, bp, CONV_W - 1, 3 * WIDTH_A), dt)
    y_prompt, wins_p, delta_p, conv_p = run_trunk(
        x_prompt, 0, (zero_win, zero_win, zero_win), zero_delta, zero_conv, layer_params, final_norm)
    y_sample, wins_s, delta_s, conv_s = run_trunk(
        x_sample, PAST_LEN, (cache_win0, cache_win1, cache_win2), state_delta, state_conv,
        layer_params, final_norm)
    return (y_prompt, y_sample, wins_p[0], wins_p[1], wins_p[2], delta_p, conv_p,
            wins_s[0], wins_s[1], wins_s[2], delta_s, conv_s)
```

```python
import functools

import jax
import jax.numpy as jnp
from jax import lax
from jax.experimental import pallas as pl
from jax.experimental.pallas import tpu as pltpu

f32 = jnp.float32
bf16 = jnp.bfloat16

D_MODEL = 1024
N_HEADS_A = 4
HEAD_DIM_A = 128
WIDTH_A = N_HEADS_A * HEAD_DIM_A
CONV_W = 4
WINDOWS = (128, 512, 2048)
DILATIONS = (1, 4, 16)
N_GROUPS = 3
HEADS_PER_GROUP = 4
HEAD_DIM_B = 64
N_HEADS_B = N_GROUPS * HEADS_PER_GROUP
WIDTH_G = HEADS_PER_GROUP * HEAD_DIM_B
D_FF = 4 * D_MODEL
EPS = 1e-6
N_OFF = 128
NEG = -1e30

SUBLANES = 8
LANES = 128
VMEM_LIMIT = 56 * 1024 * 1024

SEG_WIDTHS = (3 * WIDTH_A, WIDTH_A, WIDTH_G, WIDTH_G, WIDTH_G,
              2 * WIDTH_G, 2 * WIDTH_G, 2 * WIDTH_G, 2 * D_MODEL, LANES)
N_PERM = sum(SEG_WIDTHS)

NN = (((1,), (0,)), ((), ()))
NT = (((1,), (1,)), ((), ()))
TN = (((0,), (0,)), ((), ()))


def _slopes(g):
    return tuple(2.0 ** (-8.0 * (g * HEADS_PER_GROUP + h + 1) / N_HEADS_B)
                 for h in range(HEADS_PER_GROUP))


def _dot(a, b, dims=NN):
    return lax.dot_general(a, b, dims, preferred_element_type=f32)


def _mm1(a, b, dims=NN):
    return _dot(a.astype(bf16), b.astype(bf16), dims)


def _split(a):
    hi = a.astype(bf16)
    lo = (a - hi.astype(f32)).astype(bf16)
    return hi, lo


def _mm3(a, b, dims=NN):
    ah, al = _split(a)
    bh, bl = _split(b)
    return _dot(ah, bh, dims) + _dot(ah, bl, dims) + _dot(al, bh, dims)


def _sigmoid(x):
    return 1.0 / (1.0 + jnp.exp(-x))


def _softplus(x):
    return jnp.maximum(x, 0.0) + jnp.log1p(jnp.exp(-jnp.abs(x)))


def _rms(x, g):
    return x * lax.rsqrt(jnp.mean(x * x, axis=-1, keepdims=True) + EPS) * g


def _inproj_body(x_ref, g_ref, w_ref, *out_refs):
    h = _rms(x_ref[...], g_ref[...]).astype(bf16)
    off = 0
    for o_ref, width in zip(out_refs, SEG_WIDTHS):
        o_ref[...] = _dot(h, w_ref[:, off:off + width])
        off += width


def _inproj(x, g, w, tm=256):
    m = x.shape[0]
    tm = min(tm, m)
    assert m % tm == 0
    return pl.pallas_call(
        _inproj_body,
        grid=(m // tm,),
        in_specs=[pl.BlockSpec((tm, D_MODEL), lambda i: (i, 0)),
                  pl.BlockSpec((1, D_MODEL), lambda i: (0, 0)),
                  pl.BlockSpec((D_MODEL, N_PERM), lambda i: (0, 0),
                               pipeline_mode=pl.Buffered(1))],
        out_specs=[pl.BlockSpec((tm, wd), lambda i: (i, 0)) for wd in SEG_WIDTHS],
        out_shape=[jax.ShapeDtypeStruct((m, wd), f32) for wd in SEG_WIDTHS],
        compiler_params=pltpu.CompilerParams(
            dimension_semantics=("arbitrary",), vmem_limit_bytes=VMEM_LIMIT),
        name="inproj",
    )(x, g, w)


def _permute_w_in(w):
    o_z = 3 * WIDTH_A
    o_ab = o_z + WIDTH_A
    o_q = o_ab + 2 * N_HEADS_A
    o_k = o_q + N_GROUPS * WIDTH_G
    o_v = o_k + N_GROUPS * WIDTH_G
    o_gate = o_v + N_GROUPS * WIDTH_G
    parts = [w[:, :o_ab], w[:, o_q:o_k]]
    for g in range(N_GROUPS):
        parts.append(w[:, o_k + g * WIDTH_G:o_k + (g + 1) * WIDTH_G])
        parts.append(w[:, o_v + g * WIDTH_G:o_v + (g + 1) * WIDTH_G])
    parts.append(w[:, o_gate:])
    parts.append(w[:, o_ab:o_q])
    parts.append(jnp.zeros((w.shape[0], LANES - 2 * N_HEADS_A), w.dtype))
    return jnp.concatenate(parts, axis=1).astype(bf16)


CHUNK = 128
INV_BASE = 16


def _tri_inverse(a, row, col):
    eye = jnp.where(row == col, 1.0, 0.0)
    blk = lambda idx, size: jnp.right_shift(idx, size.bit_length() - 1)
    ad = jnp.where(blk(row, INV_BASE) == blk(col, INV_BASE), a, 0.0)
    p = eye - ad
    x = ad
    n = 2
    while n < INV_BASE:
        x = _mm3(x, x)
        p = p + _mm3(p, x)
        n *= 2
    s = INV_BASE
    while s < CHUNK:
        e = jnp.where((blk(row, 2 * s) == blk(col, 2 * s)) & (blk(row, s) != blk(col, s)), a, 0.0)
        p = p - _mm3(_mm3(p, e), p)
        s *= 2
    return p


def _delta_prompt_body(qkv_ref, ab_ref, z_ref, cw_ref, nega_ref, dtb_ref, gn_ref, halo_ref, s0_ref,
                       o_ref, s_out_ref, xs_ref, s_ref):
    c = pl.program_id(1)

    @pl.when(c == 0)
    def _():
        xs_ref[pl.ds(0, SUBLANES), :] = halo_ref[0]
        s_ref[...] = s0_ref[0]

    xs_ref[pl.ds(SUBLANES, CHUNK), :] = qkv_ref[0]
    conv = xs_ref[pl.ds(SUBLANES - CONV_W + 1, CHUNK), :] * cw_ref[0:1, :]
    for j in range(1, CONV_W):
        conv = conv + xs_ref[pl.ds(SUBLANES - CONV_W + 1 + j, CHUNK), :] * cw_ref[j:j + 1, :]
    xs_ref[pl.ds(0, SUBLANES), :] = xs_ref[pl.ds(CHUNK, SUBLANES), :]
    cs = conv * _sigmoid(conv)

    abt = ab_ref[0]
    log_a = nega_ref[...] * _softplus(abt + dtb_ref[...])
    beta_all = _sigmoid(abt)

    row = lax.broadcasted_iota(jnp.int32, (CHUNK, CHUNK), 0)
    col = lax.broadcasted_iota(jnp.int32, (CHUNK, CHUNK), 1)
    causal = row >= col
    strict = row > col
    ltri = jnp.where(causal, 1.0, 0.0).astype(bf16)

    for h in range(N_HEADS_A):
        lo = h * HEAD_DIM_A
        q = cs[:, lo:lo + HEAD_DIM_A]
        k = cs[:, WIDTH_A + lo:WIDTH_A + lo + HEAD_DIM_A]
        v = cs[:, 2 * WIDTH_A + lo:2 * WIDTH_A + lo + HEAD_DIM_A]
        q = q * (lax.rsqrt(jnp.sum(q * q, axis=-1, keepdims=True) + EPS) * HEAD_DIM_A ** -0.5)
        k = k * lax.rsqrt(jnp.sum(k * k, axis=-1, keepdims=True) + EPS)
        beta = beta_all[:, N_HEADS_A + h:N_HEADS_A + h + 1]
        gb = jnp.broadcast_to(log_a[:, h:h + 1], (CHUNK, HEAD_DIM_A))
        g1 = gb.astype(bf16)
        r1 = gb - g1.astype(f32)
        g2 = r1.astype(bf16)
        g3 = (r1 - g2.astype(f32)).astype(bf16)
        cum = _dot(ltri, g1) + _dot(ltri, g2) + _dot(ltri, g3)
        cum_t = cum.T
        decay = jnp.exp(jnp.where(causal, cum - cum_t, NEG))
        e_g = jnp.exp(cum)
        g_last = cum[CHUNK - 1:CHUNK, :]
        kb = k * beta
        a_mat = jnp.where(strict, _mm3(kb, k, NT) * decay, 0.0)
        t_inv = _tri_inverse(a_mat, row, col)
        w = _mm3(t_inv, kb * e_g)
        u = _mm3(t_inv, v * beta)
        qk = _mm3(q, k, NT) * decay
        qg = q * e_g
        kd = k * jnp.exp(g_last - cum)
        s_h = s_ref[h]
        v_new = u - _mm3(w, s_h)
        o = _mm3(qg, s_h) + _mm3(qk, v_new)
        s_ref[h] = s_h * jnp.exp(g_last) + _mm3(kd, v_new, TN)
        zz = z_ref[0, :, lo:lo + HEAD_DIM_A]
        o_ref[0, :, lo:lo + HEAD_DIM_A] = _rms(o, gn_ref[...]) * (zz * _sigmoid(zz))

    @pl.when(c == pl.num_programs(1) - 1)
    def _():
        s_out_ref[0] = s_ref[...]


def _delta_prompt(qkv, ab, z, cw, nega, dtb, gn, halo, s0):
    b, t, _ = qkv.shape
    n = t // CHUNK
    return pl.pallas_call(
        _delta_prompt_body,
        grid=(b, n),
        in_specs=[pl.BlockSpec((1, CHUNK, 3 * WIDTH_A), lambda i, c: (i, c, 0)),
                  pl.BlockSpec((1, CHUNK, LANES), lambda i, c: (i, c, 0)),
                  pl.BlockSpec((1, CHUNK, WIDTH_A), lambda i, c: (i, c, 0)),
                  pl.BlockSpec((CONV_W, 3 * WIDTH_A), lambda i, c: (0, 0)),
                  pl.BlockSpec((1, LANES), lambda i, c: (0, 0)),
                  pl.BlockSpec((1, LANES), lambda i, c: (0, 0)),
                  pl.BlockSpec((1, HEAD_DIM_A), lambda i, c: (0, 0)),
                  pl.BlockSpec((1, SUBLANES, 3 * WIDTH_A), lambda i, c: (i, 0, 0)),
                  pl.BlockSpec((1, N_HEADS_A, HEAD_DIM_A, HEAD_DIM_A), lambda i, c: (i, 0, 0, 0))],
        out_specs=[pl.BlockSpec((1, CHUNK, WIDTH_A), lambda i, c: (i, c, 0)),
                   pl.BlockSpec((1, N_HEADS_A, HEAD_DIM_A, HEAD_DIM_A), lambda i, c: (i, 0, 0, 0))],
        out_shape=[jax.ShapeDtypeStruct((b, t, WIDTH_A), f32),
                   jax.ShapeDtypeStruct((b, N_HEADS_A, HEAD_DIM_A, HEAD_DIM_A), f32)],
        scratch_shapes=[pltpu.VMEM((CHUNK + SUBLANES, 3 * WIDTH_A), f32),
                        pltpu.VMEM((N_HEADS_A, HEAD_DIM_A, HEAD_DIM_A), f32)],
        compiler_params=pltpu.CompilerParams(
            dimension_semantics=("arbitrary", "arbitrary"), vmem_limit_bytes=VMEM_LIMIT),
        name="delta_prompt",
    )(qkv, ab, z, cw, nega, dtb, gn, halo, s0)


def _delta_sample_body(src_ref, ab_ref, z_ref, cw_ref, nega_ref, dtb_ref, gn_ref, s0_ref,
                       o_ref, s_out_ref, *, nb, t):
    rowi = lax.broadcasted_iota(jnp.int32, (SUBLANES, LANES), 0)
    real = rowi < t
    for b in range(nb):
        src = src_ref[b]
        off = SUBLANES - t - (CONV_W - 1)
        conv = pltpu.roll(src, (SUBLANES - off) % SUBLANES, axis=0) * cw_ref[0:1, :]
        for j in range(1, CONV_W):
            conv = conv + pltpu.roll(src, (SUBLANES - off - j) % SUBLANES, axis=0) * cw_ref[j:j + 1, :]
        cs = conv * _sigmoid(conv)
        abt = ab_ref[b]
        log_a = jnp.where(real, nega_ref[...] * _softplus(abt + dtb_ref[...]), 0.0)
        beta_all = jnp.where(real, _sigmoid(abt), 0.0)
        for h in range(N_HEADS_A):
            lo = h * HEAD_DIM_A
            q = cs[:, lo:lo + HEAD_DIM_A]
            k = cs[:, WIDTH_A + lo:WIDTH_A + lo + HEAD_DIM_A]
            v = cs[:, 2 * WIDTH_A + lo:2 * WIDTH_A + lo + HEAD_DIM_A]
            q = q * (lax.rsqrt(jnp.sum(q * q, axis=-1, keepdims=True) + EPS) * HEAD_DIM_A ** -0.5)
            k = k * lax.rsqrt(jnp.sum(k * k, axis=-1, keepdims=True) + EPS)
            beta = beta_all[:, N_HEADS_A + h:N_HEADS_A + h + 1]
            gb = jnp.broadcast_to(log_a[:, h:h + 1], (SUBLANES, LANES))
            cum = jnp.zeros((SUBLANES, LANES), f32)
            for j in range(t):
                cum = cum + jnp.where(rowi >= j, gb[j:j + 1, :], 0.0)
            g_last = cum[t - 1:t, :]
            e_g = jnp.exp(cum)
            kb = k * beta
            w = kb * e_g
            u = v * beta
            qk_cols = []
            for j in range(t):
                dec_j = jnp.exp(jnp.where(rowi >= j, cum - cum[j:j + 1, :], NEG))
                kj = k[j:j + 1, :]
                qk_cols.append(jnp.sum(q * kj, axis=-1, keepdims=True) * dec_j)
                if j < t - 1:
                    a_j = jnp.where(rowi > j, jnp.sum(kb * kj, axis=-1, keepdims=True) * dec_j, 0.0)
                    w = w - a_j * w[j:j + 1, :]
                    u = u - a_j * u[j:j + 1, :]
            qg = q * e_g
            kd = k * jnp.exp(g_last - cum)
            s_h = s0_ref[b, h]
            v_new = u - _mm3(w, s_h)
            o = _mm3(qg, s_h)
            for j in range(t):
                o = o + qk_cols[j] * v_new[j:j + 1, :]
            s_out_ref[b, h] = s_h * jnp.exp(g_last) + _mm3(kd, v_new, TN)
            zz = z_ref[b, :, lo:lo + HEAD_DIM_A]
            o_ref[b, :, lo:lo + HEAD_DIM_A] = _rms(o, gn_ref[...]) * (zz * _sigmoid(zz))


def _delta_sample(src, ab, z, cw, nega, dtb, gn, s0, t, nb=4):
    b = src.shape[0]
    return pl.pallas_call(
        functools.partial(_delta_sample_body, nb=nb, t=t),
        grid=(b // nb,),
        in_specs=[pl.BlockSpec((nb, SUBLANES, 3 * WIDTH_A), lambda i: (i, 0, 0)),
                  pl.BlockSpec((nb, SUBLANES, LANES), lambda i: (i, 0, 0)),
                  pl.BlockSpec((nb, SUBLANES, WIDTH_A), lambda i: (i, 0, 0)),
                  pl.BlockSpec((CONV_W, 3 * WIDTH_A), lambda i: (0, 0)),
                  pl.BlockSpec((1, LANES), lambda i: (0, 0)),
                  pl.BlockSpec((1, LANES), lambda i: (0, 0)),
                  pl.BlockSpec((1, HEAD_DIM_A), lambda i: (0, 0)),
                  pl.BlockSpec((nb, N_HEADS_A, HEAD_DIM_A, HEAD_DIM_A), lambda i: (i, 0, 0, 0))],
        out_specs=[pl.BlockSpec((nb, SUBLANES, WIDTH_A), lambda i: (i, 0, 0)),
                   pl.BlockSpec((nb, N_HEADS_A, HEAD_DIM_A, HEAD_DIM_A), lambda i: (i, 0, 0, 0))],
        out_shape=[jax.ShapeDtypeStruct((b, SUBLANES, WIDTH_A), f32),
                   jax.ShapeDtypeStruct((b, N_HEADS_A, HEAD_DIM_A, HEAD_DIM_A), f32)],
        compiler_params=pltpu.CompilerParams(
            dimension_semantics=("arbitrary",), vmem_limit_bytes=VMEM_LIMIT),
        name="delta_sample",
    )(src, ab, z, cw, nega, dtb, gn, s0)


def _attn_prompt_body(q_ref, kvp_ref, kvc_ref, o_ref, l_ref, kv_scr, *, dil, slopes, tq):
    i = pl.program_id(2)
    kv_scr[pl.ds(0, N_OFF), :] = kvp_ref[0].astype(bf16)
    kv_scr[pl.ds(N_OFF, tq), :] = kvc_ref[0].astype(bf16)
    a = lax.broadcasted_iota(jnp.int32, (N_OFF, 2 * N_OFF), 0)
    cc = lax.broadcasted_iota(jnp.int32, (N_OFF, 2 * N_OFF), 1)
    delta = a - cc + N_OFF
    in_win = (delta >= 0) & (delta <= N_OFF)
    dist = delta.astype(f32) * float(dil)
    head = jnp.right_shift(lax.broadcasted_iota(jnp.int32, (N_OFF, WIDTH_G), 1),
                           HEAD_DIM_B.bit_length() - 1)
    for u in range(tq // N_OFF):
        q = q_ref[0, pl.ds(u * N_OFF, N_OFF), :] * HEAD_DIM_B ** -0.5
        kvs = kv_scr[pl.ds(u * N_OFF, 2 * N_OFF), :]
        k = kvs[:, :WIDTH_G]
        v = kvs[:, WIDTH_G:]
        key_step = i * tq + (u - 1) * N_OFF + cc
        mask = in_win & (key_step >= 0)
        o_acc = jnp.zeros((N_OFF, WIDTH_G), f32)
        l_acc = jnp.zeros((N_OFF, WIDTH_G), f32)
        for h in range(HEADS_PER_GROUP):
            qm = jnp.where(head == h, q, 0.0).astype(bf16)
            s = _dot(qm, k, NT) - slopes[h] * dist
            s = jnp.where(mask, s, NEG)
            m = jnp.max(s, axis=-1, keepdims=True)
            p = jnp.exp(s - m)
            den = jnp.sum(p, axis=-1, keepdims=True)
            pv = _dot(p.astype(bf16), v)
            o_acc = jnp.where(head == h, pv / den, o_acc)
            l_acc = jnp.where(head == h, m + jnp.log(den), l_acc)
        o_ref[0, pl.ds(u * N_OFF, N_OFF), :] = o_acc
        l_ref[0, pl.ds(u * N_OFF, N_OFF), :] = l_acc


def _attn_prompt(q, kv, g):
    b, t, _ = q.shape
    dil = DILATIONS[g]
    ts = t // dil
    tq = min(512, ts)
    qv = q.reshape(b, ts, dil * WIDTH_G)
    kvv = kv.reshape(b, ts, dil * 2 * WIDTH_G)
    sub = tq // N_OFF
    o, l = pl.pallas_call(
        functools.partial(_attn_prompt_body, dil=dil, slopes=_slopes(g), tq=tq),
        grid=(b, dil, ts // tq),
        in_specs=[pl.BlockSpec((1, tq, WIDTH_G), lambda bi, r, i: (bi, i, r)),
                  pl.BlockSpec((1, N_OFF, 2 * WIDTH_G),
                               lambda bi, r, i: (bi, jnp.maximum(i * sub - 1, 0), r)),
                  pl.BlockSpec((1, tq, 2 * WIDTH_G), lambda bi, r, i: (bi, i, r))],
        out_specs=[pl.BlockSpec((1, tq, WIDTH_G), lambda bi, r, i: (bi, i, r)),
                   pl.BlockSpec((1, tq, WIDTH_G), lambda bi, r, i: (bi, i, r))],
        out_shape=[jax.ShapeDtypeStruct((b, ts, dil * WIDTH_G), f32),
                   jax.ShapeDtypeStruct((b, ts, dil * WIDTH_G), f32)],
        scratch_shapes=[pltpu.VMEM((tq + N_OFF, 2 * WIDTH_G), bf16)],
        compiler_params=pltpu.CompilerParams(
            dimension_semantics=("arbitrary", "arbitrary", "arbitrary"),
            vmem_limit_bytes=VMEM_LIMIT),
        name=f"attn_prompt_g{g}",
    )(qv, kvv, kvv)
    return o.reshape(b, t, WIDTH_G), l.reshape(b, t, WIDTH_G)


KV_CHUNKS = 2 * WIDTH_G // LANES
K_CHUNKS = WIDTH_G // LANES


def _attn_sample_body(q_ref, kvn_ref, c_ref, o_ref, l_ref, cn_ref, *, dil, slopes, nb, t, win):
    rows = KV_CHUNKS * win
    keep = KV_CHUNKS * (win - t)
    m_idx = lax.broadcasted_iota(jnp.int32, (SUBLANES, N_OFF), 1)
    lane = lax.broadcasted_iota(jnp.int32, (SUBLANES, LANES), 1)
    rowi = lax.broadcasted_iota(jnp.int32, (SUBLANES, LANES), 0)
    lane1 = lax.broadcasted_iota(jnp.int32, (1, LANES), 1)
    def one_sequence(b, carry):
        cn_ref[b, pl.ds(0, keep), :] = c_ref[b, pl.ds(rows - keep, keep), :]
        cn_ref[b, pl.ds(keep, KV_CHUNKS * t), :] = kvn_ref[b]
        for i in range(t):
            start = i if dil > 1 else 0
            steps = float((win + i - start) // dil) - m_idx.astype(f32)
            valid = steps <= float(N_OFF)
            new_rows = [n for n in range(i + 1) if (i - n) % dil == 0]
            for c in range(K_CHUNKS):
                qrow = q_ref[b, pl.ds(K_CHUNKS * i + c, 1), :] * HEAD_DIM_B ** -0.5
                sel = ((rowi == 0) & (lane < HEAD_DIM_B)) | ((rowi == 1) & (lane >= HEAD_DIM_B))
                qm = jnp.where(sel, jnp.broadcast_to(qrow, (SUBLANES, LANES)), 0.0)
                slope = jnp.where(rowi[:, 0:1] == 0, slopes[2 * c], slopes[2 * c + 1])
                kc = c_ref[b, pl.ds(KV_CHUNKS * start + c, N_OFF, stride=KV_CHUNKS * dil), :]
                vc = c_ref[b, pl.ds(KV_CHUNKS * start + K_CHUNKS + c, N_OFF, stride=KV_CHUNKS * dil), :]
                s = _mm1(qm, kc, NT) - (slope * float(dil)) * steps
                s = jnp.where(valid, s, NEG)
                mx = jnp.max(s, axis=-1, keepdims=True)
                s_new = []
                for n in new_rows:
                    kn = kvn_ref[b, pl.ds(KV_CHUNKS * n + c, 1), :]
                    sn = jnp.sum(qm * kn, axis=-1, keepdims=True) - slope * float(i - n)
                    s_new.append(sn)
                    mx = jnp.maximum(mx, sn)
                p = jnp.exp(s - mx)
                den = jnp.sum(p, axis=-1, keepdims=True)
                acc = _mm1(p, vc)
                for n, sn in zip(new_rows, s_new):
                    pn = jnp.exp(sn - mx)
                    den = den + pn
                    acc = acc + pn * kvn_ref[b, pl.ds(KV_CHUNKS * n + K_CHUNKS + c, 1), :]
                out = acc / den
                lse = jnp.broadcast_to(mx + jnp.log(den), (SUBLANES, LANES))
                first = lane1 < HEAD_DIM_B
                o_ref[b, pl.ds(K_CHUNKS * i + c, 1), :] = jnp.where(first, out[0:1, :], out[1:2, :])
                l_ref[b, pl.ds(K_CHUNKS * i + c, 1), :] = jnp.where(first, lse[0:1, :], lse[1:2, :])
        return carry

    lax.fori_loop(0, nb, one_sequence, 0)


def _attn_sample(q, kvn, cache, g, t):
    bsz, win = cache.shape[0], cache.shape[1]
    dil = DILATIONS[g]
    nb = max(1, min(16, 2048 // win))
    qv = q.reshape(bsz, K_CHUNKS * t, LANES)
    kvv = kvn.reshape(bsz, KV_CHUNKS * t, LANES)
    cv = cache.reshape(bsz, KV_CHUNKS * win, LANES)
    o, l, cn = pl.pallas_call(
        functools.partial(_attn_sample_body, dil=dil, slopes=_slopes(g), nb=nb, t=t, win=win),
        grid=(bsz // nb,),
        in_specs=[pl.BlockSpec((nb, K_CHUNKS * t, LANES), lambda i: (i, 0, 0)),
                  pl.BlockSpec((nb, KV_CHUNKS * t, LANES), lambda i: (i, 0, 0)),
                  pl.BlockSpec((nb, KV_CHUNKS * win, LANES), lambda i: (i, 0, 0))],
        out_specs=[pl.BlockSpec((nb, K_CHUNKS * t, LANES), lambda i: (i, 0, 0)),
                   pl.BlockSpec((nb, K_CHUNKS * t, LANES), lambda i: (i, 0, 0)),
                   pl.BlockSpec((nb, KV_CHUNKS * win, LANES), lambda i: (i, 0, 0))],
        out_shape=[jax.ShapeDtypeStruct((bsz, K_CHUNKS * t, LANES), f32),
                   jax.ShapeDtypeStruct((bsz, K_CHUNKS * t, LANES), f32),
                   jax.ShapeDtypeStruct((bsz, KV_CHUNKS * win, LANES), f32)],
        compiler_params=pltpu.CompilerParams(
            dimension_semantics=("arbitrary",), vmem_limit_bytes=VMEM_LIMIT),
        name=f"attn_sample_g{g}",
    )(qv, kvv, cv)
    return (o.reshape(bsz * t, WIDTH_G), l.reshape(bsz * t, WIDTH_G),
            cn.reshape(cache.shape))


FF_CHUNK = 512


def _tail_body(x_ref, oa_ref, o0_ref, o1_ref, o2_ref, l0_ref, l1_ref, l2_ref, gate_ref,
               wa_ref, wb_ref, wo_ref, n2_ref, wu_ref, wd_ref, fn_ref, y_ref, *, final):
    l0, l1, l2 = l0_ref[...], l1_ref[...], l2_ref[...]
    lm = jnp.maximum(jnp.maximum(l0, l1), l2)
    e0, e1, e2 = jnp.exp(l0 - lm), jnp.exp(l1 - lm), jnp.exp(l2 - lm)
    den = e0 + e1 + e2
    ob = (e0 / den) * o0_ref[...] + (e1 / den) * o1_ref[...] + (e2 / den) * o2_ref[...]
    ya = _mm1(oa_ref[...], wa_ref[...])
    yb = _mm1(ob, wb_ref[...])
    merged = _sigmoid(gate_ref[:, :D_MODEL]) * ya + _sigmoid(gate_ref[:, D_MODEL:]) * yb
    x1 = x_ref[...] + _mm1(merged, wo_ref[...])
    h2 = _rms(x1, n2_ref[...]).astype(bf16)
    acc = x1
    for j in range(D_FF // FF_CHUNK):
        up = _dot(h2, wu_ref[:, j * FF_CHUNK:(j + 1) * FF_CHUNK])
        act = jnp.square(jnp.maximum(up, 0.0)).astype(bf16)
        acc = acc + _dot(act, wd_ref[j * FF_CHUNK:(j + 1) * FF_CHUNK, :])
    if final:
        acc = _rms(acc, fn_ref[...])
    y_ref[...] = acc


def _tail(x, oa, outs, lses, gate, wa, wb, wo, n2, wu, wd, fn, final, tm=256):
    m = x.shape[0]
    tm = min(tm, m)
    assert m % tm == 0
    tok = lambda wd_: pl.BlockSpec((tm, wd_), lambda i: (i, 0))
    const = lambda shape: pl.BlockSpec(shape, lambda i: (0, 0), pipeline_mode=pl.Buffered(1))
    return pl.pallas_call(
        functools.partial(_tail_body, final=final),
        grid=(m // tm,),
        in_specs=[tok(D_MODEL), tok(WIDTH_A)] + [tok(WIDTH_G)] * 6 + [tok(2 * D_MODEL),
                  const((WIDTH_A, D_MODEL)), const((WIDTH_G, D_MODEL)), const((D_MODEL, D_MODEL)),
                  const((1, D_MODEL)), const((D_MODEL, D_FF)), const((D_FF, D_MODEL)),
                  const((1, D_MODEL))],
        out_specs=tok(D_MODEL),
        out_shape=jax.ShapeDtypeStruct((m, D_MODEL), f32),
        compiler_params=pltpu.CompilerParams(
            dimension_semantics=("arbitrary",), vmem_limit_bytes=VMEM_LIMIT),
        name="tail",
    )(x, oa, *outs, *lses, gate, wa, wb, wo, n2, wu, wd, fn)


def _lane_row(vals):
    return jnp.zeros((1, LANES), f32).at[0, :vals.shape[0]].set(vals.astype(f32))


def _layer_params(l, norm1, w_in, conv_w, a_log, dt_bias, gnorm_a, w_br_a, w_br_b, w_out, norm2,
                  w_up, w_down):
    return dict(
        n1=norm1[l].reshape(1, D_MODEL), w_in=_permute_w_in(w_in[l]), cw=conv_w[l],
        nega=_lane_row(-jnp.exp(a_log[l].astype(f32))), dtb=_lane_row(dt_bias[l]),
        gn=gnorm_a[l].reshape(1, HEAD_DIM_A).astype(f32),
        wa=w_br_a[l].astype(bf16), wb=w_br_b[l].astype(bf16), wo=w_out[l].astype(bf16),
        n2=norm2[l].reshape(1, D_MODEL), wu=w_up[l].astype(bf16), wd=w_down[l].astype(bf16))


def _prompt_layer(x, p, fn, final):
    b, t, _ = x.shape
    xf = x.reshape(b * t, D_MODEL)
    qkv, z, q0, q1, q2, kv0, kv1, kv2, gate, ab = _inproj(xf, p["n1"], p["w_in"])
    qkv3 = qkv.reshape(b, t, 3 * WIDTH_A)
    halo = jnp.zeros((b, SUBLANES, 3 * WIDTH_A), f32)
    s0 = jnp.zeros((b, N_HEADS_A, HEAD_DIM_A, HEAD_DIM_A), f32)
    oa, s_new = _delta_prompt(qkv3, ab.reshape(b, t, LANES), z.reshape(b, t, WIDTH_A), p["cw"],
                              p["nega"], p["dtb"], p["gn"], halo, s0)
    outs, lses, wins = [], [], []
    for g, (qg, kvg) in enumerate(((q0, kv0), (q1, kv1), (q2, kv2))):
        o, l = _attn_prompt(qg.reshape(b, t, WIDTH_G), kvg.reshape(b, t, 2 * WIDTH_G), g)
        outs.append(o.reshape(b * t, WIDTH_G))
        lses.append(l.reshape(b * t, WIDTH_G))
        keep = min(WINDOWS[g], t)
        wins.append(kvg.reshape(b, t, 2, HEADS_PER_GROUP, HEAD_DIM_B)[:, t - keep:])
    y = _tail(xf, oa.reshape(b * t, WIDTH_A), outs, lses, gate, p["wa"], p["wb"], p["wo"], p["n2"],
              p["wu"], p["wd"], fn, final)
    return y.reshape(b, t, D_MODEL), wins, s_new, qkv3[:, t - (CONV_W - 1):]


def _sample_layer(x, p, fn, final, caches, s0, conv_buf):
    b, t, _ = x.shape
    xf = x.reshape(b * t, D_MODEL)
    qkv, z, q0, q1, q2, kv0, kv1, kv2, gate, ab = _inproj(xf, p["n1"], p["w_in"])
    qkv3 = qkv.reshape(b, t, 3 * WIDTH_A)
    pad = SUBLANES - t
    src = jnp.concatenate([jnp.zeros((b, SUBLANES - t - (CONV_W - 1), 3 * WIDTH_A), f32),
                           conv_buf.astype(f32), qkv3], axis=1)
    pad_rows = lambda a_: jnp.pad(a_, ((0, 0), (0, pad), (0, 0)))
    oa, s_new = _delta_sample(src, pad_rows(ab.reshape(b, t, LANES)),
                              pad_rows(z.reshape(b, t, WIDTH_A)), p["cw"], p["nega"], p["dtb"],
                              p["gn"], s0, t)
    oa = oa[:, :t].reshape(b * t, WIDTH_A)
    outs, lses, wins = [], [], []
    for g, (qg, kvg) in enumerate(((q0, kv0), (q1, kv1), (q2, kv2))):
        o, l, cn = _attn_sample(qg, kvg, caches[g], g, t)
        outs.append(o)
        lses.append(l)
        wins.append(cn)
    y = _tail(xf, oa, outs, lses, gate, p["wa"], p["wb"], p["wo"], p["n2"], p["wu"], p["wd"],
              fn, final)
    return y.reshape(b, t, D_MODEL), wins, s_new, qkv3[:, t - (CONV_W - 1):]


def kernel(x_prompt, x_sample, cache_win0, cache_win1, cache_win2, state_delta, state_conv, norm1,
           w_in, conv_w, a_log, dt_bias, gnorm_a, w_br_a, w_br_b, w_out, norm2, w_up, w_down,
           final_norm):
    depth = w_in.shape[0]
    assert x_sample.shape[1] + CONV_W - 1 <= SUBLANES
    fn = final_norm.reshape(1, D_MODEL)
    caches = (cache_win0, cache_win1, cache_win2)
    xp, xs = x_prompt, x_sample
    wins_p = [[] for _ in range(N_GROUPS)]
    wins_s = [[] for _ in range(N_GROUPS)]
    delta_p, conv_p, delta_s, conv_s = [], [], [], []
    for l in range(depth):
        p = _layer_params(l, norm1, w_in, conv_w, a_log, dt_bias, gnorm_a, w_br_a, w_br_b, w_out,
                          norm2, w_up, w_down)
        final = l == depth - 1
        xp, wp, sp, cp = _prompt_layer(xp, p, fn, final)
        xs, ws, ss, cs = _sample_layer(xs, p, fn, final, [c[l] for c in caches], state_delta[l],
                                       state_conv[l])
        for g in range(N_GROUPS):
            wins_p[g].append(wp[g])
            wins_s[g].append(ws[g])
        delta_p.append(sp)
        conv_p.append(cp)
        delta_s.append(ss)
        conv_s.append(cs)
    st = jnp.stack
    return (xp, xs, st(wins_p[0]), st(wins_p[1]), st(wins_p[2]), st(delta_p), st(conv_p),
            st(wins_s[0]), st(wins_s[1]), st(wins_s[2]), st(delta_s), st(conv_s))
```

```python
import functools

import jax
import jax.numpy as jnp
from jax import lax
from jax.experimental import pallas as pl
from jax.experimental.pallas import tpu as pltpu

f32 = jnp.float32
bf16 = jnp.bfloat16

D_MODEL = 1024
N_HEADS_A = 4
HEAD_DIM_A = 128
WIDTH_A = N_HEADS_A * HEAD_DIM_A
CONV_W = 4
WINDOWS = (128, 512, 2048)
DILATIONS = (1, 4, 16)
N_GROUPS = 3
HEADS_PER_GROUP = 4
HEAD_DIM_B = 64
N_HEADS_B = N_GROUPS * HEADS_PER_GROUP
WIDTH_G = HEADS_PER_GROUP * HEAD_DIM_B
D_FF = 4 * D_MODEL
EPS = 1e-6
N_OFF = 128
NEG = -1e30

SUBLANES = 8
LANES = 128
VMEM_LIMIT = 56 * 1024 * 1024

SEG_WIDTHS = (3 * WIDTH_A, WIDTH_A, WIDTH_G, WIDTH_G, WIDTH_G,
              2 * WIDTH_G, 2 * WIDTH_G, 2 * WIDTH_G, 2 * D_MODEL, LANES)
N_PERM = sum(SEG_WIDTHS)

NN = (((1,), (0,)), ((), ()))
NT = (((1,), (1,)), ((), ()))
TN = (((0,), (0,)), ((), ()))


def _slopes(g):
    return tuple(2.0 ** (-8.0 * (g * HEADS_PER_GROUP + h + 1) / N_HEADS_B)
                 for h in range(HEADS_PER_GROUP))


def _dot(a, b, dims=NN):
    return lax.dot_general(a, b, dims, preferred_element_type=f32)


def _mm1(a, b, dims=NN):
    return _dot(a.astype(bf16), b.astype(bf16), dims)


def _split(a):
    hi = a.astype(bf16)
    lo = (a - hi.astype(f32)).astype(bf16)
    return hi, lo


def _split3(a):
    hi = a.astype(bf16)
    r = a - hi.astype(f32)
    mid = r.astype(bf16)
    lo = (r - mid.astype(f32)).astype(bf16)
    return hi, mid, lo


def _mm3s(a, b, dims=NN):
    return _dot(a[0], b[0], dims) + _dot(a[0], b[1], dims) + _dot(a[1], b[0], dims)


def _mm3(a, b, dims=NN):
    return _mm3s(_split(a), _split(b), dims)


def _sigmoid(x):
    return 1.0 / (1.0 + jnp.exp(-x))


def _softplus(x):
    return jnp.maximum(x, 0.0) + jnp.log1p(jnp.exp(-jnp.abs(x)))


def _rms(x, g):
    return x * lax.rsqrt(jnp.mean(x * x, axis=-1, keepdims=True) + EPS) * g


def _inproj_body(x_ref, g_ref, w_ref, *out_refs):
    h = _rms(x_ref[...], g_ref[...]).astype(bf16)
    off = 0
    for o_ref, width in zip(out_refs, SEG_WIDTHS):
        o_ref[...] = _dot(h, w_ref[:, off:off + width])
        off += width


def _inproj(x, g, w, tm=256):
    m = x.shape[0]
    tm = min(tm, m)
    assert m % tm == 0
    return pl.pallas_call(
        _inproj_body,
        grid=(m // tm,),
        in_specs=[pl.BlockSpec((tm, D_MODEL), lambda i: (i, 0)),
                  pl.BlockSpec((1, D_MODEL), lambda i: (0, 0)),
                  pl.BlockSpec((D_MODEL, N_PERM), lambda i: (0, 0),
                               pipeline_mode=pl.Buffered(1))],
        out_specs=[pl.BlockSpec((tm, wd), lambda i: (i, 0)) for wd in SEG_WIDTHS],
        out_shape=[jax.ShapeDtypeStruct((m, wd), f32) for wd in SEG_WIDTHS],
        compiler_params=pltpu.CompilerParams(
            dimension_semantics=("arbitrary",), vmem_limit_bytes=VMEM_LIMIT),
        name="inproj",
    )(x, g, w)


def _permute_w_in(w):
    o_z = 3 * WIDTH_A
    o_ab = o_z + WIDTH_A
    o_q = o_ab + 2 * N_HEADS_A
    o_k = o_q + N_GROUPS * WIDTH_G
    o_v = o_k + N_GROUPS * WIDTH_G
    o_gate = o_v + N_GROUPS * WIDTH_G
    parts = [w[:, :o_ab], w[:, o_q:o_k]]
    for g in range(N_GROUPS):
        parts.append(w[:, o_k + g * WIDTH_G:o_k + (g + 1) * WIDTH_G])
        parts.append(w[:, o_v + g * WIDTH_G:o_v + (g + 1) * WIDTH_G])
    parts.append(w[:, o_gate:])
    parts.append(w[:, o_ab:o_q])
    parts.append(jnp.zeros((w.shape[0], LANES - 2 * N_HEADS_A), w.dtype))
    return jnp.concatenate(parts, axis=1).astype(bf16)


CHUNK = 128
INV_BASE = 16
HEADS = tuple(range(N_HEADS_A))


def _tri_inverse_heads(a_list, row, col):
    eye = jnp.where(row == col, 1.0, 0.0)
    blk = lambda idx, size: jnp.right_shift(idx, size.bit_length() - 1)
    same16 = blk(row, INV_BASE) == blk(col, INV_BASE)
    ad = [jnp.where(same16, a, 0.0) for a in a_list]
    p = [eye - x for x in ad]
    xs = [_split(x) for x in ad]
    n = 2
    while n < INV_BASE:
        x = [_mm3s(s, s) for s in xs]
        xs = [_split(v) for v in x]
        ps = [_split(v) for v in p]
        p = [pv + _mm3s(pp, xx) for pv, pp, xx in zip(p, ps, xs)]
        n *= 2
    s = INV_BASE
    while s < CHUNK:
        pair = (blk(row, 2 * s) == blk(col, 2 * s)) & (blk(row, s) != blk(col, s))
        es = [_split(jnp.where(pair, a, 0.0)) for a in a_list]
        ps = [_split(v) for v in p]
        pe = [_split(_mm3s(pp, ee)) for pp, ee in zip(ps, es)]
        p = [pv - _mm3s(x, pp) for pv, x, pp in zip(p, pe, ps)]
        s *= 2
    return [_split(v) for v in p]


def _delta_prompt_body(qkv_ref, ab_ref, z_ref, cw_ref, nega_ref, dtb_ref, gn_ref, halo_ref, s0_ref,
                       o_ref, s_out_ref, xs_ref, s_ref):
    c = pl.program_id(1)

    @pl.when(c == 0)
    def _():
        xs_ref[pl.ds(0, SUBLANES), :] = halo_ref[0]
        s_ref[...] = s0_ref[0]

    xs_ref[pl.ds(SUBLANES, CHUNK), :] = qkv_ref[0]
    conv = xs_ref[pl.ds(SUBLANES - CONV_W + 1, CHUNK), :] * cw_ref[0:1, :]
    for j in range(1, CONV_W):
        conv = conv + xs_ref[pl.ds(SUBLANES - CONV_W + 1 + j, CHUNK), :] * cw_ref[j:j + 1, :]
    xs_ref[pl.ds(0, SUBLANES), :] = xs_ref[pl.ds(CHUNK, SUBLANES), :]
    cs = conv * _sigmoid(conv)

    abt = ab_ref[0]
    log_a = nega_ref[...] * _softplus(abt + dtb_ref[...])
    beta_all = _sigmoid(abt)

    row = lax.broadcasted_iota(jnp.int32, (CHUNK, CHUNK), 0)
    col = lax.broadcasted_iota(jnp.int32, (CHUNK, CHUNK), 1)
    causal = row >= col
    strict = row > col
    ltri = jnp.where(causal, 1.0, 0.0).astype(bf16)

    def l2n(x, scale):
        return x * (lax.rsqrt(jnp.sum(x * x, axis=-1, keepdims=True) + EPS) * scale)

    q = [l2n(cs[:, h * HEAD_DIM_A:(h + 1) * HEAD_DIM_A], HEAD_DIM_A ** -0.5) for h in HEADS]
    k = [l2n(cs[:, WIDTH_A + h * HEAD_DIM_A:WIDTH_A + (h + 1) * HEAD_DIM_A], 1.0) for h in HEADS]
    v = [cs[:, 2 * WIDTH_A + h * HEAD_DIM_A:2 * WIDTH_A + (h + 1) * HEAD_DIM_A] for h in HEADS]
    beta = [beta_all[:, N_HEADS_A + h:N_HEADS_A + h + 1] for h in HEADS]
    g3 = [_split3(jnp.broadcast_to(log_a[:, h:h + 1], (CHUNK, HEAD_DIM_A))) for h in HEADS]
    cum = [_dot(ltri, t[0]) + _dot(ltri, t[1]) + _dot(ltri, t[2]) for t in g3]
    decay = [jnp.exp(jnp.where(causal, x - x.T, NEG)) for x in cum]
    e_g = [jnp.exp(x) for x in cum]
    g_last = [x[CHUNK - 1:CHUNK, :] for x in cum]
    kb = [a * b for a, b in zip(k, beta)]
    k_bf = [x.astype(bf16) for x in k]
    a_mat = [jnp.where(strict, _dot(x.astype(bf16), y, NT) * d, 0.0)
             for x, y, d in zip(kb, k_bf, decay)]
    t_inv = _tri_inverse_heads(a_mat, row, col)
    w = [_mm3s(t, _split(x * e)) for t, x, e in zip(t_inv, kb, e_g)]
    u = [_mm3s(t, _split(x * b)) for t, x, b in zip(t_inv, v, beta)]
    qk = [_dot(x.astype(bf16), y, NT) * d for x, y, d in zip(q, k_bf, decay)]
    qg = [x * e for x, e in zip(q, e_g)]
    kd = [x * jnp.exp(gl - cm) for x, gl, cm in zip(k, g_last, cum)]
    s_old = [s_ref[h] for h in HEADS]
    s_sp = [_split(x) for x in s_old]
    v_new = [uu - _mm3s(_split(ww), ss) for uu, ww, ss in zip(u, w, s_sp)]
    o = [_dot(x.astype(bf16), ss[0]) + _mm1(y, vn)
         for x, ss, y, vn in zip(qg, s_sp, qk, v_new)]
    s_new = [so * jnp.exp(gl) + _mm3(x, vn, TN)
             for so, gl, x, vn in zip(s_old, g_last, kd, v_new)]
    for h in HEADS:
        s_ref[h] = s_new[h]
        lo = h * HEAD_DIM_A
        zz = z_ref[0, :, lo:lo + HEAD_DIM_A]
        o_ref[0, :, lo:lo + HEAD_DIM_A] = _rms(o[h], gn_ref[...]) * (zz * _sigmoid(zz))

    @pl.when(c == pl.num_programs(1) - 1)
    def _():
        s_out_ref[0] = s_ref[...]


def _delta_prompt(qkv, ab, z, cw, nega, dtb, gn, halo, s0):
    b, t, _ = qkv.shape
    n = t // CHUNK
    return pl.pallas_call(
        _delta_prompt_body,
        grid=(b, n),
        in_specs=[pl.BlockSpec((1, CHUNK, 3 * WIDTH_A), lambda i, c: (i, c, 0)),
                  pl.BlockSpec((1, CHUNK, LANES), lambda i, c: (i, c, 0)),
                  pl.BlockSpec((1, CHUNK, WIDTH_A), lambda i, c: (i, c, 0)),
                  pl.BlockSpec((CONV_W, 3 * WIDTH_A), lambda i, c: (0, 0)),
                  pl.BlockSpec((1, LANES), lambda i, c: (0, 0)),
                  pl.BlockSpec((1, LANES), lambda i, c: (0, 0)),
                  pl.BlockSpec((1, HEAD_DIM_A), lambda i, c: (0, 0)),
                  pl.BlockSpec((1, SUBLANES, 3 * WIDTH_A), lambda i, c: (i, 0, 0)),
                  pl.BlockSpec((1, N_HEADS_A, HEAD_DIM_A, HEAD_DIM_A), lambda i, c: (i, 0, 0, 0))],
        out_specs=[pl.BlockSpec((1, CHUNK, WIDTH_A), lambda i, c: (i, c, 0)),
                   pl.BlockSpec((1, N_HEADS_A, HEAD_DIM_A, HEAD_DIM_A), lambda i, c: (i, 0, 0, 0))],
        out_shape=[jax.ShapeDtypeStruct((b, t, WIDTH_A), f32),
                   jax.ShapeDtypeStruct((b, N_HEADS_A, HEAD_DIM_A, HEAD_DIM_A), f32)],
        scratch_shapes=[pltpu.VMEM((CHUNK + SUBLANES, 3 * WIDTH_A), f32),
                        pltpu.VMEM((N_HEADS_A, HEAD_DIM_A, HEAD_DIM_A), f32)],
        compiler_params=pltpu.CompilerParams(
            dimension_semantics=("arbitrary", "arbitrary"), vmem_limit_bytes=VMEM_LIMIT),
        name="delta_prompt",
    )(qkv, ab, z, cw, nega, dtb, gn, halo, s0)


def _delta_sample_body(*refs, nb, t, aliased):
    src_ref, ab_ref, z_ref, cw_ref, nega_ref, dtb_ref, gn_ref, s0_ref = refs[:8]
    o_ref, s_out_ref = refs[-2:]
    rowi = lax.broadcasted_iota(jnp.int32, (SUBLANES, LANES), 0)
    real = rowi < t
    for b in range(nb):
        src = src_ref[b]
        off = SUBLANES - t - (CONV_W - 1)
        conv = pltpu.roll(src, (SUBLANES - off) % SUBLANES, axis=0) * cw_ref[0:1, :]
        for j in range(1, CONV_W):
            conv = conv + pltpu.roll(src, (SUBLANES - off - j) % SUBLANES, axis=0) * cw_ref[j:j + 1, :]
        cs = conv * _sigmoid(conv)
        abt = ab_ref[b]
        log_a = jnp.where(real, nega_ref[...] * _softplus(abt + dtb_ref[...]), 0.0)
        beta_all = jnp.where(real, _sigmoid(abt), 0.0)
        for h in range(N_HEADS_A):
            lo = h * HEAD_DIM_A
            q = cs[:, lo:lo + HEAD_DIM_A]
            k = cs[:, WIDTH_A + lo:WIDTH_A + lo + HEAD_DIM_A]
            v = cs[:, 2 * WIDTH_A + lo:2 * WIDTH_A + lo + HEAD_DIM_A]
            q = q * (lax.rsqrt(jnp.sum(q * q, axis=-1, keepdims=True) + EPS) * HEAD_DIM_A ** -0.5)
            k = k * lax.rsqrt(jnp.sum(k * k, axis=-1, keepdims=True) + EPS)
            beta = beta_all[:, N_HEADS_A + h:N_HEADS_A + h + 1]
            gb = jnp.broadcast_to(log_a[:, h:h + 1], (SUBLANES, LANES))
            cum = jnp.zeros((SUBLANES, LANES), f32)
            for j in range(t):
                cum = cum + jnp.where(rowi >= j, gb[j:j + 1, :], 0.0)
            g_last = cum[t - 1:t, :]
            e_g = jnp.exp(cum)
            kb = k * beta
            w = kb * e_g
            u = v * beta
            qk_cols = []
            for j in range(t):
                dec_j = jnp.exp(jnp.where(rowi >= j, cum - cum[j:j + 1, :], NEG))
                kj = k[j:j + 1, :]
                qk_cols.append(jnp.sum(q * kj, axis=-1, keepdims=True) * dec_j)
                if j < t - 1:
                    a_j = jnp.where(rowi > j, jnp.sum(kb * kj, axis=-1, keepdims=True) * dec_j, 0.0)
                    w = w - a_j * w[j:j + 1, :]
                    u = u - a_j * u[j:j + 1, :]
            qg = q * e_g
            kd = k * jnp.exp(g_last - cum)
            s_h = s0_ref[0, b, h]
            v_new = u - _mm3(w, s_h)
            o = _mm3(qg, s_h)
            for j in range(t):
                o = o + qk_cols[j] * v_new[j:j + 1, :]
            s_out_ref[0, b, h] = s_h * jnp.exp(g_last) + _mm3(kd, v_new, TN)
            zz = z_ref[b, :, lo:lo + HEAD_DIM_A]
            o_ref[b, :, lo:lo + HEAD_DIM_A] = _rms(o, gn_ref[...]) * (zz * _sigmoid(zz))


def _delta_sample(src, ab, z, cw, nega, dtb, gn, s_all, s_prev, layer, t, nb=4):
    b = src.shape[0]
    aliased = s_prev is not None
    state_spec = pl.BlockSpec((1, nb, N_HEADS_A, HEAD_DIM_A, HEAD_DIM_A),
                              lambda i: (layer, i, 0, 0, 0))
    in_specs = [pl.BlockSpec((nb, SUBLANES, 3 * WIDTH_A), lambda i: (i, 0, 0)),
                pl.BlockSpec((nb, SUBLANES, LANES), lambda i: (i, 0, 0)),
                pl.BlockSpec((nb, SUBLANES, WIDTH_A), lambda i: (i, 0, 0)),
                pl.BlockSpec((CONV_W, 3 * WIDTH_A), lambda i: (0, 0)),
                pl.BlockSpec((1, LANES), lambda i: (0, 0)),
                pl.BlockSpec((1, LANES), lambda i: (0, 0)),
                pl.BlockSpec((1, HEAD_DIM_A), lambda i: (0, 0)),
                state_spec]
    args = [src, ab, z, cw, nega, dtb, gn, s_all]
    if aliased:
        in_specs.append(pl.BlockSpec(memory_space=pl.ANY))
        args.append(s_prev)
    return pl.pallas_call(
        functools.partial(_delta_sample_body, nb=nb, t=t, aliased=aliased),
        grid=(b // nb,),
        in_specs=in_specs,
        out_specs=[pl.BlockSpec((nb, SUBLANES, WIDTH_A), lambda i: (i, 0, 0)), state_spec],
        out_shape=[jax.ShapeDtypeStruct((b, SUBLANES, WIDTH_A), f32),
                   jax.ShapeDtypeStruct(s_all.shape, f32)],
        input_output_aliases={len(args) - 1: 1} if aliased else {},
        compiler_params=pltpu.CompilerParams(
            dimension_semantics=("arbitrary",), vmem_limit_bytes=VMEM_LIMIT),
        name="delta_sample",
    )(*args)


ATT_TB = 2048
Q_CHUNKS = WIDTH_G // LANES
KV_CHUNKS = 2 * WIDTH_G // LANES


def _attn_prompt_body(q_ref, kvh_ref, kvc_ref, o_ref, l_ref, q_scr, kv_scr, o_scr, l_scr, *,
                      dil, slopes, tb):
    i = pl.program_id(1)
    halo = N_OFF * dil
    for c in range(Q_CHUNKS):
        q_scr[c] = q_ref[0, :, c * LANES:(c + 1) * LANES]
    for c in range(KV_CHUNKS):
        kv_scr[c, pl.ds(0, halo), :] = kvh_ref[0, :, c * LANES:(c + 1) * LANES]
        kv_scr[c, pl.ds(halo, tb), :] = kvc_ref[0, :, c * LANES:(c + 1) * LANES]
    a = lax.broadcasted_iota(jnp.int32, (N_OFF, 2 * N_OFF), 0)
    cc = lax.broadcasted_iota(jnp.int32, (N_OFF, 2 * N_OFF), 1)
    delta = a - cc + N_OFF
    in_win = (delta >= 0) & (delta <= N_OFF)
    dist = delta.astype(f32) * float(dil)
    upper = lax.broadcasted_iota(jnp.int32, (N_OFF, LANES), 1) >= HEAD_DIM_B
    n_sub = tb // halo

    def sub_block(idx, carry):
        r = idx // n_sub
        u = idx - r * n_sub
        base = r + halo * u
        key_tok = i * tb - halo + base + dil * cc
        mask = in_win & (key_tok >= 0)
        cs_ = range(Q_CHUNKS)
        hs = range(HEADS_PER_GROUP)
        qc = [q_scr[c, pl.ds(base, N_OFF, stride=dil), :] * HEAD_DIM_B ** -0.5 for c in cs_]
        kc = [kv_scr[c, pl.ds(base, 2 * N_OFF, stride=dil), :].astype(bf16) for c in cs_]
        vc = [kv_scr[Q_CHUNKS + c, pl.ds(base, 2 * N_OFF, stride=dil), :].astype(bf16) for c in cs_]
        qm = [jnp.where(upper if h % 2 else jnp.logical_not(upper), qc[h // 2], 0.0).astype(bf16)
              for h in hs]
        s = [jnp.where(mask, _dot(qm[h], kc[h // 2], NT) - slopes[h] * dist, NEG) for h in hs]
        m = [jnp.max(s[h], axis=-1, keepdims=True) for h in hs]
        p = [jnp.exp(s[h] - m[h]) for h in hs]
        den = [jnp.sum(p[h], axis=-1, keepdims=True) for h in hs]
        pv = [_dot(p[h].astype(bf16), vc[h // 2]) / den[h] for h in hs]
        lse = [jnp.broadcast_to(m[h] + jnp.log(den[h]), (N_OFF, LANES)) for h in hs]
        for c in cs_:
            o_scr[c, pl.ds(base, N_OFF, stride=dil), :] = jnp.where(upper, pv[2 * c + 1], pv[2 * c])
            l_scr[c, pl.ds(base, N_OFF, stride=dil), :] = jnp.where(upper, lse[2 * c + 1], lse[2 * c])
        return carry

    lax.fori_loop(0, tb // N_OFF, sub_block, 0)
    for c in range(Q_CHUNKS):
        o_ref[0, :, c * LANES:(c + 1) * LANES] = o_scr[c]
        l_ref[0, :, c * LANES:(c + 1) * LANES] = l_scr[c]


def _attn_prompt(q, kv, g):
    b, t, _ = q.shape
    dil = DILATIONS[g]
    halo = N_OFF * dil
    tb = min(ATT_TB, t)
    assert tb % halo == 0 and t % tb == 0
    per = tb // halo
    return pl.pallas_call(
        functools.partial(_attn_prompt_body, dil=dil, slopes=_slopes(g), tb=tb),
        grid=(b, t // tb),
        in_specs=[pl.BlockSpec((1, tb, WIDTH_G), lambda bi, i: (bi, i, 0)),
                  pl.BlockSpec((1, halo, 2 * WIDTH_G),
                               lambda bi, i: (bi, jnp.maximum(i * per - 1, 0), 0)),
                  pl.BlockSpec((1, tb, 2 * WIDTH_G), lambda bi, i: (bi, i, 0))],
        out_specs=[pl.BlockSpec((1, tb, WIDTH_G), lambda bi, i: (bi, i, 0)),
                   pl.BlockSpec((1, tb, WIDTH_G), lambda bi, i: (bi, i, 0))],
        out_shape=[jax.ShapeDtypeStruct((b, t, WIDTH_G), f32),
                   jax.ShapeDtypeStruct((b, t, WIDTH_G), f32)],
        scratch_shapes=[pltpu.VMEM((Q_CHUNKS, tb, LANES), f32),
                        pltpu.VMEM((KV_CHUNKS, halo + tb, LANES), f32),
                        pltpu.VMEM((Q_CHUNKS, tb, LANES), f32),
                        pltpu.VMEM((Q_CHUNKS, tb, LANES), f32)],
        compiler_params=pltpu.CompilerParams(
            dimension_semantics=("arbitrary", "arbitrary"), vmem_limit_bytes=VMEM_LIMIT),
        name=f"attn_prompt_g{g}",
    )(q, kv, kv)


def _attn_sample_body(*refs, dil, slopes, nb, t, win):
    q_ref, kn_ref, vn_ref, kvn_ref, c_ref = refs[:5]
    o_ref, l_ref, cn_ref = refs[-3:]
    rr = lax.broadcasted_iota(jnp.int32, (SUBLANES, win), 1)
    qi = lax.broadcasted_iota(jnp.int32, (SUBLANES, win), 0)
    valid = ((rr & (dil - 1)) == (qi & (dil - 1))) & (rr >= qi) & (qi < t)
    dist = (win + qi - rr).astype(f32)
    nn = lax.broadcasted_iota(jnp.int32, (SUBLANES, SUBLANES), 1)
    qn = lax.broadcasted_iota(jnp.int32, (SUBLANES, SUBLANES), 0)
    valid_n = (nn <= qn) & (((qn - nn) & (dil - 1)) == 0) & (qn < t)
    dist_n = (qn - nn).astype(f32)
    lane = lax.broadcasted_iota(jnp.int32, (HEAD_DIM_B, LANES), 1)
    tail_lanes = lane >= LANES - t
    e_rows = lax.broadcasted_iota(jnp.int32, (SUBLANES, LANES), 0)
    e_lane = lax.broadcasted_iota(jnp.int32, (SUBLANES, LANES), 1)
    place = jnp.where((e_lane == e_rows + (LANES - t)) & (e_rows < t), 1.0, 0.0).astype(bf16)

    def one_sequence(b, carry):
        k3 = _split3(kvn_ref[b])
        new_t = _dot(k3[0], place, TN) + _dot(k3[1], place, TN) + _dot(k3[2], place, TN)
        hs = range(HEADS_PER_GROUP)
        k_t = [c_ref[0, b, 0, h] for h in hs]
        v_t = [c_ref[0, b, 1, h] for h in hs]
        q8 = [(q_ref[b, h] * HEAD_DIM_B ** -0.5).astype(bf16) for h in hs]
        s = [jnp.where(valid, _dot(q8[h], k_t[h].astype(bf16)) - slopes[h] * dist, NEG) for h in hs]
        sn = [jnp.where(valid_n, _dot(q8[h], kn_ref[b, h].astype(bf16), NT) - slopes[h] * dist_n, NEG)
              for h in hs]
        m = [jnp.maximum(jnp.max(s[h], axis=-1, keepdims=True),
                         jnp.max(sn[h], axis=-1, keepdims=True)) for h in hs]
        p = [jnp.exp(s[h] - m[h]) for h in hs]
        pn = [jnp.exp(sn[h] - m[h]) for h in hs]
        den = [jnp.sum(p[h], axis=-1, keepdims=True) + jnp.sum(pn[h], axis=-1, keepdims=True)
               for h in hs]
        acc = [_dot(p[h].astype(bf16), v_t[h].astype(bf16), NT) + _mm1(pn[h], vn_ref[b, h])
               for h in hs]
        for h in hs:
            o_ref[b, :, h * HEAD_DIM_B:(h + 1) * HEAD_DIM_B] = acc[h] / den[h]
            l_ref[b, :, h * HEAD_DIM_B:(h + 1) * HEAD_DIM_B] = jnp.broadcast_to(
                m[h] + jnp.log(den[h]), (SUBLANES, HEAD_DIM_B))
            for kv, x in ((0, k_t[h]), (1, v_t[h])):
                rolled = pltpu.roll(x, win - t, axis=1)
                fresh = new_t[(kv * HEADS_PER_GROUP + h) * HEAD_DIM_B:
                              (kv * HEADS_PER_GROUP + h + 1) * HEAD_DIM_B, :]
                if win > LANES:
                    cn_ref[0, b, kv, h, :, pl.ds(0, win - LANES)] = rolled[:, :win - LANES]
                cn_ref[0, b, kv, h, :, pl.ds(win - LANES, LANES)] = jnp.where(
                    tail_lanes, fresh, rolled[:, win - LANES:])
        return carry

    lax.fori_loop(0, nb, one_sequence, 0)


def _attn_sample(q, kvn, cache_t, prev, layer, g, t):
    bsz, win = cache_t.shape[1], cache_t.shape[-1]
    dil = DILATIONS[g]
    nb = max(1, min(16, 2048 // win))
    pad = SUBLANES - t
    heads = lambda a_: jnp.pad(a_.reshape(bsz, t, HEADS_PER_GROUP, HEAD_DIM_B).transpose(0, 2, 1, 3),
                               ((0, 0), (0, 0), (0, pad), (0, 0)))
    q4 = heads(q)
    kn4 = heads(kvn[:, :WIDTH_G])
    vn4 = heads(kvn[:, WIDTH_G:])
    kvn8 = jnp.pad(kvn.reshape(bsz, t, 2 * WIDTH_G), ((0, 0), (0, pad), (0, 0)))
    aliased = prev is not None
    head_spec = pl.BlockSpec((nb, HEADS_PER_GROUP, SUBLANES, HEAD_DIM_B), lambda i: (i, 0, 0, 0))
    cache_spec = pl.BlockSpec((1, nb, 2, HEADS_PER_GROUP, HEAD_DIM_B, win),
                              lambda i: (layer, i, 0, 0, 0, 0))
    out_spec = pl.BlockSpec((nb, SUBLANES, WIDTH_G), lambda i: (i, 0, 0))
    in_specs = [head_spec, head_spec, head_spec,
                pl.BlockSpec((nb, SUBLANES, 2 * WIDTH_G), lambda i: (i, 0, 0)), cache_spec]
    args = [q4, kn4, vn4, kvn8, cache_t]
    if aliased:
        in_specs.append(pl.BlockSpec(memory_space=pl.ANY))
        args.append(prev)
    o, l, cn = pl.pallas_call(
        functools.partial(_attn_sample_body, dil=dil, slopes=_slopes(g), nb=nb, t=t, win=win),
        grid=(bsz // nb,),
        in_specs=in_specs,
        out_specs=[out_spec, out_spec, cache_spec],
        out_shape=[jax.ShapeDtypeStruct((bsz, SUBLANES, WIDTH_G), f32),
                   jax.ShapeDtypeStruct((bsz, SUBLANES, WIDTH_G), f32),
                   jax.ShapeDtypeStruct(cache_t.shape, f32)],
        input_output_aliases={len(args) - 1: 2} if aliased else {},
        compiler_params=pltpu.CompilerParams(
            dimension_semantics=("arbitrary",), vmem_limit_bytes=VMEM_LIMIT),
        name=f"attn_sample_g{g}",
    )(*args)
    return (o[:, :t].reshape(bsz * t, WIDTH_G), l[:, :t].reshape(bsz * t, WIDTH_G), cn)


FF_CHUNK = 512


def _tail_body(x_ref, oa_ref, o0_ref, o1_ref, o2_ref, l0_ref, l1_ref, l2_ref, gate_ref,
               wa_ref, wb_ref, wo_ref, n2_ref, wu_ref, wd_ref, fn_ref, y_ref, *, final):
    l0, l1, l2 = l0_ref[...], l1_ref[...], l2_ref[...]
    lm = jnp.maximum(jnp.maximum(l0, l1), l2)
    e0, e1, e2 = jnp.exp(l0 - lm), jnp.exp(l1 - lm), jnp.exp(l2 - lm)
    den = e0 + e1 + e2
    ob = (e0 / den) * o0_ref[...] + (e1 / den) * o1_ref[...] + (e2 / den) * o2_ref[...]
    ya = _mm1(oa_ref[...], wa_ref[...])
    yb = _mm1(ob, wb_ref[...])
    merged = _sigmoid(gate_ref[:, :D_MODEL]) * ya + _sigmoid(gate_ref[:, D_MODEL:]) * yb
    x1 = x_ref[...] + _mm1(merged, wo_ref[...])
    h2 = _rms(x1, n2_ref[...]).astype(bf16)
    acc = x1
    for j in range(D_FF // FF_CHUNK):
        up = _dot(h2, wu_ref[:, j * FF_CHUNK:(j + 1) * FF_CHUNK])
        act = jnp.square(jnp.maximum(up, 0.0)).astype(bf16)
        acc = acc + _dot(act, wd_ref[j * FF_CHUNK:(j + 1) * FF_CHUNK, :])
    if final:
        acc = _rms(acc, fn_ref[...])
    y_ref[...] = acc


def _tail(x, oa, outs, lses, gate, wa, wb, wo, n2, wu, wd, fn, final, tm=256):
    m = x.shape[0]
    tm = min(tm, m)
    assert m % tm == 0
    tok = lambda wd_: pl.BlockSpec((tm, wd_), lambda i: (i, 0))
    const = lambda shape: pl.BlockSpec(shape, lambda i: (0, 0), pipeline_mode=pl.Buffered(1))
    return pl.pallas_call(
        functools.partial(_tail_body, final=final),
        grid=(m // tm,),
        in_specs=[tok(D_MODEL), tok(WIDTH_A)] + [tok(WIDTH_G)] * 6 + [tok(2 * D_MODEL),
                  const((WIDTH_A, D_MODEL)), const((WIDTH_G, D_MODEL)), const((D_MODEL, D_MODEL)),
                  const((1, D_MODEL)), const((D_MODEL, D_FF)), const((D_FF, D_MODEL)),
                  const((1, D_MODEL))],
        out_specs=tok(D_MODEL),
        out_shape=jax.ShapeDtypeStruct((m, D_MODEL), f32),
        compiler_params=pltpu.CompilerParams(
            dimension_semantics=("arbitrary",), vmem_limit_bytes=VMEM_LIMIT),
        name="tail",
    )(x, oa, *outs, *lses, gate, wa, wb, wo, n2, wu, wd, fn)


def _lane_row(vals):
    return jnp.zeros((1, LANES), f32).at[0, :vals.shape[0]].set(vals.astype(f32))


def _layer_params(l, norm1, w_in, conv_w, a_log, dt_bias, gnorm_a, w_br_a, w_br_b, w_out, norm2,
                  w_up, w_down):
    return dict(
        n1=norm1[l].reshape(1, D_MODEL), w_in=_permute_w_in(w_in[l]), cw=conv_w[l],
        nega=_lane_row(-jnp.exp(a_log[l].astype(f32))), dtb=_lane_row(dt_bias[l]),
        gn=gnorm_a[l].reshape(1, HEAD_DIM_A).astype(f32),
        wa=w_br_a[l].astype(bf16), wb=w_br_b[l].astype(bf16), wo=w_out[l].astype(bf16),
        n2=norm2[l].reshape(1, D_MODEL), wu=w_up[l].astype(bf16), wd=w_down[l].astype(bf16))


def _prompt_layer(x, p, fn, final):
    b, t, _ = x.shape
    xf = x.reshape(b * t, D_MODEL)
    qkv, z, q0, q1, q2, kv0, kv1, kv2, gate, ab = _inproj(xf, p["n1"], p["w_in"])
    qkv3 = qkv.reshape(b, t, 3 * WIDTH_A)
    halo = jnp.zeros((b, SUBLANES, 3 * WIDTH_A), f32)
    s0 = jnp.zeros((b, N_HEADS_A, HEAD_DIM_A, HEAD_DIM_A), f32)
    oa, s_new = _delta_prompt(qkv3, ab.reshape(b, t, LANES), z.reshape(b, t, WIDTH_A), p["cw"],
                              p["nega"], p["dtb"], p["gn"], halo, s0)
    outs, lses, wins = [], [], []
    for g, (qg, kvg) in enumerate(((q0, kv0), (q1, kv1), (q2, kv2))):
        kv3 = kvg.reshape(b, t, 2 * WIDTH_G)
        o, l = _attn_prompt(qg.reshape(b, t, WIDTH_G), kv3, g)
        outs.append(o.reshape(b * t, WIDTH_G))
        lses.append(l.reshape(b * t, WIDTH_G))
        keep = min(WINDOWS[g], t)
        wins.append(kv3[:, t - keep:].reshape(b, keep, 2, HEADS_PER_GROUP, HEAD_DIM_B))
    y = _tail(xf, oa.reshape(b * t, WIDTH_A), outs, lses, gate, p["wa"], p["wb"], p["wo"], p["n2"],
              p["wu"], p["wd"], fn, final)
    return y.reshape(b, t, D_MODEL), wins, s_new, qkv3[:, t - (CONV_W - 1):]


def _sample_layer(x, p, fn, final, layer, caches_t, prev_caches, s_all, s_prev, conv_buf):
    b, t, _ = x.shape
    xf = x.reshape(b * t, D_MODEL)
    qkv, z, q0, q1, q2, kv0, kv1, kv2, gate, ab = _inproj(xf, p["n1"], p["w_in"])
    qkv3 = qkv.reshape(b, t, 3 * WIDTH_A)
    pad = SUBLANES - t
    src = jnp.concatenate([jnp.zeros((b, SUBLANES - t - (CONV_W - 1), 3 * WIDTH_A), f32),
                           conv_buf.astype(f32), qkv3], axis=1)
    pad_rows = lambda a_: jnp.pad(a_, ((0, 0), (0, pad), (0, 0)))
    oa, s_stack = _delta_sample(src, pad_rows(ab.reshape(b, t, LANES)),
                                pad_rows(z.reshape(b, t, WIDTH_A)), p["cw"], p["nega"], p["dtb"],
                                p["gn"], s_all, s_prev, layer, t)
    oa = oa[:, :t].reshape(b * t, WIDTH_A)
    outs, lses, new_caches = [], [], []
    for g, (qg, kvg) in enumerate(((q0, kv0), (q1, kv1), (q2, kv2))):
        o, l, cn = _attn_sample(qg, kvg, caches_t[g], prev_caches[g], layer, g, t)
        outs.append(o)
        lses.append(l)
        new_caches.append(cn)
    y = _tail(xf, oa, outs, lses, gate, p["wa"], p["wb"], p["wo"], p["n2"], p["wu"], p["wd"],
              fn, final)
    return y.reshape(b, t, D_MODEL), new_caches, s_stack, qkv3[:, t - (CONV_W - 1):]


def kernel(x_prompt, x_sample, cache_win0, cache_win1, cache_win2, state_delta, state_conv, norm1,
           w_in, conv_w, a_log, dt_bias, gnorm_a, w_br_a, w_br_b, w_out, norm2, w_up, w_down,
           final_norm):
    depth = w_in.shape[0]
    assert x_sample.shape[1] + CONV_W - 1 <= SUBLANES
    fn = final_norm.reshape(1, D_MODEL)
    caches_t = [c.transpose(0, 1, 3, 4, 5, 2) for c in (cache_win0, cache_win1, cache_win2)]
    new_caches = [None] * N_GROUPS
    s_stack = None
    xp, xs = x_prompt, x_sample
    wins_p = [[] for _ in range(N_GROUPS)]
    delta_p, conv_p, conv_s = [], [], []
    for l in range(depth):
        p = _layer_params(l, norm1, w_in, conv_w, a_log, dt_bias, gnorm_a, w_br_a, w_br_b, w_out,
                          norm2, w_up, w_down)
        final = l == depth - 1
        xp, wp, sp, cp = _prompt_layer(xp, p, fn, final)
        xs, new_caches, s_stack, cs = _sample_layer(xs, p, fn, final, l, caches_t, new_caches,
                                                    state_delta, s_stack, state_conv[l])
        for g in range(N_GROUPS):
            wins_p[g].append(wp[g])
        delta_p.append(sp)
        conv_p.append(cp)
        conv_s.append(cs)
    st = jnp.stack
    wins_s = [c.transpose(0, 1, 5, 2, 3, 4) for c in new_caches]
    return (xp, xs, st(wins_p[0]), st(wins_p[1]), st(wins_p[2]), st(delta_p), st(conv_p),
            wins_s[0], wins_s[1], wins_s[2], s_stack, st(conv_s))
```

```python
import functools

import jax
import jax.numpy as jnp
from jax import lax
from jax.experimental import pallas as pl
from jax.experimental.pallas import tpu as pltpu

f32 = jnp.float32
bf16 = jnp.bfloat16

D_MODEL = 1024
N_HEADS_A = 4
HEAD_DIM_A = 128
WIDTH_A = N_HEADS_A * HEAD_DIM_A
CONV_W = 4
WINDOWS = (128, 512, 2048)
DILATIONS = (1, 4, 16)
N_GROUPS = 3
HEADS_PER_GROUP = 4
HEAD_DIM_B = 64
N_HEADS_B = N_GROUPS * HEADS_PER_GROUP
WIDTH_G = HEADS_PER_GROUP * HEAD_DIM_B
D_FF = 4 * D_MODEL
EPS = 1e-6
N_OFF = 128
NEG = -1e30

SUBLANES = 8
LANES = 128
VMEM_LIMIT = 56 * 1024 * 1024

SEG_WIDTHS = (3 * WIDTH_A, WIDTH_A, WIDTH_G, WIDTH_G, WIDTH_G,
              2 * WIDTH_G, 2 * WIDTH_G, 2 * WIDTH_G, 2 * D_MODEL, LANES)
N_PERM = sum(SEG_WIDTHS)

NN = (((1,), (0,)), ((), ()))
NT = (((1,), (1,)), ((), ()))
TN = (((0,), (0,)), ((), ()))


def _slopes(g):
    return tuple(2.0 ** (-8.0 * (g * HEADS_PER_GROUP + h + 1) / N_HEADS_B)
                 for h in range(HEADS_PER_GROUP))


def _dot(a, b, dims=NN):
    return lax.dot_general(a, b, dims, preferred_element_type=f32)


def _mm1(a, b, dims=NN):
    return _dot(a.astype(bf16), b.astype(bf16), dims)


def _split(a):
    hi = a.astype(bf16)
    lo = (a - hi.astype(f32)).astype(bf16)
    return hi, lo


def _split3(a):
    hi = a.astype(bf16)
    r = a - hi.astype(f32)
    mid = r.astype(bf16)
    lo = (r - mid.astype(f32)).astype(bf16)
    return hi, mid, lo


def _mm3s(a, b, dims=NN):
    return _dot(a[0], b[0], dims) + _dot(a[0], b[1], dims) + _dot(a[1], b[0], dims)


def _mm3(a, b, dims=NN):
    return _mm3s(_split(a), _split(b), dims)


def _sigmoid(x):
    return 1.0 / (1.0 + jnp.exp(-x))


def _softplus(x):
    return jnp.maximum(x, 0.0) + jnp.log1p(jnp.exp(-jnp.abs(x)))


def _rms(x, g):
    return x * lax.rsqrt(jnp.mean(x * x, axis=-1, keepdims=True) + EPS) * g


def _inproj_body(x_ref, g_ref, w_ref, *out_refs):
    h = _rms(x_ref[...], g_ref[...]).astype(bf16)
    off = 0
    for o_ref, width in zip(out_refs, SEG_WIDTHS):
        o_ref[...] = _dot(h, w_ref[:, off:off + width])
        off += width


def _inproj_conv_body(x_ref, g_ref, w_ref, cw_ref, *refs, tm, tiles_per_seq):
    out_refs, last_ref, xs_ref = refs[:len(SEG_WIDTHS)], refs[-2], refs[-1]
    i = pl.program_id(0)

    @pl.when(lax.rem(i, tiles_per_seq) == 0)
    def _():
        xs_ref[pl.ds(0, SUBLANES), :] = jnp.zeros((SUBLANES, SEG_WIDTHS[0]), f32)

    h = _rms(x_ref[...], g_ref[...]).astype(bf16)
    xs_ref[pl.ds(SUBLANES, tm), :] = _dot(h, w_ref[:, :SEG_WIDTHS[0]])
    conv = xs_ref[pl.ds(SUBLANES - CONV_W + 1, tm), :] * cw_ref[0:1, :]
    for j in range(1, CONV_W):
        conv = conv + xs_ref[pl.ds(SUBLANES - CONV_W + 1 + j, tm), :] * cw_ref[j:j + 1, :]
    last = xs_ref[pl.ds(tm, SUBLANES), :]
    xs_ref[pl.ds(0, SUBLANES), :] = last
    last_ref[0] = last
    out_refs[0][...] = conv * _sigmoid(conv)
    off = SEG_WIDTHS[0]
    for o_ref, width in zip(out_refs[1:], SEG_WIDTHS[1:]):
        o_ref[...] = _dot(h, w_ref[:, off:off + width])
        off += width


def _inproj(x, g, w, tm=256, conv_w=None, seq_len=None):
    m = x.shape[0]
    tm = min(tm, m)
    assert m % tm == 0
    in_specs = [pl.BlockSpec((tm, D_MODEL), lambda i: (i, 0)),
                pl.BlockSpec((1, D_MODEL), lambda i: (0, 0)),
                pl.BlockSpec((D_MODEL, N_PERM), lambda i: (0, 0), pipeline_mode=pl.Buffered(1))]
    out_specs = [pl.BlockSpec((tm, wd), lambda i: (i, 0)) for wd in SEG_WIDTHS]
    out_shape = [jax.ShapeDtypeStruct((m, wd), f32) for wd in SEG_WIDTHS]
    args = [x, g, w]
    body, scratch = _inproj_body, []
    if conv_w is not None:
        assert seq_len % tm == 0 and tm >= SUBLANES
        per = seq_len // tm
        in_specs.append(pl.BlockSpec((CONV_W, SEG_WIDTHS[0]), lambda i: (0, 0)))
        out_specs.append(pl.BlockSpec((1, SUBLANES, SEG_WIDTHS[0]), lambda i: (i // per, 0, 0)))
        out_shape.append(jax.ShapeDtypeStruct((m // seq_len, SUBLANES, SEG_WIDTHS[0]), f32))
        args.append(conv_w)
        body = functools.partial(_inproj_conv_body, tm=tm, tiles_per_seq=per)
        scratch = [pltpu.VMEM((tm + SUBLANES, SEG_WIDTHS[0]), f32)]
    return pl.pallas_call(
        body,
        grid=(m // tm,),
        in_specs=in_specs,
        out_specs=out_specs,
        out_shape=out_shape,
        scratch_shapes=scratch,
        compiler_params=pltpu.CompilerParams(
            dimension_semantics=("arbitrary",), vmem_limit_bytes=VMEM_LIMIT),
        name="inproj",
    )(*args)


def _permute_w_in(w):
    o_z = 3 * WIDTH_A
    o_ab = o_z + WIDTH_A
    o_q = o_ab + 2 * N_HEADS_A
    o_k = o_q + N_GROUPS * WIDTH_G
    o_v = o_k + N_GROUPS * WIDTH_G
    o_gate = o_v + N_GROUPS * WIDTH_G
    parts = [w[:, :o_ab], w[:, o_q:o_k]]
    for g in range(N_GROUPS):
        parts.append(w[:, o_k + g * WIDTH_G:o_k + (g + 1) * WIDTH_G])
        parts.append(w[:, o_v + g * WIDTH_G:o_v + (g + 1) * WIDTH_G])
    parts.append(w[:, o_gate:])
    parts.append(w[:, o_ab:o_q])
    parts.append(jnp.zeros((w.shape[0], LANES - 2 * N_HEADS_A), w.dtype))
    return jnp.concatenate(parts, axis=1).astype(bf16)


CHUNK = 128
INV_BASE = 16
HEADS = tuple(range(N_HEADS_A))


def _tri_inverse_heads(a_list, row, col):
    eye = jnp.where(row == col, 1.0, 0.0)
    blk = lambda idx, size: jnp.right_shift(idx, size.bit_length() - 1)
    same16 = blk(row, INV_BASE) == blk(col, INV_BASE)
    ad = [jnp.where(same16, a, 0.0) for a in a_list]
    p = [eye - x for x in ad]
    xs = [_split(x) for x in ad]
    n = 2
    while n < INV_BASE:
        x = [_mm3s(s, s) for s in xs]
        xs = [_split(v) for v in x]
        ps = [_split(v) for v in p]
        p = [pv + _mm3s(pp, xx) for pv, pp, xx in zip(p, ps, xs)]
        n *= 2
    s = INV_BASE
    while s < CHUNK:
        pair = (blk(row, 2 * s) == blk(col, 2 * s)) & (blk(row, s) != blk(col, s))
        es = [_split(jnp.where(pair, a, 0.0)) for a in a_list]
        ps = [_split(v) for v in p]
        pe = [_split(_mm3s(pp, ee)) for pp, ee in zip(ps, es)]
        p = [pv - _mm3s(x, pp) for pv, x, pp in zip(p, pe, ps)]
        s *= 2
    return [_split(v) for v in p]


def _delta_prompt_body(cs_ref, ab_ref, z_ref, nega_ref, dtb_ref, gn_ref, s0_ref,
                       o_ref, s_out_ref, s_ref, *, nchunk):
    c = pl.program_id(1)

    @pl.when(c == 0)
    def _():
        s_ref[...] = s0_ref[0]

    items = [(ck, h) for ck in range(nchunk) for h in HEADS]
    rows = lambda ck: pl.ds(ck * CHUNK, CHUNK)
    log_a = [nega_ref[...] * _softplus(ab_ref[0, rows(ck), :] + dtb_ref[...])
             for ck in range(nchunk)]
    beta_all = [_sigmoid(ab_ref[0, rows(ck), :]) for ck in range(nchunk)]

    row = lax.broadcasted_iota(jnp.int32, (CHUNK, CHUNK), 0)
    col = lax.broadcasted_iota(jnp.int32, (CHUNK, CHUNK), 1)
    causal = row >= col
    strict = row > col
    ltri = jnp.where(causal, 1.0, 0.0).astype(bf16)

    def l2n(x, scale):
        return x * (lax.rsqrt(jnp.sum(x * x, axis=-1, keepdims=True) + EPS) * scale)

    def part(ck, h, which):
        lo = which * WIDTH_A + h * HEAD_DIM_A
        return cs_ref[0, rows(ck), lo:lo + HEAD_DIM_A]

    q = [l2n(part(ck, h, 0), HEAD_DIM_A ** -0.5) for ck, h in items]
    k = [l2n(part(ck, h, 1), 1.0) for ck, h in items]
    v = [part(ck, h, 2) for ck, h in items]
    beta = [beta_all[ck][:, N_HEADS_A + h:N_HEADS_A + h + 1] for ck, h in items]
    g3 = [_split3(jnp.broadcast_to(log_a[ck][:, h:h + 1], (CHUNK, HEAD_DIM_A))) for ck, h in items]
    cum = [_dot(ltri, t[0]) + _dot(ltri, t[1]) + _dot(ltri, t[2]) for t in g3]
    decay = [jnp.exp(jnp.where(causal, x - x.T, NEG)) for x in cum]
    e_g = [jnp.exp(x) for x in cum]
    g_last = [x[CHUNK - 1:CHUNK, :] for x in cum]
    kb = [a * b for a, b in zip(k, beta)]
    k_bf = [x.astype(bf16) for x in k]
    a_mat = [jnp.where(strict, _dot(x.astype(bf16), y, NT) * d, 0.0)
             for x, y, d in zip(kb, k_bf, decay)]
    t_inv = _tri_inverse_heads(a_mat, row, col)
    w = [_mm3s(t, _split(x * e)) for t, x, e in zip(t_inv, kb, e_g)]
    u = [_mm3s(t, _split(x * b)) for t, x, b in zip(t_inv, v, beta)]
    qk = [_dot(x.astype(bf16), y, NT) * d for x, y, d in zip(q, k_bf, decay)]
    qg = [x * e for x, e in zip(q, e_g)]
    kd = [x * jnp.exp(gl - cm) for x, gl, cm in zip(k, g_last, cum)]
    s_cur = [s_ref[h] for h in HEADS]
    for ck in range(nchunk):
        n0 = ck * N_HEADS_A
        s_sp = [_split(x) for x in s_cur]
        v_new = [u[n0 + h] - _mm3s(_split(w[n0 + h]), s_sp[h]) for h in HEADS]
        o = [_dot(qg[n0 + h].astype(bf16), s_sp[h][0]) + _mm1(qk[n0 + h], v_new[h]) for h in HEADS]
        s_cur = [s_cur[h] * jnp.exp(g_last[n0 + h]) + _mm3(kd[n0 + h], v_new[h], TN) for h in HEADS]
        for h in HEADS:
            lo = h * HEAD_DIM_A
            zz = z_ref[0, rows(ck), lo:lo + HEAD_DIM_A]
            o_ref[0, rows(ck), lo:lo + HEAD_DIM_A] = _rms(o[h], gn_ref[...]) * (zz * _sigmoid(zz))
    for h in HEADS:
        s_ref[h] = s_cur[h]

    @pl.when(c == pl.num_programs(1) - 1)
    def _():
        s_out_ref[0] = s_ref[...]


DELTA_CHUNKS_PER_STEP = 2


def _delta_prompt(cs, ab, z, nega, dtb, gn, s0):
    b, t, _ = cs.shape
    nchunk = DELTA_CHUNKS_PER_STEP if t % (DELTA_CHUNKS_PER_STEP * CHUNK) == 0 else 1
    tb = nchunk * CHUNK
    assert t % tb == 0
    return pl.pallas_call(
        functools.partial(_delta_prompt_body, nchunk=nchunk),
        grid=(b, t // tb),
        in_specs=[pl.BlockSpec((1, tb, 3 * WIDTH_A), lambda i, c: (i, c, 0)),
                  pl.BlockSpec((1, tb, LANES), lambda i, c: (i, c, 0)),
                  pl.BlockSpec((1, tb, WIDTH_A), lambda i, c: (i, c, 0)),
                  pl.BlockSpec((1, LANES), lambda i, c: (0, 0)),
                  pl.BlockSpec((1, LANES), lambda i, c: (0, 0)),
                  pl.BlockSpec((1, HEAD_DIM_A), lambda i, c: (0, 0)),
                  pl.BlockSpec((1, N_HEADS_A, HEAD_DIM_A, HEAD_DIM_A), lambda i, c: (i, 0, 0, 0))],
        out_specs=[pl.BlockSpec((1, tb, WIDTH_A), lambda i, c: (i, c, 0)),
                   pl.BlockSpec((1, N_HEADS_A, HEAD_DIM_A, HEAD_DIM_A), lambda i, c: (i, 0, 0, 0))],
        out_shape=[jax.ShapeDtypeStruct((b, t, WIDTH_A), f32),
                   jax.ShapeDtypeStruct((b, N_HEADS_A, HEAD_DIM_A, HEAD_DIM_A), f32)],
        scratch_shapes=[pltpu.VMEM((N_HEADS_A, HEAD_DIM_A, HEAD_DIM_A), f32)],
        compiler_params=pltpu.CompilerParams(
            dimension_semantics=("arbitrary", "arbitrary"), vmem_limit_bytes=VMEM_LIMIT),
        name="delta_prompt",
    )(cs, ab, z, nega, dtb, gn, s0)


def _delta_sample_body(*refs, nb, t, aliased):
    src_ref, ab_ref, z_ref, cw_ref, nega_ref, dtb_ref, gn_ref, s0_ref = refs[:8]
    o_ref, s_out_ref = refs[-2:]
    rowi = lax.broadcasted_iota(jnp.int32, (SUBLANES, LANES), 0)
    real = rowi < t
    off = SUBLANES - t - (CONV_W - 1)
    seqs = range(nb)
    items = [(b, h) for b in seqs for h in HEADS]
    src = [src_ref[b] for b in seqs]
    conv = [sum(pltpu.roll(x, (SUBLANES - off - j) % SUBLANES, axis=0) * cw_ref[j:j + 1, :]
                for j in range(CONV_W)) for x in src]
    cs = [x * _sigmoid(x) for x in conv]
    log_a = [jnp.where(real, nega_ref[...] * _softplus(ab_ref[b] + dtb_ref[...]), 0.0) for b in seqs]
    beta_all = [jnp.where(real, _sigmoid(ab_ref[b]), 0.0) for b in seqs]

    def part(b, h, which):
        lo = which * WIDTH_A + h * HEAD_DIM_A
        return cs[b][:, lo:lo + HEAD_DIM_A]

    def l2n(x, scale):
        return x * (lax.rsqrt(jnp.sum(x * x, axis=-1, keepdims=True) + EPS) * scale)

    q = [l2n(part(b, h, 0), HEAD_DIM_A ** -0.5) for b, h in items]
    k = [l2n(part(b, h, 1), 1.0) for b, h in items]
    v = [part(b, h, 2) for b, h in items]
    beta = [beta_all[b][:, N_HEADS_A + h:N_HEADS_A + h + 1] for b, h in items]
    gb = [jnp.broadcast_to(log_a[b][:, h:h + 1], (SUBLANES, LANES)) for b, h in items]
    cum = [sum(jnp.where(rowi >= j, x[j:j + 1, :], 0.0) for j in range(t)) for x in gb]
    g_last = [x[t - 1:t, :] for x in cum]
    e_g = [jnp.exp(x) for x in cum]
    kb = [x * y for x, y in zip(k, beta)]
    w = [x * y for x, y in zip(kb, e_g)]
    u = [x * y for x, y in zip(v, beta)]
    dec = [[jnp.exp(jnp.where(rowi >= j, x - x[j:j + 1, :], NEG)) for j in range(t)] for x in cum]
    qk_cols = [[jnp.sum(q[n] * k[n][j:j + 1, :], axis=-1, keepdims=True) * dec[n][j]
                for j in range(t)] for n in range(len(items))]
    a_cols = [[jnp.where(rowi > j, jnp.sum(kb[n] * k[n][j:j + 1, :], axis=-1, keepdims=True)
                         * dec[n][j], 0.0) for j in range(t - 1)] for n in range(len(items))]
    for j in range(t - 1):
        w = [x - a[j] * x[j:j + 1, :] for x, a in zip(w, a_cols)]
        u = [x - a[j] * x[j:j + 1, :] for x, a in zip(u, a_cols)]
    qg = [x * y for x, y in zip(q, e_g)]
    kd = [x * jnp.exp(gl - cm) for x, gl, cm in zip(k, g_last, cum)]
    s_old = [s0_ref[0, b, h] for b, h in items]
    s_sp = [_split(x) for x in s_old]
    v_new = [uu - _mm3s(_split(ww), ss) for uu, ww, ss in zip(u, w, s_sp)]
    o = [_mm3s(_split(x), ss) for x, ss in zip(qg, s_sp)]
    o = [x + sum(c[j] * vn[j:j + 1, :] for j in range(t)) for x, c, vn in zip(o, qk_cols, v_new)]
    s_new = [so * jnp.exp(gl) + _mm3(x, vn, TN) for so, gl, x, vn in zip(s_old, g_last, kd, v_new)]
    for n, (b, h) in enumerate(items):
        lo = h * HEAD_DIM_A
        s_out_ref[0, b, h] = s_new[n]
        zz = z_ref[b, :, lo:lo + HEAD_DIM_A]
        o_ref[b, :, lo:lo + HEAD_DIM_A] = _rms(o[n], gn_ref[...]) * (zz * _sigmoid(zz))


def _delta_sample(src, ab, z, cw, nega, dtb, gn, s_all, s_prev, layer, t, nb=4):
    b = src.shape[0]
    aliased = s_prev is not None
    state_spec = pl.BlockSpec((1, nb, N_HEADS_A, HEAD_DIM_A, HEAD_DIM_A),
                              lambda i: (layer, i, 0, 0, 0))
    in_specs = [pl.BlockSpec((nb, SUBLANES, 3 * WIDTH_A), lambda i: (i, 0, 0)),
                pl.BlockSpec((nb, SUBLANES, LANES), lambda i: (i, 0, 0)),
                pl.BlockSpec((nb, SUBLANES, WIDTH_A), lambda i: (i, 0, 0)),
                pl.BlockSpec((CONV_W, 3 * WIDTH_A), lambda i: (0, 0)),
                pl.BlockSpec((1, LANES), lambda i: (0, 0)),
                pl.BlockSpec((1, LANES), lambda i: (0, 0)),
                pl.BlockSpec((1, HEAD_DIM_A), lambda i: (0, 0)),
                state_spec]
    args = [src, ab, z, cw, nega, dtb, gn, s_all]
    if aliased:
        in_specs.append(pl.BlockSpec(memory_space=pl.ANY))
        args.append(s_prev)
    return pl.pallas_call(
        functools.partial(_delta_sample_body, nb=nb, t=t, aliased=aliased),
        grid=(b // nb,),
        in_specs=in_specs,
        out_specs=[pl.BlockSpec((nb, SUBLANES, WIDTH_A), lambda i: (i, 0, 0)), state_spec],
        out_shape=[jax.ShapeDtypeStruct((b, SUBLANES, WIDTH_A), f32),
                   jax.ShapeDtypeStruct(s_all.shape, f32)],
        input_output_aliases={len(args) - 1: 1} if aliased else {},
        compiler_params=pltpu.CompilerParams(
            dimension_semantics=("arbitrary",), vmem_limit_bytes=VMEM_LIMIT),
        name="delta_sample",
    )(*args)


ATT_TB = 2048
Q_CHUNKS = WIDTH_G // LANES
KV_CHUNKS = 2 * WIDTH_G // LANES


def _attn_prompt_body(q_ref, kvh_ref, kvc_ref, o_ref, l_ref, q_scr, kv_scr, o_scr, l_scr, *,
                      dil, slopes, tb):
    i = pl.program_id(1)
    halo = N_OFF * dil
    for c in range(Q_CHUNKS):
        q_scr[c] = q_ref[0, :, c * LANES:(c + 1) * LANES]
    for c in range(KV_CHUNKS):
        kv_scr[c, pl.ds(0, halo), :] = kvh_ref[0, :, c * LANES:(c + 1) * LANES]
        kv_scr[c, pl.ds(halo, tb), :] = kvc_ref[0, :, c * LANES:(c + 1) * LANES]
    a = lax.broadcasted_iota(jnp.int32, (N_OFF, 2 * N_OFF), 0)
    cc = lax.broadcasted_iota(jnp.int32, (N_OFF, 2 * N_OFF), 1)
    delta = a - cc + N_OFF
    in_win = (delta >= 0) & (delta <= N_OFF)
    dist = delta.astype(f32) * float(dil)
    upper = lax.broadcasted_iota(jnp.int32, (N_OFF, LANES), 1) >= HEAD_DIM_B
    n_sub = tb // halo

    def sub_block(idx, carry):
        r = idx // n_sub
        u = idx - r * n_sub
        base = r + halo * u
        key_tok = i * tb - halo + base + dil * cc
        mask = in_win & (key_tok >= 0)
        cs_ = range(Q_CHUNKS)
        hs = range(HEADS_PER_GROUP)
        qc = [q_scr[c, pl.ds(base, N_OFF, stride=dil), :] * HEAD_DIM_B ** -0.5 for c in cs_]
        kc = [kv_scr[c, pl.ds(base, 2 * N_OFF, stride=dil), :].astype(bf16) for c in cs_]
        vc = [kv_scr[Q_CHUNKS + c, pl.ds(base, 2 * N_OFF, stride=dil), :].astype(bf16) for c in cs_]
        qm = [jnp.where(upper if h % 2 else jnp.logical_not(upper), qc[h // 2], 0.0).astype(bf16)
              for h in hs]
        s = [jnp.where(mask, _dot(qm[h], kc[h // 2], NT) - slopes[h] * dist, NEG) for h in hs]
        m = [jnp.max(s[h], axis=-1, keepdims=True) for h in hs]
        p = [jnp.exp(s[h] - m[h]) for h in hs]
        den = [jnp.sum(p[h], axis=-1, keepdims=True) for h in hs]
        pv = [_dot(p[h].astype(bf16), vc[h // 2]) / den[h] for h in hs]
        lse = [jnp.broadcast_to(m[h] + jnp.log(den[h]), (N_OFF, LANES)) for h in hs]
        for c in cs_:
            o_scr[c, pl.ds(base, N_OFF, stride=dil), :] = jnp.where(upper, pv[2 * c + 1], pv[2 * c])
            l_scr[c, pl.ds(base, N_OFF, stride=dil), :] = jnp.where(upper, lse[2 * c + 1], lse[2 * c])
        return carry

    lax.fori_loop(0, tb // N_OFF, sub_block, 0, unroll=2)
    for c in range(Q_CHUNKS):
        o_ref[0, :, c * LANES:(c + 1) * LANES] = o_scr[c]
        l_ref[0, :, c * LANES:(c + 1) * LANES] = l_scr[c]


def _attn_prompt(q, kv, g):
    b, t, _ = q.shape
    dil = DILATIONS[g]
    halo = N_OFF * dil
    tb = min(ATT_TB, t)
    assert tb % halo == 0 and t % tb == 0
    per = tb // halo
    return pl.pallas_call(
        functools.partial(_attn_prompt_body, dil=dil, slopes=_slopes(g), tb=tb),
        grid=(b, t // tb),
        in_specs=[pl.BlockSpec((1, tb, WIDTH_G), lambda bi, i: (bi, i, 0)),
                  pl.BlockSpec((1, halo, 2 * WIDTH_G),
                               lambda bi, i: (bi, jnp.maximum(i * per - 1, 0), 0)),
                  pl.BlockSpec((1, tb, 2 * WIDTH_G), lambda bi, i: (bi, i, 0))],
        out_specs=[pl.BlockSpec((1, tb, WIDTH_G), lambda bi, i: (bi, i, 0)),
                   pl.BlockSpec((1, tb, WIDTH_G), lambda bi, i: (bi, i, 0))],
        out_shape=[jax.ShapeDtypeStruct((b, t, WIDTH_G), f32),
                   jax.ShapeDtypeStruct((b, t, WIDTH_G), f32)],
        scratch_shapes=[pltpu.VMEM((Q_CHUNKS, tb, LANES), f32),
                        pltpu.VMEM((KV_CHUNKS, halo + tb, LANES), f32),
                        pltpu.VMEM((Q_CHUNKS, tb, LANES), f32),
                        pltpu.VMEM((Q_CHUNKS, tb, LANES), f32)],
        compiler_params=pltpu.CompilerParams(
            dimension_semantics=("arbitrary", "arbitrary"), vmem_limit_bytes=VMEM_LIMIT),
        name=f"attn_prompt_g{g}",
    )(q, kv, kv)


def _attn_sample_body(*refs, dil, slopes, nb, t, win):
    q_ref, kn_ref, vn_ref, kvn_ref, c_ref = refs[:5]
    o_ref, l_ref, cn_ref = refs[-3:]
    rr = lax.broadcasted_iota(jnp.int32, (SUBLANES, win), 1)
    qi = lax.broadcasted_iota(jnp.int32, (SUBLANES, win), 0)
    valid = ((rr & (dil - 1)) == (qi & (dil - 1))) & (rr >= qi) & (qi < t)
    dist = (win + qi - rr).astype(f32)
    nn = lax.broadcasted_iota(jnp.int32, (SUBLANES, SUBLANES), 1)
    qn = lax.broadcasted_iota(jnp.int32, (SUBLANES, SUBLANES), 0)
    valid_n = (nn <= qn) & (((qn - nn) & (dil - 1)) == 0) & (qn < t)
    dist_n = (qn - nn).astype(f32)
    lane = lax.broadcasted_iota(jnp.int32, (HEAD_DIM_B, LANES), 1)
    tail_lanes = lane >= LANES - t
    e_rows = lax.broadcasted_iota(jnp.int32, (SUBLANES, LANES), 0)
    e_lane = lax.broadcasted_iota(jnp.int32, (SUBLANES, LANES), 1)
    place = jnp.where((e_lane == e_rows + (LANES - t)) & (e_rows < t), 1.0, 0.0).astype(bf16)

    def one_sequence(b, carry):
        k3 = _split3(kvn_ref[b])
        new_t = _dot(k3[0], place, TN) + _dot(k3[1], place, TN) + _dot(k3[2], place, TN)
        hs = range(HEADS_PER_GROUP)
        k_t = [c_ref[0, b, 0, h] for h in hs]
        v_t = [c_ref[0, b, 1, h] for h in hs]
        q8 = [(q_ref[b, h] * HEAD_DIM_B ** -0.5).astype(bf16) for h in hs]
        s = [jnp.where(valid, _dot(q8[h], k_t[h].astype(bf16)) - slopes[h] * dist, NEG) for h in hs]
        sn = [jnp.where(valid_n, _dot(q8[h], kn_ref[b, h].astype(bf16), NT) - slopes[h] * dist_n, NEG)
              for h in hs]
        m = [jnp.maximum(jnp.max(s[h], axis=-1, keepdims=True),
                         jnp.max(sn[h], axis=-1, keepdims=True)) for h in hs]
        p = [jnp.exp(s[h] - m[h]) for h in hs]
        pn = [jnp.exp(sn[h] - m[h]) for h in hs]
        den = [jnp.sum(p[h], axis=-1, keepdims=True) + jnp.sum(pn[h], axis=-1, keepdims=True)
               for h in hs]
        acc = [_dot(p[h].astype(bf16), v_t[h].astype(bf16), NT) + _mm1(pn[h], vn_ref[b, h])
               for h in hs]
        for h in hs:
            o_ref[b, :, h * HEAD_DIM_B:(h + 1) * HEAD_DIM_B] = acc[h] / den[h]
            l_ref[b, :, h * HEAD_DIM_B:(h + 1) * HEAD_DIM_B] = jnp.broadcast_to(
                m[h] + jnp.log(den[h]), (SUBLANES, HEAD_DIM_B))
            for kv, x in ((0, k_t[h]), (1, v_t[h])):
                rolled = pltpu.roll(x, win - t, axis=1)
                fresh = new_t[(kv * HEADS_PER_GROUP + h) * HEAD_DIM_B:
                              (kv * HEADS_PER_GROUP + h + 1) * HEAD_DIM_B, :]
                if win > LANES:
                    cn_ref[0, b, kv, h, :, pl.ds(0, win - LANES)] = rolled[:, :win - LANES]
                cn_ref[0, b, kv, h, :, pl.ds(win - LANES, LANES)] = jnp.where(
                    tail_lanes, fresh, rolled[:, win - LANES:])
        return carry

    lax.fori_loop(0, nb, one_sequence, 0, unroll=2 if nb % 2 == 0 else 1)


def _attn_sample(q, kvn, cache_t, prev, layer, g, t):
    bsz, win = cache_t.shape[1], cache_t.shape[-1]
    dil = DILATIONS[g]
    nb = max(1, min(16, 2048 // win))
    pad = SUBLANES - t
    heads = lambda a_: jnp.pad(a_.reshape(bsz, t, HEADS_PER_GROUP, HEAD_DIM_B).transpose(0, 2, 1, 3),
                               ((0, 0), (0, 0), (0, pad), (0, 0)))
    q4 = heads(q)
    kn4 = heads(kvn[:, :WIDTH_G])
    vn4 = heads(kvn[:, WIDTH_G:])
    kvn8 = jnp.pad(kvn.reshape(bsz, t, 2 * WIDTH_G), ((0, 0), (0, pad), (0, 0)))
    aliased = prev is not None
    head_spec = pl.BlockSpec((nb, HEADS_PER_GROUP, SUBLANES, HEAD_DIM_B), lambda i: (i, 0, 0, 0))
    cache_spec = pl.BlockSpec((1, nb, 2, HEADS_PER_GROUP, HEAD_DIM_B, win),
                              lambda i: (layer, i, 0, 0, 0, 0))
    out_spec = pl.BlockSpec((nb, SUBLANES, WIDTH_G), lambda i: (i, 0, 0))
    in_specs = [head_spec, head_spec, head_spec,
                pl.BlockSpec((nb, SUBLANES, 2 * WIDTH_G), lambda i: (i, 0, 0)), cache_spec]
    args = [q4, kn4, vn4, kvn8, cache_t]
    if aliased:
        in_specs.append(pl.BlockSpec(memory_space=pl.ANY))
        args.append(prev)
    o, l, cn = pl.pallas_call(
        functools.partial(_attn_sample_body, dil=dil, slopes=_slopes(g), nb=nb, t=t, win=win),
        grid=(bsz // nb,),
        in_specs=in_specs,
        out_specs=[out_spec, out_spec, cache_spec],
        out_shape=[jax.ShapeDtypeStruct((bsz, SUBLANES, WIDTH_G), f32),
                   jax.ShapeDtypeStruct((bsz, SUBLANES, WIDTH_G), f32),
                   jax.ShapeDtypeStruct(cache_t.shape, f32)],
        input_output_aliases={len(args) - 1: 2} if aliased else {},
        compiler_params=pltpu.CompilerParams(
            dimension_semantics=("arbitrary",), vmem_limit_bytes=VMEM_LIMIT),
        name=f"attn_sample_g{g}",
    )(*args)
    return (o[:, :t].reshape(bsz * t, WIDTH_G), l[:, :t].reshape(bsz * t, WIDTH_G), cn)


FF_CHUNK = 512


def _tail_body(x_ref, oa_ref, o0_ref, o1_ref, o2_ref, l0_ref, l1_ref, l2_ref, gate_ref,
               wa_ref, wb_ref, wo_ref, n2_ref, wu_ref, wd_ref, fn_ref, y_ref, *, final):
    l0, l1, l2 = l0_ref[...], l1_ref[...], l2_ref[...]
    lm = jnp.maximum(jnp.maximum(l0, l1), l2)
    e0, e1, e2 = jnp.exp(l0 - lm), jnp.exp(l1 - lm), jnp.exp(l2 - lm)
    den = e0 + e1 + e2
    ob = (e0 / den) * o0_ref[...] + (e1 / den) * o1_ref[...] + (e2 / den) * o2_ref[...]
    ya = _mm1(oa_ref[...], wa_ref[...])
    yb = _mm1(ob, wb_ref[...])
    merged = _sigmoid(gate_ref[:, :D_MODEL]) * ya + _sigmoid(gate_ref[:, D_MODEL:]) * yb
    x1 = x_ref[...] + _mm1(merged, wo_ref[...])
    h2 = _rms(x1, n2_ref[...]).astype(bf16)
    acc = x1
    for j in range(D_FF // FF_CHUNK):
        up = _dot(h2, wu_ref[:, j * FF_CHUNK:(j + 1) * FF_CHUNK])
        act = jnp.square(jnp.maximum(up, 0.0)).astype(bf16)
        acc = acc + _dot(act, wd_ref[j * FF_CHUNK:(j + 1) * FF_CHUNK, :])
    if final:
        acc = _rms(acc, fn_ref[...])
    y_ref[...] = acc


def _tail(x, oa, outs, lses, gate, wa, wb, wo, n2, wu, wd, fn, final, tm=512):
    m = x.shape[0]
    tm = min(tm, m)
    assert m % tm == 0
    tok = lambda wd_: pl.BlockSpec((tm, wd_), lambda i: (i, 0))
    const = lambda shape: pl.BlockSpec(shape, lambda i: (0, 0), pipeline_mode=pl.Buffered(1))
    return pl.pallas_call(
        functools.partial(_tail_body, final=final),
        grid=(m // tm,),
        in_specs=[tok(D_MODEL), tok(WIDTH_A)] + [tok(WIDTH_G)] * 6 + [tok(2 * D_MODEL),
                  const((WIDTH_A, D_MODEL)), const((WIDTH_G, D_MODEL)), const((D_MODEL, D_MODEL)),
                  const((1, D_MODEL)), const((D_MODEL, D_FF)), const((D_FF, D_MODEL)),
                  const((1, D_MODEL))],
        out_specs=tok(D_MODEL),
        out_shape=jax.ShapeDtypeStruct((m, D_MODEL), f32),
        compiler_params=pltpu.CompilerParams(
            dimension_semantics=("arbitrary",), vmem_limit_bytes=VMEM_LIMIT),
        name="tail",
    )(x, oa, *outs, *lses, gate, wa, wb, wo, n2, wu, wd, fn)


def _lane_row(vals):
    return jnp.zeros((1, LANES), f32).at[0, :vals.shape[0]].set(vals.astype(f32))


def _layer_params(l, norm1, w_in, conv_w, a_log, dt_bias, gnorm_a, w_br_a, w_br_b, w_out, norm2,
                  w_up, w_down):
    return dict(
        n1=norm1[l].reshape(1, D_MODEL), w_in=_permute_w_in(w_in[l]), cw=conv_w[l],
        nega=_lane_row(-jnp.exp(a_log[l].astype(f32))), dtb=_lane_row(dt_bias[l]),
        gn=gnorm_a[l].reshape(1, HEAD_DIM_A).astype(f32),
        wa=w_br_a[l].astype(bf16), wb=w_br_b[l].astype(bf16), wo=w_out[l].astype(bf16),
        n2=norm2[l].reshape(1, D_MODEL), wu=w_up[l].astype(bf16), wd=w_down[l].astype(bf16))


def _prompt_layer(x, p, fn, final):
    b, t, _ = x.shape
    xf = x.reshape(b * t, D_MODEL)
    cs, z, q0, q1, q2, kv0, kv1, kv2, gate, ab, last = _inproj(xf, p["n1"], p["w_in"],
                                                               conv_w=p["cw"], seq_len=t)
    s0 = jnp.zeros((b, N_HEADS_A, HEAD_DIM_A, HEAD_DIM_A), f32)
    oa, s_new = _delta_prompt(cs.reshape(b, t, 3 * WIDTH_A), ab.reshape(b, t, LANES),
                              z.reshape(b, t, WIDTH_A), p["nega"], p["dtb"], p["gn"], s0)
    outs, lses, wins = [], [], []
    for g, (qg, kvg) in enumerate(((q0, kv0), (q1, kv1), (q2, kv2))):
        kv3 = kvg.reshape(b, t, 2 * WIDTH_G)
        o, l = _attn_prompt(qg.reshape(b, t, WIDTH_G), kv3, g)
        outs.append(o.reshape(b * t, WIDTH_G))
        lses.append(l.reshape(b * t, WIDTH_G))
        keep = min(WINDOWS[g], t)
        wins.append(kv3[:, t - keep:].reshape(b, keep, 2, HEADS_PER_GROUP, HEAD_DIM_B))
    y = _tail(xf, oa.reshape(b * t, WIDTH_A), outs, lses, gate, p["wa"], p["wb"], p["wo"], p["n2"],
              p["wu"], p["wd"], fn, final)
    return y.reshape(b, t, D_MODEL), wins, s_new, last[:, SUBLANES - (CONV_W - 1):]


def _sample_layer(x, p, fn, final, layer, caches_t, prev_caches, s_all, s_prev, conv_buf):
    b, t, _ = x.shape
    xf = x.reshape(b * t, D_MODEL)
    qkv, z, q0, q1, q2, kv0, kv1, kv2, gate, ab = _inproj(xf, p["n1"], p["w_in"])
    qkv3 = qkv.reshape(b, t, 3 * WIDTH_A)
    pad = SUBLANES - t
    src = jnp.concatenate([jnp.zeros((b, SUBLANES - t - (CONV_W - 1), 3 * WIDTH_A), f32),
                           conv_buf.astype(f32), qkv3], axis=1)
    pad_rows = lambda a_: jnp.pad(a_, ((0, 0), (0, pad), (0, 0)))
    oa, s_stack = _delta_sample(src, pad_rows(ab.reshape(b, t, LANES)),
                                pad_rows(z.reshape(b, t, WIDTH_A)), p["cw"], p["nega"], p["dtb"],
                                p["gn"], s_all, s_prev, layer, t)
    oa = oa[:, :t].reshape(b * t, WIDTH_A)
    outs, lses, new_caches = [], [], []
    for g, (qg, kvg) in enumerate(((q0, kv0), (q1, kv1), (q2, kv2))):
        o, l, cn = _attn_sample(qg, kvg, caches_t[g], prev_caches[g], layer, g, t)
        outs.append(o)
        lses.append(l)
        new_caches.append(cn)
    y = _tail(xf, oa, outs, lses, gate, p["wa"], p["wb"], p["wo"], p["n2"], p["wu"], p["wd"],
              fn, final)
    return y.reshape(b, t, D_MODEL), new_caches, s_stack, qkv3[:, t - (CONV_W - 1):]


def kernel(x_prompt, x_sample, cache_win0, cache_win1, cache_win2, state_delta, state_conv, norm1,
           w_in, conv_w, a_log, dt_bias, gnorm_a, w_br_a, w_br_b, w_out, norm2, w_up, w_down,
           final_norm):
    depth = w_in.shape[0]
    assert x_sample.shape[1] + CONV_W - 1 <= SUBLANES
    fn = final_norm.reshape(1, D_MODEL)
    caches_t = [c.transpose(0, 1, 3, 4, 5, 2) for c in (cache_win0, cache_win1, cache_win2)]
    new_caches = [None] * N_GROUPS
    s_stack = None
    xp, xs = x_prompt, x_sample
    wins_p = [[] for _ in range(N_GROUPS)]
    delta_p, conv_p, conv_s = [], [], []
    for l in range(depth):
        p = _layer_params(l, norm1, w_in, conv_w, a_log, dt_bias, gnorm_a, w_br_a, w_br_b, w_out,
                          norm2, w_up, w_down)
        final = l == depth - 1
        xp, wp, sp, cp = _prompt_layer(xp, p, fn, final)
        xs, new_caches, s_stack, cs = _sample_layer(xs, p, fn, final, l, caches_t, new_caches,
                                                    state_delta, s_stack, state_conv[l])
        for g in range(N_GROUPS):
            wins_p[g].append(wp[g])
        delta_p.append(sp)
        conv_p.append(cp)
        conv_s.append(cs)
    st = jnp.stack
    wins_s = [c.transpose(0, 1, 5, 2, 3, 4) for c in new_caches]
    return (xp, xs, st(wins_p[0]), st(wins_p[1]), st(wins_p[2]), st(delta_p), st(conv_p),
            wins_s[0], wins_s[1], wins_s[2], s_stack, st(conv_s))
```

```python
import functools

import jax
import jax.numpy as jnp
from jax import lax
from jax.experimental import pallas as pl
from jax.experimental.pallas import tpu as pltpu

f32 = jnp.float32
bf16 = jnp.bfloat16

D_MODEL = 1024
N_HEADS_A = 4
HEAD_DIM_A = 128
WIDTH_A = N_HEADS_A * HEAD_DIM_A
CONV_W = 4
WINDOWS = (128, 512, 2048)
DILATIONS = (1, 4, 16)
N_GROUPS = 3
HEADS_PER_GROUP = 4
HEAD_DIM_B = 64
N_HEADS_B = N_GROUPS * HEADS_PER_GROUP
WIDTH_G = HEADS_PER_GROUP * HEAD_DIM_B
D_FF = 4 * D_MODEL
EPS = 1e-6
N_OFF = 128
NEG = -1e30

SUBLANES = 8
LANES = 128
VMEM_LIMIT = 56 * 1024 * 1024

SEG_WIDTHS = (3 * WIDTH_A, WIDTH_A, WIDTH_G, WIDTH_G, WIDTH_G,
              2 * WIDTH_G, 2 * WIDTH_G, 2 * WIDTH_G, 2 * D_MODEL, LANES)
N_PERM = sum(SEG_WIDTHS)

NN = (((1,), (0,)), ((), ()))
NT = (((1,), (1,)), ((), ()))
TN = (((0,), (0,)), ((), ()))


def _slopes(g):
    return tuple(2.0 ** (-8.0 * (g * HEADS_PER_GROUP + h + 1) / N_HEADS_B)
                 for h in range(HEADS_PER_GROUP))


def _dot(a, b, dims=NN):
    return lax.dot_general(a, b, dims, preferred_element_type=f32)


def _mm1(a, b, dims=NN):
    return _dot(a.astype(bf16), b.astype(bf16), dims)


def _split(a):
    hi = a.astype(bf16)
    lo = (a - hi.astype(f32)).astype(bf16)
    return hi, lo


def _split3(a):
    hi = a.astype(bf16)
    r = a - hi.astype(f32)
    mid = r.astype(bf16)
    lo = (r - mid.astype(f32)).astype(bf16)
    return hi, mid, lo


def _mm3s(a, b, dims=NN):
    return _dot(a[0], b[0], dims) + _dot(a[0], b[1], dims) + _dot(a[1], b[0], dims)


def _mm3(a, b, dims=NN):
    return _mm3s(_split(a), _split(b), dims)


def _sigmoid(x):
    return 1.0 / (1.0 + jnp.exp(-x))


def _softplus(x):
    return jnp.maximum(x, 0.0) + jnp.log1p(jnp.exp(-jnp.abs(x)))


def _rms(x, g):
    return x * lax.rsqrt(jnp.mean(x * x, axis=-1, keepdims=True) + EPS) * g


def _inproj_body(x_ref, g_ref, w_ref, *out_refs):
    h = _rms(x_ref[...], g_ref[...]).astype(bf16)
    off = 0
    for o_ref, width in zip(out_refs, SEG_WIDTHS):
        o_ref[...] = _dot(h, w_ref[:, off:off + width])
        off += width


def _inproj_conv_body(x_ref, g_ref, w_ref, cw_ref, *refs, tm, tiles_per_seq):
    out_refs, last_ref, xs_ref = refs[:len(SEG_WIDTHS)], refs[-2], refs[-1]
    i = pl.program_id(0)

    @pl.when(lax.rem(i, tiles_per_seq) == 0)
    def _():
        xs_ref[pl.ds(0, SUBLANES), :] = jnp.zeros((SUBLANES, SEG_WIDTHS[0]), f32)

    h = _rms(x_ref[...], g_ref[...]).astype(bf16)
    xs_ref[pl.ds(SUBLANES, tm), :] = _dot(h, w_ref[:, :SEG_WIDTHS[0]])
    conv = xs_ref[pl.ds(SUBLANES - CONV_W + 1, tm), :] * cw_ref[0:1, :]
    for j in range(1, CONV_W):
        conv = conv + xs_ref[pl.ds(SUBLANES - CONV_W + 1 + j, tm), :] * cw_ref[j:j + 1, :]
    last = xs_ref[pl.ds(tm, SUBLANES), :]
    xs_ref[pl.ds(0, SUBLANES), :] = last
    last_ref[0] = last
    out_refs[0][...] = conv * _sigmoid(conv)
    off = SEG_WIDTHS[0]
    for o_ref, width in zip(out_refs[1:], SEG_WIDTHS[1:]):
        o_ref[...] = _dot(h, w_ref[:, off:off + width])
        off += width


def _inproj(x, g, w, tm=256, conv_w=None, seq_len=None):
    m = x.shape[0]
    tm = min(tm, m)
    assert m % tm == 0
    in_specs = [pl.BlockSpec((tm, D_MODEL), lambda i: (i, 0)),
                pl.BlockSpec((1, D_MODEL), lambda i: (0, 0)),
                pl.BlockSpec((D_MODEL, N_PERM), lambda i: (0, 0), pipeline_mode=pl.Buffered(1))]
    out_specs = [pl.BlockSpec((tm, wd), lambda i: (i, 0)) for wd in SEG_WIDTHS]
    out_shape = [jax.ShapeDtypeStruct((m, wd), f32) for wd in SEG_WIDTHS]
    args = [x, g, w]
    body, scratch = _inproj_body, []
    if conv_w is not None:
        assert seq_len % tm == 0 and tm >= SUBLANES
        per = seq_len // tm
        in_specs.append(pl.BlockSpec((CONV_W, SEG_WIDTHS[0]), lambda i: (0, 0)))
        out_specs.append(pl.BlockSpec((1, SUBLANES, SEG_WIDTHS[0]), lambda i: (i // per, 0, 0)))
        out_shape.append(jax.ShapeDtypeStruct((m // seq_len, SUBLANES, SEG_WIDTHS[0]), f32))
        args.append(conv_w)
        body = functools.partial(_inproj_conv_body, tm=tm, tiles_per_seq=per)
        scratch = [pltpu.VMEM((tm + SUBLANES, SEG_WIDTHS[0]), f32)]
    return pl.pallas_call(
        body,
        grid=(m // tm,),
        in_specs=in_specs,
        out_specs=out_specs,
        out_shape=out_shape,
        scratch_shapes=scratch,
        compiler_params=pltpu.CompilerParams(
            dimension_semantics=("arbitrary",), vmem_limit_bytes=VMEM_LIMIT),
        name="inproj",
    )(*args)


def _permute_w_in(w):
    o_z = 3 * WIDTH_A
    o_ab = o_z + WIDTH_A
    o_q = o_ab + 2 * N_HEADS_A
    o_k = o_q + N_GROUPS * WIDTH_G
    o_v = o_k + N_GROUPS * WIDTH_G
    o_gate = o_v + N_GROUPS * WIDTH_G
    parts = [w[:, :o_ab], w[:, o_q:o_k]]
    for g in range(N_GROUPS):
        parts.append(w[:, o_k + g * WIDTH_G:o_k + (g + 1) * WIDTH_G])
        parts.append(w[:, o_v + g * WIDTH_G:o_v + (g + 1) * WIDTH_G])
    parts.append(w[:, o_gate:])
    parts.append(w[:, o_ab:o_q])
    parts.append(jnp.zeros((w.shape[0], LANES - 2 * N_HEADS_A), w.dtype))
    return jnp.concatenate(parts, axis=1).astype(bf16)


CHUNK = 128
INV_BASE = 16
HEADS = tuple(range(N_HEADS_A))


def _tri_inverse_heads(a_list, row, col):
    eye = jnp.where(row == col, 1.0, 0.0)
    blk = lambda idx, size: jnp.right_shift(idx, size.bit_length() - 1)
    same16 = blk(row, INV_BASE) == blk(col, INV_BASE)
    ad = [jnp.where(same16, a, 0.0) for a in a_list]
    p = [eye - x for x in ad]
    xs = [_split(x) for x in ad]
    n = 2
    while n < INV_BASE:
        x = [_mm3s(s, s) for s in xs]
        xs = [_split(v) for v in x]
        ps = [_split(v) for v in p]
        p = [pv + _mm3s(pp, xx) for pv, pp, xx in zip(p, ps, xs)]
        n *= 2
    s = INV_BASE
    while s < CHUNK:
        pair = (blk(row, 2 * s) == blk(col, 2 * s)) & (blk(row, s) != blk(col, s))
        es = [_split(jnp.where(pair, a, 0.0)) for a in a_list]
        ps = [_split(v) for v in p]
        pe = [_split(_mm3s(pp, ee)) for pp, ee in zip(ps, es)]
        p = [pv - _mm3s(x, pp) for pv, x, pp in zip(p, pe, ps)]
        s *= 2
    return [_split(v) for v in p]


def _delta_prompt_body(cs_ref, ab_ref, z_ref, nega_ref, dtb_ref, gn_ref, s0_ref,
                       o_ref, s_out_ref, s_ref, *, nchunk):
    c = pl.program_id(1)

    @pl.when(c == 0)
    def _():
        s_ref[...] = s0_ref[0]

    items = [(ck, h) for ck in range(nchunk) for h in HEADS]
    rows = lambda ck: pl.ds(ck * CHUNK, CHUNK)
    log_a = [nega_ref[...] * _softplus(ab_ref[0, rows(ck), :] + dtb_ref[...])
             for ck in range(nchunk)]
    beta_all = [_sigmoid(ab_ref[0, rows(ck), :]) for ck in range(nchunk)]

    row = lax.broadcasted_iota(jnp.int32, (CHUNK, CHUNK), 0)
    col = lax.broadcasted_iota(jnp.int32, (CHUNK, CHUNK), 1)
    causal = row >= col
    strict = row > col
    ltri = jnp.where(causal, 1.0, 0.0).astype(bf16)

    def l2n(x, scale):
        return x * (lax.rsqrt(jnp.sum(x * x, axis=-1, keepdims=True) + EPS) * scale)

    def part(ck, h, which):
        lo = which * WIDTH_A + h * HEAD_DIM_A
        return cs_ref[0, rows(ck), lo:lo + HEAD_DIM_A]

    q = [l2n(part(ck, h, 0), HEAD_DIM_A ** -0.5) for ck, h in items]
    k = [l2n(part(ck, h, 1), 1.0) for ck, h in items]
    v = [part(ck, h, 2) for ck, h in items]
    beta = [beta_all[ck][:, N_HEADS_A + h:N_HEADS_A + h + 1] for ck, h in items]
    g3 = [_split3(jnp.broadcast_to(log_a[ck][:, h:h + 1], (CHUNK, HEAD_DIM_A))) for ck, h in items]
    cum = [_dot(ltri, t[0]) + _dot(ltri, t[1]) + _dot(ltri, t[2]) for t in g3]
    decay = [jnp.exp(jnp.where(causal, x - x.T, NEG)) for x in cum]
    e_g = [jnp.exp(x) for x in cum]
    g_last = [x[CHUNK - 1:CHUNK, :] for x in cum]
    kb = [a * b for a, b in zip(k, beta)]
    k_bf = [x.astype(bf16) for x in k]
    a_mat = [jnp.where(strict, _dot(x.astype(bf16), y, NT) * d, 0.0)
             for x, y, d in zip(kb, k_bf, decay)]
    t_inv = _tri_inverse_heads(a_mat, row, col)
    w = [_mm3s(t, _split(x * e)) for t, x, e in zip(t_inv, kb, e_g)]
    u = [_mm3s(t, _split(x * b)) for t, x, b in zip(t_inv, v, beta)]
    qk = [_dot(x.astype(bf16), y, NT) * d for x, y, d in zip(q, k_bf, decay)]
    qg = [x * e for x, e in zip(q, e_g)]
    kd = [x * jnp.exp(gl - cm) for x, gl, cm in zip(k, g_last, cum)]
    s_cur = [s_ref[h] for h in HEADS]
    for ck in range(nchunk):
        n0 = ck * N_HEADS_A
        s_sp = [_split(x) for x in s_cur]
        v_new = [u[n0 + h] - _mm3s(_split(w[n0 + h]), s_sp[h]) for h in HEADS]
        o = [_dot(qg[n0 + h].astype(bf16), s_sp[h][0]) + _mm1(qk[n0 + h], v_new[h]) for h in HEADS]
        s_cur = [s_cur[h] * jnp.exp(g_last[n0 + h]) + _mm3(kd[n0 + h], v_new[h], TN) for h in HEADS]
        for h in HEADS:
            lo = h * HEAD_DIM_A
            zz = z_ref[0, rows(ck), lo:lo + HEAD_DIM_A]
            o_ref[0, rows(ck), lo:lo + HEAD_DIM_A] = _rms(o[h], gn_ref[...]) * (zz * _sigmoid(zz))
    for h in HEADS:
        s_ref[h] = s_cur[h]

    @pl.when(c == pl.num_programs(1) - 1)
    def _():
        s_out_ref[0] = s_ref[...]


DELTA_CHUNKS_PER_STEP = 2


def _delta_prompt(cs, ab, z, nega, dtb, gn, s0):
    b, t, _ = cs.shape
    nchunk = DELTA_CHUNKS_PER_STEP if t % (DELTA_CHUNKS_PER_STEP * CHUNK) == 0 else 1
    tb = nchunk * CHUNK
    assert t % tb == 0
    return pl.pallas_call(
        functools.partial(_delta_prompt_body, nchunk=nchunk),
        grid=(b, t // tb),
        in_specs=[pl.BlockSpec((1, tb, 3 * WIDTH_A), lambda i, c: (i, c, 0)),
                  pl.BlockSpec((1, tb, LANES), lambda i, c: (i, c, 0)),
                  pl.BlockSpec((1, tb, WIDTH_A), lambda i, c: (i, c, 0)),
                  pl.BlockSpec((1, LANES), lambda i, c: (0, 0)),
                  pl.BlockSpec((1, LANES), lambda i, c: (0, 0)),
                  pl.BlockSpec((1, HEAD_DIM_A), lambda i, c: (0, 0)),
                  pl.BlockSpec((1, N_HEADS_A, HEAD_DIM_A, HEAD_DIM_A), lambda i, c: (i, 0, 0, 0))],
        out_specs=[pl.BlockSpec((1, tb, WIDTH_A), lambda i, c: (i, c, 0)),
                   pl.BlockSpec((1, N_HEADS_A, HEAD_DIM_A, HEAD_DIM_A), lambda i, c: (i, 0, 0, 0))],
        out_shape=[jax.ShapeDtypeStruct((b, t, WIDTH_A), f32),
                   jax.ShapeDtypeStruct((b, N_HEADS_A, HEAD_DIM_A, HEAD_DIM_A), f32)],
        scratch_shapes=[pltpu.VMEM((N_HEADS_A, HEAD_DIM_A, HEAD_DIM_A), f32)],
        compiler_params=pltpu.CompilerParams(
            dimension_semantics=("arbitrary", "arbitrary"), vmem_limit_bytes=VMEM_LIMIT),
        name="delta_prompt",
    )(cs, ab, z, nega, dtb, gn, s0)


def _delta_sample_body(*refs, nb, t, aliased):
    src_ref, ab_ref, z_ref, cw_ref, nega_ref, dtb_ref, gn_ref, s0_ref = refs[:8]
    o_ref, s_out_ref = refs[-2:]
    rowi = lax.broadcasted_iota(jnp.int32, (SUBLANES, LANES), 0)
    real = rowi < t
    off = SUBLANES - t - (CONV_W - 1)
    seqs = range(nb)
    items = [(b, h) for b in seqs for h in HEADS]
    src = [src_ref[b] for b in seqs]
    conv = [sum(pltpu.roll(x, (SUBLANES - off - j) % SUBLANES, axis=0) * cw_ref[j:j + 1, :]
                for j in range(CONV_W)) for x in src]
    cs = [x * _sigmoid(x) for x in conv]
    log_a = [jnp.where(real, nega_ref[...] * _softplus(ab_ref[b] + dtb_ref[...]), 0.0) for b in seqs]
    beta_all = [jnp.where(real, _sigmoid(ab_ref[b]), 0.0) for b in seqs]

    def part(b, h, which):
        lo = which * WIDTH_A + h * HEAD_DIM_A
        return cs[b][:, lo:lo + HEAD_DIM_A]

    def l2n(x, scale):
        return x * (lax.rsqrt(jnp.sum(x * x, axis=-1, keepdims=True) + EPS) * scale)

    q = [l2n(part(b, h, 0), HEAD_DIM_A ** -0.5) for b, h in items]
    k = [l2n(part(b, h, 1), 1.0) for b, h in items]
    v = [part(b, h, 2) for b, h in items]
    beta = [beta_all[b][:, N_HEADS_A + h:N_HEADS_A + h + 1] for b, h in items]
    gb = [jnp.broadcast_to(log_a[b][:, h:h + 1], (SUBLANES, LANES)) for b, h in items]
    cum = [sum(jnp.where(rowi >= j, x[j:j + 1, :], 0.0) for j in range(t)) for x in gb]
    g_last = [x[t - 1:t, :] for x in cum]
    e_g = [jnp.exp(x) for x in cum]
    kb = [x * y for x, y in zip(k, beta)]
    w = [x * y for x, y in zip(kb, e_g)]
    u = [x * y for x, y in zip(v, beta)]
    dec = [[jnp.exp(jnp.where(rowi >= j, x - x[j:j + 1, :], NEG)) for j in range(t)] for x in cum]
    qk_cols = [[jnp.sum(q[n] * k[n][j:j + 1, :], axis=-1, keepdims=True) * dec[n][j]
                for j in range(t)] for n in range(len(items))]
    a_cols = [[jnp.where(rowi > j, jnp.sum(kb[n] * k[n][j:j + 1, :], axis=-1, keepdims=True)
                         * dec[n][j], 0.0) for j in range(t - 1)] for n in range(len(items))]
    for j in range(t - 1):
        w = [x - a[j] * x[j:j + 1, :] for x, a in zip(w, a_cols)]
        u = [x - a[j] * x[j:j + 1, :] for x, a in zip(u, a_cols)]
    qg = [x * y for x, y in zip(q, e_g)]
    kd = [x * jnp.exp(gl - cm) for x, gl, cm in zip(k, g_last, cum)]
    s_old = [s0_ref[0, b, h] for b, h in items]
    s_sp = [_split(x) for x in s_old]
    v_new = [uu - _mm3s(_split(ww), ss) for uu, ww, ss in zip(u, w, s_sp)]
    o = [_mm3s(_split(x), ss) for x, ss in zip(qg, s_sp)]
    o = [x + sum(c[j] * vn[j:j + 1, :] for j in range(t)) for x, c, vn in zip(o, qk_cols, v_new)]
    s_new = [so * jnp.exp(gl) + _mm3(x, vn, TN) for so, gl, x, vn in zip(s_old, g_last, kd, v_new)]
    for n, (b, h) in enumerate(items):
        lo = h * HEAD_DIM_A
        s_out_ref[0, b, h] = s_new[n]
        zz = z_ref[b, :, lo:lo + HEAD_DIM_A]
        o_ref[b, :, lo:lo + HEAD_DIM_A] = _rms(o[n], gn_ref[...]) * (zz * _sigmoid(zz))


def _delta_sample(src, ab, z, cw, nega, dtb, gn, s_all, s_prev, layer, t, nb=4):
    b = src.shape[0]
    aliased = s_prev is not None
    state_spec = pl.BlockSpec((1, nb, N_HEADS_A, HEAD_DIM_A, HEAD_DIM_A),
                              lambda i: (layer, i, 0, 0, 0))
    in_specs = [pl.BlockSpec((nb, SUBLANES, 3 * WIDTH_A), lambda i: (i, 0, 0)),
                pl.BlockSpec((nb, SUBLANES, LANES), lambda i: (i, 0, 0)),
                pl.BlockSpec((nb, SUBLANES, WIDTH_A), lambda i: (i, 0, 0)),
                pl.BlockSpec((CONV_W, 3 * WIDTH_A), lambda i: (0, 0)),
                pl.BlockSpec((1, LANES), lambda i: (0, 0)),
                pl.BlockSpec((1, LANES), lambda i: (0, 0)),
                pl.BlockSpec((1, HEAD_DIM_A), lambda i: (0, 0)),
                state_spec]
    args = [src, ab, z, cw, nega, dtb, gn, s_all]
    if aliased:
        in_specs.append(pl.BlockSpec(memory_space=pl.ANY))
        args.append(s_prev)
    return pl.pallas_call(
        functools.partial(_delta_sample_body, nb=nb, t=t, aliased=aliased),
        grid=(b // nb,),
        in_specs=in_specs,
        out_specs=[pl.BlockSpec((nb, SUBLANES, WIDTH_A), lambda i: (i, 0, 0)), state_spec],
        out_shape=[jax.ShapeDtypeStruct((b, SUBLANES, WIDTH_A), f32),
                   jax.ShapeDtypeStruct(s_all.shape, f32)],
        input_output_aliases={len(args) - 1: 1} if aliased else {},
        compiler_params=pltpu.CompilerParams(
            dimension_semantics=("arbitrary",), vmem_limit_bytes=VMEM_LIMIT),
        name="delta_sample",
    )(*args)


ATT_TB = 2048
Q_CHUNKS = WIDTH_G // LANES
KV_CHUNKS = 2 * WIDTH_G // LANES


def _attn_prompt_body(q_ref, kvh_ref, kvc_ref, o_ref, l_ref, q_scr, kv_scr, o_scr, l_scr, *,
                      dil, slopes, tb):
    i = pl.program_id(1)
    halo = N_OFF * dil
    for c in range(Q_CHUNKS):
        q_scr[c] = q_ref[0, :, c * LANES:(c + 1) * LANES]
    for c in range(KV_CHUNKS):
        kv_scr[c, pl.ds(0, halo), :] = kvh_ref[0, :, c * LANES:(c + 1) * LANES]
        kv_scr[c, pl.ds(halo, tb), :] = kvc_ref[0, :, c * LANES:(c + 1) * LANES]
    a = lax.broadcasted_iota(jnp.int32, (N_OFF, 2 * N_OFF), 0)
    cc = lax.broadcasted_iota(jnp.int32, (N_OFF, 2 * N_OFF), 1)
    delta = a - cc + N_OFF
    in_win = (delta >= 0) & (delta <= N_OFF)
    dist = delta.astype(f32) * float(dil)
    upper = lax.broadcasted_iota(jnp.int32, (N_OFF, LANES), 1) >= HEAD_DIM_B
    n_sub = tb // halo

    def sub_block(idx, carry):
        r = idx // n_sub
        u = idx - r * n_sub
        base = r + halo * u
        key_tok = i * tb - halo + base + dil * cc
        mask = in_win & (key_tok >= 0)
        cs_ = range(Q_CHUNKS)
        hs = range(HEADS_PER_GROUP)
        qc = [q_scr[c, pl.ds(base, N_OFF, stride=dil), :] * HEAD_DIM_B ** -0.5 for c in cs_]
        kc = [kv_scr[c, pl.ds(base, 2 * N_OFF, stride=dil), :].astype(bf16) for c in cs_]
        vc = [kv_scr[Q_CHUNKS + c, pl.ds(base, 2 * N_OFF, stride=dil), :].astype(bf16) for c in cs_]
        qm = [jnp.where(upper if h % 2 else jnp.logical_not(upper), qc[h // 2], 0.0).astype(bf16)
              for h in hs]
        s = [jnp.where(mask, _dot(qm[h], kc[h // 2], NT) - slopes[h] * dist, NEG) for h in hs]
        m = [jnp.max(s[h], axis=-1, keepdims=True) for h in hs]
        p = [jnp.exp(s[h] - m[h]) for h in hs]
        den = [jnp.sum(p[h], axis=-1, keepdims=True) for h in hs]
        pv = [_dot(p[h].astype(bf16), vc[h // 2]) / den[h] for h in hs]
        lse = [jnp.broadcast_to(m[h] + jnp.log(den[h]), (N_OFF, LANES)) for h in hs]
        for c in cs_:
            o_scr[c, pl.ds(base, N_OFF, stride=dil), :] = jnp.where(upper, pv[2 * c + 1], pv[2 * c])
            l_scr[c, pl.ds(base, N_OFF, stride=dil), :] = jnp.where(upper, lse[2 * c + 1], lse[2 * c])
        return carry

    lax.fori_loop(0, tb // N_OFF, sub_block, 0, unroll=2)
    for c in range(Q_CHUNKS):
        o_ref[0, :, c * LANES:(c + 1) * LANES] = o_scr[c]
        l_ref[0, :, c * LANES:(c + 1) * LANES] = l_scr[c]


def _attn_prompt(q, kv, g):
    b, t, _ = q.shape
    dil = DILATIONS[g]
    halo = N_OFF * dil
    tb = min(ATT_TB, t)
    assert tb % halo == 0 and t % tb == 0
    per = tb // halo
    return pl.pallas_call(
        functools.partial(_attn_prompt_body, dil=dil, slopes=_slopes(g), tb=tb),
        grid=(b, t // tb),
        in_specs=[pl.BlockSpec((1, tb, WIDTH_G), lambda bi, i: (bi, i, 0)),
                  pl.BlockSpec((1, halo, 2 * WIDTH_G),
                               lambda bi, i: (bi, jnp.maximum(i * per - 1, 0), 0)),
                  pl.BlockSpec((1, tb, 2 * WIDTH_G), lambda bi, i: (bi, i, 0))],
        out_specs=[pl.BlockSpec((1, tb, WIDTH_G), lambda bi, i: (bi, i, 0)),
                   pl.BlockSpec((1, tb, WIDTH_G), lambda bi, i: (bi, i, 0))],
        out_shape=[jax.ShapeDtypeStruct((b, t, WIDTH_G), f32),
                   jax.ShapeDtypeStruct((b, t, WIDTH_G), f32)],
        scratch_shapes=[pltpu.VMEM((Q_CHUNKS, tb, LANES), f32),
                        pltpu.VMEM((KV_CHUNKS, halo + tb, LANES), f32),
                        pltpu.VMEM((Q_CHUNKS, tb, LANES), f32),
                        pltpu.VMEM((Q_CHUNKS, tb, LANES), f32)],
        compiler_params=pltpu.CompilerParams(
            dimension_semantics=("arbitrary", "arbitrary"), vmem_limit_bytes=VMEM_LIMIT),
        name=f"attn_prompt_g{g}",
    )(q, kv, kv)


def _sample_consts(dil, t, win):
    rr = lax.broadcasted_iota(jnp.int32, (SUBLANES, win), 1)
    qi = lax.broadcasted_iota(jnp.int32, (SUBLANES, win), 0)
    nn = lax.broadcasted_iota(jnp.int32, (SUBLANES, SUBLANES), 1)
    qn = lax.broadcasted_iota(jnp.int32, (SUBLANES, SUBLANES), 0)
    lane = lax.broadcasted_iota(jnp.int32, (HEAD_DIM_B, LANES), 1)
    e_rows = lax.broadcasted_iota(jnp.int32, (SUBLANES, LANES), 0)
    e_lane = lax.broadcasted_iota(jnp.int32, (SUBLANES, LANES), 1)
    return dict(
        valid=((rr & (dil - 1)) == (qi & (dil - 1))) & (rr >= qi) & (qi < t),
        dist=(win + qi - rr).astype(f32),
        valid_n=(nn <= qn) & (((qn - nn) & (dil - 1)) == 0) & (qn < t),
        dist_n=(qn - nn).astype(f32),
        tail_lanes=lane >= LANES - t,
        place=jnp.where((e_lane == e_rows + (LANES - t)) & (e_rows < t), 1.0, 0.0).astype(bf16))


def _sample_unit(c_in, c_out, q_b, kn_b, vn_b, kvn_b, o_b, l_b, cst, slopes, t, win):
    k3 = _split3(kvn_b[...])
    place = cst["place"]
    new_t = _dot(k3[0], place, TN) + _dot(k3[1], place, TN) + _dot(k3[2], place, TN)
    hs = range(HEADS_PER_GROUP)
    k_t = [c_in[0, h] for h in hs]
    v_t = [c_in[1, h] for h in hs]
    q8 = [(q_b[h] * HEAD_DIM_B ** -0.5).astype(bf16) for h in hs]
    s = [jnp.where(cst["valid"], _dot(q8[h], k_t[h].astype(bf16)) - slopes[h] * cst["dist"], NEG)
         for h in hs]
    sn = [jnp.where(cst["valid_n"],
                    _dot(q8[h], kn_b[h].astype(bf16), NT) - slopes[h] * cst["dist_n"], NEG)
          for h in hs]
    m = [jnp.maximum(jnp.max(s[h], axis=-1, keepdims=True),
                     jnp.max(sn[h], axis=-1, keepdims=True)) for h in hs]
    p = [jnp.exp(s[h] - m[h]) for h in hs]
    pn = [jnp.exp(sn[h] - m[h]) for h in hs]
    den = [jnp.sum(p[h], axis=-1, keepdims=True) + jnp.sum(pn[h], axis=-1, keepdims=True)
           for h in hs]
    acc = [_dot(p[h].astype(bf16), v_t[h].astype(bf16), NT) + _mm1(pn[h], vn_b[h]) for h in hs]
    for h in hs:
        o_b[:, h * HEAD_DIM_B:(h + 1) * HEAD_DIM_B] = acc[h] / den[h]
        l_b[:, h * HEAD_DIM_B:(h + 1) * HEAD_DIM_B] = jnp.broadcast_to(
            m[h] + jnp.log(den[h]), (SUBLANES, HEAD_DIM_B))
        for kv, x in ((0, k_t[h]), (1, v_t[h])):
            rolled = pltpu.roll(x, win - t, axis=1)
            fresh = new_t[(kv * HEADS_PER_GROUP + h) * HEAD_DIM_B:
                          (kv * HEADS_PER_GROUP + h + 1) * HEAD_DIM_B, :]
            if win > LANES:
                c_out[kv, h, :, pl.ds(0, win - LANES)] = rolled[:, :win - LANES]
            c_out[kv, h, :, pl.ds(win - LANES, LANES)] = jnp.where(
                cst["tail_lanes"], fresh, rolled[:, win - LANES:])


def _attn_sample_body(*refs, dil, slopes, nb, t, win):
    q_ref, kn_ref, vn_ref, kvn_ref, c_ref = refs[:5]
    o_ref, l_ref, cn_ref = refs[-3:]
    cst = _sample_consts(dil, t, win)

    def one_sequence(b, carry):
        _sample_unit(c_ref.at[0, b], cn_ref.at[0, b], q_ref.at[b], kn_ref.at[b], vn_ref.at[b],
                     kvn_ref.at[b], o_ref.at[b], l_ref.at[b], cst, slopes, t, win)
        return carry

    lax.fori_loop(0, nb, one_sequence, 0, unroll=2 if nb % 2 == 0 else 1)


def _sample_operands(q, kvn, bsz, t):
    pad = SUBLANES - t
    heads = lambda a_: jnp.pad(a_.reshape(bsz, t, HEADS_PER_GROUP, HEAD_DIM_B).transpose(0, 2, 1, 3),
                               ((0, 0), (0, 0), (0, pad), (0, 0)))
    kvn8 = jnp.pad(kvn.reshape(bsz, t, 2 * WIDTH_G), ((0, 0), (0, pad), (0, 0)))
    return heads(q), heads(kvn[:, :WIDTH_G]), heads(kvn[:, WIDTH_G:]), kvn8


def _attn_sample(q, kvn, cache_t, prev, layer, g, t):
    bsz, win = cache_t.shape[1], cache_t.shape[-1]
    dil = DILATIONS[g]
    nb = max(1, min(16, 2048 // win))
    q4, kn4, vn4, kvn8 = _sample_operands(q, kvn, bsz, t)
    aliased = prev is not None
    head_spec = pl.BlockSpec((nb, HEADS_PER_GROUP, SUBLANES, HEAD_DIM_B), lambda i: (i, 0, 0, 0))
    cache_spec = pl.BlockSpec((1, nb, 2, HEADS_PER_GROUP, HEAD_DIM_B, win),
                              lambda i: (layer, i, 0, 0, 0, 0))
    out_spec = pl.BlockSpec((nb, SUBLANES, WIDTH_G), lambda i: (i, 0, 0))
    in_specs = [head_spec, head_spec, head_spec,
                pl.BlockSpec((nb, SUBLANES, 2 * WIDTH_G), lambda i: (i, 0, 0)), cache_spec]
    args = [q4, kn4, vn4, kvn8, cache_t]
    if aliased:
        in_specs.append(pl.BlockSpec(memory_space=pl.ANY))
        args.append(prev)
    o, l, cn = pl.pallas_call(
        functools.partial(_attn_sample_body, dil=dil, slopes=_slopes(g), nb=nb, t=t, win=win),
        grid=(bsz // nb,),
        in_specs=in_specs,
        out_specs=[out_spec, out_spec, cache_spec],
        out_shape=[jax.ShapeDtypeStruct((bsz, SUBLANES, WIDTH_G), f32),
                   jax.ShapeDtypeStruct((bsz, SUBLANES, WIDTH_G), f32),
                   jax.ShapeDtypeStruct(cache_t.shape, f32)],
        input_output_aliases={len(args) - 1: 2} if aliased else {},
        compiler_params=pltpu.CompilerParams(
            dimension_semantics=("arbitrary",), vmem_limit_bytes=VMEM_LIMIT),
        name=f"attn_sample_g{g}",
    )(*args)
    return (o[:, :t].reshape(bsz * t, WIDTH_G), l[:, :t].reshape(bsz * t, WIDTH_G), cn)


FF_CHUNK = 512


def _tail_front(x_ref, oa_ref, o_refs, l_refs, gate_ref, wa_ref, wb_ref, wo_ref, n2_ref):
    l0, l1, l2 = (r[...] for r in l_refs)
    lm = jnp.maximum(jnp.maximum(l0, l1), l2)
    e0, e1, e2 = jnp.exp(l0 - lm), jnp.exp(l1 - lm), jnp.exp(l2 - lm)
    den = e0 + e1 + e2
    ob = (e0 / den) * o_refs[0][...] + (e1 / den) * o_refs[1][...] + (e2 / den) * o_refs[2][...]
    ya = _mm1(oa_ref[...], wa_ref[...])
    yb = _mm1(ob, wb_ref[...])
    merged = _sigmoid(gate_ref[:, :D_MODEL]) * ya + _sigmoid(gate_ref[:, D_MODEL:]) * yb
    x1 = x_ref[...] + _mm1(merged, wo_ref[...])
    return x1, _rms(x1, n2_ref[...]).astype(bf16)


def _tail_ff(acc, h2, wu_ref, wd_ref, chunks):
    for j in chunks:
        up = _dot(h2, wu_ref[:, j * FF_CHUNK:(j + 1) * FF_CHUNK])
        act = jnp.square(jnp.maximum(up, 0.0)).astype(bf16)
        acc = acc + _dot(act, wd_ref[j * FF_CHUNK:(j + 1) * FF_CHUNK, :])
    return acc


def _tail_body(x_ref, oa_ref, o0_ref, o1_ref, o2_ref, l0_ref, l1_ref, l2_ref, gate_ref,
               wa_ref, wb_ref, wo_ref, n2_ref, wu_ref, wd_ref, fn_ref, y_ref, *, final):
    x1, h2 = _tail_front(x_ref, oa_ref, (o0_ref, o1_ref, o2_ref), (l0_ref, l1_ref, l2_ref),
                         gate_ref, wa_ref, wb_ref, wo_ref, n2_ref)
    acc = _tail_ff(x1, h2, wu_ref, wd_ref, range(D_FF // FF_CHUNK))
    if final:
        acc = _rms(acc, fn_ref[...])
    y_ref[...] = acc


def _tail_specs(tm):
    tok = lambda wd_: pl.BlockSpec((tm, wd_), lambda i: (i, 0))
    const = lambda shape: pl.BlockSpec(shape, lambda i: (0, 0), pipeline_mode=pl.Buffered(1))
    in_specs = [tok(D_MODEL), tok(WIDTH_A)] + [tok(WIDTH_G)] * 6 + [
        tok(2 * D_MODEL), const((WIDTH_A, D_MODEL)), const((WIDTH_G, D_MODEL)),
        const((D_MODEL, D_MODEL)), const((1, D_MODEL)), const((D_MODEL, D_FF)),
        const((D_FF, D_MODEL)), const((1, D_MODEL))]
    return in_specs, tok(D_MODEL)


def _tail(x, oa, outs, lses, gate, wa, wb, wo, n2, wu, wd, fn, final, tm=512):
    m = x.shape[0]
    tm = min(tm, m)
    assert m % tm == 0
    in_specs, out_spec = _tail_specs(tm)
    return pl.pallas_call(
        functools.partial(_tail_body, final=final),
        grid=(m // tm,),
        in_specs=in_specs,
        out_specs=out_spec,
        out_shape=jax.ShapeDtypeStruct((m, D_MODEL), f32),
        compiler_params=pltpu.CompilerParams(
            dimension_semantics=("arbitrary",), vmem_limit_bytes=VMEM_LIMIT),
        name="tail",
    )(x, oa, *outs, *lses, gate, wa, wb, wo, n2, wu, wd, fn)


FUSED_UNITS = 2
N_TAIL_IN = 16


def _tail_cache_body(*refs, final, dil, slopes, t, win, layer, aliased):
    (x_ref, oa_ref, o0_ref, o1_ref, o2_ref, l0_ref, l1_ref, l2_ref, gate_ref,
     wa_ref, wb_ref, wo_ref, n2_ref, wu_ref, wd_ref, fn_ref) = refs[:N_TAIL_IN]
    q_ref, kn_ref, vn_ref, kvn_ref, c_hbm = refs[N_TAIL_IN:N_TAIL_IN + 5]
    n_in = N_TAIL_IN + (6 if aliased else 5)
    y_ref, os_ref, ls_ref, cn_hbm = refs[n_in:n_in + 4]
    in_buf, out_buf, in_sem, out_sem = refs[n_in + 4:]
    i = pl.program_id(0)
    n = pl.num_programs(0)
    cst = _sample_consts(dil, t, win)

    def fetch(u, slot):
        return pltpu.make_async_copy(c_hbm.at[layer, u], in_buf.at[slot], in_sem.at[slot])

    def flush(u, slot):
        return pltpu.make_async_copy(out_buf.at[slot], cn_hbm.at[layer, u], out_sem.at[slot])

    @pl.when(i == 0)
    def _():
        fetch(0, 0).start()

    n_ff = D_FF // FF_CHUNK
    first_part = (3 * n_ff) // 8
    acc = h2 = None
    for j in range(FUSED_UNITS):
        u = i * FUSED_UNITS + j
        fetch(u, j).wait()
        if j + 1 < FUSED_UNITS:
            fetch(u + 1, j + 1).start()
        else:
            @pl.when(i + 1 < n)
            def _():
                fetch(u + 1, 0).start()

        @pl.when(i >= 1)
        def _():
            flush(u - FUSED_UNITS, j).wait()

        if j == 0:
            x1, h2 = _tail_front(x_ref, oa_ref, (o0_ref, o1_ref, o2_ref),
                                 (l0_ref, l1_ref, l2_ref), gate_ref, wa_ref, wb_ref, wo_ref, n2_ref)
            acc = _tail_ff(x1, h2, wu_ref, wd_ref, range(first_part))
        else:
            acc = _tail_ff(acc, h2, wu_ref, wd_ref, range(first_part, n_ff))
            if final:
                acc = _rms(acc, fn_ref[...])
            y_ref[...] = acc
        _sample_unit(in_buf.at[j], out_buf.at[j], q_ref.at[j], kn_ref.at[j], vn_ref.at[j],
                     kvn_ref.at[j], os_ref.at[j], ls_ref.at[j], cst, slopes, t, win)
        flush(u, j).start()

    @pl.when(i == n - 1)
    def _():
        for j in range(FUSED_UNITS):
            flush(i * FUSED_UNITS + j, j).wait()


def _tail_with_cache(x, oa, outs, lses, gate, wa, wb, wo, n2, wu, wd, fn, final,
                     q, kvn, cache_t, prev, layer, g, t):
    m = x.shape[0]
    bsz, win = cache_t.shape[1], cache_t.shape[-1]
    assert FUSED_UNITS == 2 and bsz % FUSED_UNITS == 0 and m % (bsz // FUSED_UNITS) == 0
    steps = bsz // FUSED_UNITS
    tm = m // steps
    assert tm % SUBLANES == 0
    in_specs, y_spec = _tail_specs(tm)
    q4, kn4, vn4, kvn8 = _sample_operands(q, kvn, bsz, t)
    head_spec = pl.BlockSpec((FUSED_UNITS, HEADS_PER_GROUP, SUBLANES, HEAD_DIM_B),
                             lambda i: (i, 0, 0, 0))
    o_spec = pl.BlockSpec((FUSED_UNITS, SUBLANES, WIDTH_G), lambda i: (i, 0, 0))
    hbm = pl.BlockSpec(memory_space=pl.ANY)
    in_specs = in_specs + [head_spec, head_spec, head_spec,
                           pl.BlockSpec((FUSED_UNITS, SUBLANES, 2 * WIDTH_G), lambda i: (i, 0, 0)), hbm]
    args = [x, oa, *outs, *lses, gate, wa, wb, wo, n2, wu, wd, fn, q4, kn4, vn4, kvn8, cache_t]
    aliased = prev is not None
    if aliased:
        in_specs.append(hbm)
        args.append(prev)
    slab = (FUSED_UNITS, 2, HEADS_PER_GROUP, HEAD_DIM_B, win)
    y, o, l, cn = pl.pallas_call(
        functools.partial(_tail_cache_body, final=final, dil=DILATIONS[g], slopes=_slopes(g), t=t,
                          win=win, layer=layer, aliased=aliased),
        grid=(steps,),
        in_specs=in_specs,
        out_specs=[y_spec, o_spec, o_spec, hbm],
        out_shape=[jax.ShapeDtypeStruct((m, D_MODEL), f32),
                   jax.ShapeDtypeStruct((bsz, SUBLANES, WIDTH_G), f32),
                   jax.ShapeDtypeStruct((bsz, SUBLANES, WIDTH_G), f32),
                   jax.ShapeDtypeStruct(cache_t.shape, f32)],
        scratch_shapes=[pltpu.VMEM(slab, f32), pltpu.VMEM(slab, f32),
                        pltpu.SemaphoreType.DMA((FUSED_UNITS,)),
                        pltpu.SemaphoreType.DMA((FUSED_UNITS,))],
        input_output_aliases={len(args) - 1: 3} if aliased else {},
        compiler_params=pltpu.CompilerParams(
            dimension_semantics=("arbitrary",), vmem_limit_bytes=VMEM_LIMIT),
        name="tail_cache",
    )(*args)
    return y, o[:, :t].reshape(bsz * t, WIDTH_G), l[:, :t].reshape(bsz * t, WIDTH_G), cn


def _lane_row(vals):
    return jnp.zeros((1, LANES), f32).at[0, :vals.shape[0]].set(vals.astype(f32))


def _layer_params(l, norm1, w_in, conv_w, a_log, dt_bias, gnorm_a, w_br_a, w_br_b, w_out, norm2,
                  w_up, w_down):
    return dict(
        n1=norm1[l].reshape(1, D_MODEL), w_in=_permute_w_in(w_in[l]), cw=conv_w[l],
        nega=_lane_row(-jnp.exp(a_log[l].astype(f32))), dtb=_lane_row(dt_bias[l]),
        gn=gnorm_a[l].reshape(1, HEAD_DIM_A).astype(f32),
        wa=w_br_a[l].astype(bf16), wb=w_br_b[l].astype(bf16), wo=w_out[l].astype(bf16),
        n2=norm2[l].reshape(1, D_MODEL), wu=w_up[l].astype(bf16), wd=w_down[l].astype(bf16))


def _layer(xp, xs, p, fn, final, layer, caches_t, prev_caches, s_all, s_prev, conv_buf):
    bp, tp, _ = xp.shape
    xpf = xp.reshape(bp * tp, D_MODEL)
    cs, z, q0, q1, q2, kv0, kv1, kv2, gate_p, ab, last = _inproj(xpf, p["n1"], p["w_in"],
                                                                 conv_w=p["cw"], seq_len=tp)
    s0 = jnp.zeros((bp, N_HEADS_A, HEAD_DIM_A, HEAD_DIM_A), f32)
    oa_p, s_new_p = _delta_prompt(cs.reshape(bp, tp, 3 * WIDTH_A), ab.reshape(bp, tp, LANES),
                                  z.reshape(bp, tp, WIDTH_A), p["nega"], p["dtb"], p["gn"], s0)
    outs_p, lses_p, wins_p = [], [], []
    for g, (qg, kvg) in enumerate(((q0, kv0), (q1, kv1), (q2, kv2))):
        kv3 = kvg.reshape(bp, tp, 2 * WIDTH_G)
        o, l = _attn_prompt(qg.reshape(bp, tp, WIDTH_G), kv3, g)
        outs_p.append(o.reshape(bp * tp, WIDTH_G))
        lses_p.append(l.reshape(bp * tp, WIDTH_G))
        keep = min(WINDOWS[g], tp)
        wins_p.append(kv3[:, tp - keep:].reshape(bp, keep, 2, HEADS_PER_GROUP, HEAD_DIM_B))
    conv_p = last[:, SUBLANES - (CONV_W - 1):]

    bs, ts, _ = xs.shape
    xsf = xs.reshape(bs * ts, D_MODEL)
    qkv, z, q0, q1, q2, kv0, kv1, kv2, gate_s, ab = _inproj(xsf, p["n1"], p["w_in"])
    qkv3 = qkv.reshape(bs, ts, 3 * WIDTH_A)
    pad = SUBLANES - ts
    src = jnp.concatenate([jnp.zeros((bs, SUBLANES - ts - (CONV_W - 1), 3 * WIDTH_A), f32),
                           conv_buf.astype(f32), qkv3], axis=1)
    pad_rows = lambda a_: jnp.pad(a_, ((0, 0), (0, pad), (0, 0)))
    oa_s, s_stack = _delta_sample(src, pad_rows(ab.reshape(bs, ts, LANES)),
                                  pad_rows(z.reshape(bs, ts, WIDTH_A)), p["cw"], p["nega"],
                                  p["dtb"], p["gn"], s_all, s_prev, layer, ts)
    oa_s = oa_s[:, :ts].reshape(bs * ts, WIDTH_A)
    outs_s, lses_s, new_caches = [], [], []
    for g, (qg, kvg) in enumerate(((q0, kv0), (q1, kv1))):
        o, l, cn = _attn_sample(qg, kvg, caches_t[g], prev_caches[g], layer, g, ts)
        outs_s.append(o)
        lses_s.append(l)
        new_caches.append(cn)

    g = N_GROUPS - 1
    yp, o, l, cn = _tail_with_cache(xpf, oa_p.reshape(bp * tp, WIDTH_A), outs_p, lses_p, gate_p,
                                    p["wa"], p["wb"], p["wo"], p["n2"], p["wu"], p["wd"], fn, final,
                                    q2, kv2, caches_t[g], prev_caches[g], layer, g, ts)
    outs_s.append(o)
    lses_s.append(l)
    new_caches.append(cn)
    ys = _tail(xsf, oa_s, outs_s, lses_s, gate_s, p["wa"], p["wb"], p["wo"], p["n2"], p["wu"],
               p["wd"], fn, final)
    conv_s = qkv3[:, ts - (CONV_W - 1):]
    return (yp.reshape(bp, tp, D_MODEL), ys.reshape(bs, ts, D_MODEL), wins_p, s_new_p, conv_p,
            new_caches, s_stack, conv_s)


def kernel(x_prompt, x_sample, cache_win0, cache_win1, cache_win2, state_delta, state_conv, norm1,
           w_in, conv_w, a_log, dt_bias, gnorm_a, w_br_a, w_br_b, w_out, norm2, w_up, w_down,
           final_norm):
    depth = w_in.shape[0]
    assert x_sample.shape[1] + CONV_W - 1 <= SUBLANES
    fn = final_norm.reshape(1, D_MODEL)
    caches_t = [c.transpose(0, 1, 3, 4, 5, 2) for c in (cache_win0, cache_win1, cache_win2)]
    new_caches = [None] * N_GROUPS
    s_stack = None
    xp, xs = x_prompt, x_sample
    wins_p = [[] for _ in range(N_GROUPS)]
    delta_p, conv_p, conv_s = [], [], []
    for l in range(depth):
        p = _layer_params(l, norm1, w_in, conv_w, a_log, dt_bias, gnorm_a, w_br_a, w_br_b, w_out,
                          norm2, w_up, w_down)
        xp, xs, wp, sp, cp, new_caches, s_stack, cs = _layer(
            xp, xs, p, fn, l == depth - 1, l, caches_t, new_caches, state_delta, s_stack,
            state_conv[l])
        for g in range(N_GROUPS):
            wins_p[g].append(wp[g])
        delta_p.append(sp)
        conv_p.append(cp)
        conv_s.append(cs)
    st = jnp.stack
    wins_s = [c.transpose(0, 1, 5, 2, 3, 4) for c in new_caches]
    return (xp, xs, st(wins_p[0]), st(wins_p[1]), st(wins_p[2]), st(delta_p), st(conv_p),
            wins_s[0], wins_s[1], wins_s[2], s_stack, st(conv_s))
```

```python
import functools

import jax
import jax.numpy as jnp
from jax import lax
from jax.experimental import pallas as pl
from jax.experimental.pallas import tpu as pltpu

f32 = jnp.float32
bf16 = jnp.bfloat16

D_MODEL = 1024
N_HEADS_A = 4
HEAD_DIM_A = 128
WIDTH_A = N_HEADS_A * HEAD_DIM_A
CONV_W = 4
WINDOWS = (128, 512, 2048)
DILATIONS = (1, 4, 16)
N_GROUPS = 3
HEADS_PER_GROUP = 4
HEAD_DIM_B = 64
N_HEADS_B = N_GROUPS * HEADS_PER_GROUP
WIDTH_G = HEADS_PER_GROUP * HEAD_DIM_B
D_FF = 4 * D_MODEL
EPS = 1e-6
N_OFF = 128
NEG = -1e30

SUBLANES = 8
LANES = 128
VMEM_LIMIT = 56 * 1024 * 1024

SEG_WIDTHS = (3 * WIDTH_A, WIDTH_A, WIDTH_G, WIDTH_G, WIDTH_G,
              2 * WIDTH_G, 2 * WIDTH_G, 2 * WIDTH_G, 2 * D_MODEL, LANES)
N_PERM = sum(SEG_WIDTHS)
SEG_DTYPES = (f32,) * 8 + (bf16, f32)

NN = (((1,), (0,)), ((), ()))
NT = (((1,), (1,)), ((), ()))
TN = (((0,), (0,)), ((), ()))


def _slopes(g):
    return tuple(2.0 ** (-8.0 * (g * HEADS_PER_GROUP + h + 1) / N_HEADS_B)
                 for h in range(HEADS_PER_GROUP))


def _dot(a, b, dims=NN):
    return lax.dot_general(a, b, dims, preferred_element_type=f32)


def _mm1(a, b, dims=NN):
    return _dot(a.astype(bf16), b.astype(bf16), dims)


def _split(a):
    hi = a.astype(bf16)
    lo = (a - hi.astype(f32)).astype(bf16)
    return hi, lo


def _split3(a):
    hi = a.astype(bf16)
    r = a - hi.astype(f32)
    mid = r.astype(bf16)
    lo = (r - mid.astype(f32)).astype(bf16)
    return hi, mid, lo


def _mm3s(a, b, dims=NN):
    return _dot(a[0], b[0], dims) + _dot(a[0], b[1], dims) + _dot(a[1], b[0], dims)


def _mm3(a, b, dims=NN):
    return _mm3s(_split(a), _split(b), dims)


def _sigmoid(x):
    return 1.0 / (1.0 + jnp.exp(-x))


def _softplus(x):
    return jnp.maximum(x, 0.0) + jnp.log1p(jnp.exp(-jnp.abs(x)))


def _rms(x, g):
    return x * lax.rsqrt(jnp.mean(x * x, axis=-1, keepdims=True) + EPS) * g


def _inproj_body(x_ref, g_ref, w_ref, *out_refs):
    h = _rms(x_ref[...], g_ref[...]).astype(bf16)
    off = 0
    for o_ref, width in zip(out_refs, SEG_WIDTHS):
        o_ref[...] = _dot(h, w_ref[:, off:off + width]).astype(o_ref.dtype)
        off += width


def _inproj_conv_body(x_ref, g_ref, w_ref, cw_ref, *refs, tm, tiles_per_seq):
    out_refs, last_ref, xs_ref = refs[:len(SEG_WIDTHS)], refs[-2], refs[-1]
    i = pl.program_id(0)

    @pl.when(lax.rem(i, tiles_per_seq) == 0)
    def _():
        xs_ref[pl.ds(0, SUBLANES), :] = jnp.zeros((SUBLANES, SEG_WIDTHS[0]), f32)

    h = _rms(x_ref[...], g_ref[...]).astype(bf16)
    xs_ref[pl.ds(SUBLANES, tm), :] = _dot(h, w_ref[:, :SEG_WIDTHS[0]])
    conv = xs_ref[pl.ds(SUBLANES - CONV_W + 1, tm), :] * cw_ref[0:1, :]
    for j in range(1, CONV_W):
        conv = conv + xs_ref[pl.ds(SUBLANES - CONV_W + 1 + j, tm), :] * cw_ref[j:j + 1, :]
    last = xs_ref[pl.ds(tm, SUBLANES), :]
    xs_ref[pl.ds(0, SUBLANES), :] = last
    last_ref[0] = last
    out_refs[0][...] = conv * _sigmoid(conv)
    off = SEG_WIDTHS[0]
    for o_ref, width in zip(out_refs[1:], SEG_WIDTHS[1:]):
        o_ref[...] = _dot(h, w_ref[:, off:off + width]).astype(o_ref.dtype)
        off += width


def _inproj(x, g, w, tm=256, conv_w=None, seq_len=None):
    m = x.shape[0]
    tm = min(tm, m)
    assert m % tm == 0
    in_specs = [pl.BlockSpec((tm, D_MODEL), lambda i: (i, 0)),
                pl.BlockSpec((1, D_MODEL), lambda i: (0, 0)),
                pl.BlockSpec((D_MODEL, N_PERM), lambda i: (0, 0), pipeline_mode=pl.Buffered(1))]
    out_specs = [pl.BlockSpec((tm, wd), lambda i: (i, 0)) for wd in SEG_WIDTHS]
    out_shape = [jax.ShapeDtypeStruct((m, wd), dt) for wd, dt in zip(SEG_WIDTHS, SEG_DTYPES)]
    args = [x, g, w]
    body, scratch = _inproj_body, []
    if conv_w is not None:
        assert seq_len % tm == 0 and tm >= SUBLANES
        per = seq_len // tm
        in_specs.append(pl.BlockSpec((CONV_W, SEG_WIDTHS[0]), lambda i: (0, 0)))
        out_specs.append(pl.BlockSpec((1, SUBLANES, SEG_WIDTHS[0]), lambda i: (i // per, 0, 0)))
        out_shape.append(jax.ShapeDtypeStruct((m // seq_len, SUBLANES, SEG_WIDTHS[0]), f32))
        args.append(conv_w)
        body = functools.partial(_inproj_conv_body, tm=tm, tiles_per_seq=per)
        scratch = [pltpu.VMEM((tm + SUBLANES, SEG_WIDTHS[0]), f32)]
    return pl.pallas_call(
        body,
        grid=(m // tm,),
        in_specs=in_specs,
        out_specs=out_specs,
        out_shape=out_shape,
        scratch_shapes=scratch,
        compiler_params=pltpu.CompilerParams(
            dimension_semantics=("arbitrary",), vmem_limit_bytes=VMEM_LIMIT),
        name="inproj",
    )(*args)


def _permute_w_in(w):
    o_z = 3 * WIDTH_A
    o_ab = o_z + WIDTH_A
    o_q = o_ab + 2 * N_HEADS_A
    o_k = o_q + N_GROUPS * WIDTH_G
    o_v = o_k + N_GROUPS * WIDTH_G
    o_gate = o_v + N_GROUPS * WIDTH_G
    parts = [w[:, :o_ab], w[:, o_q:o_k]]
    for g in range(N_GROUPS):
        parts.append(w[:, o_k + g * WIDTH_G:o_k + (g + 1) * WIDTH_G])
        parts.append(w[:, o_v + g * WIDTH_G:o_v + (g + 1) * WIDTH_G])
    parts.append(w[:, o_gate:])
    parts.append(w[:, o_ab:o_q])
    parts.append(jnp.zeros((w.shape[0], LANES - 2 * N_HEADS_A), w.dtype))
    return jnp.concatenate(parts, axis=1).astype(bf16)


CHUNK = 128
INV_BASE = 16
HEADS = tuple(range(N_HEADS_A))


def _tri_inverse_heads(a_list, row, col):
    eye = jnp.where(row == col, 1.0, 0.0)
    blk = lambda idx, size: jnp.right_shift(idx, size.bit_length() - 1)
    same16 = blk(row, INV_BASE) == blk(col, INV_BASE)
    ad = [jnp.where(same16, a, 0.0) for a in a_list]
    p = [eye - x for x in ad]
    xs = [_split(x) for x in ad]
    x = [_mm3s(s, s) for s in xs]
    xs = [_split(v) for v in x]
    p = [pv + _mm3s(_split(pv), xx) for pv, xx in zip(p, xs)]
    xb = [s[0] for s in xs]
    n = 4
    while n < INV_BASE:
        xb = [_dot(v, v).astype(bf16) for v in xb]
        p = [pv + _dot(pv.astype(bf16), xx) for pv, xx in zip(p, xb)]
        n *= 2
    s = INV_BASE
    while s < CHUNK:
        pair = (blk(row, 2 * s) == blk(col, 2 * s)) & (blk(row, s) != blk(col, s))
        es = [_split(jnp.where(pair, a, 0.0)) for a in a_list]
        ps = [_split(v) for v in p]
        pe = [_split(_mm3s(pp, ee)) for pp, ee in zip(ps, es)]
        p = [pv - _mm3s(x, pp) for pv, x, pp in zip(p, pe, ps)]
        s *= 2
    return [_split(v) for v in p]


def _delta_prompt_body(cs_ref, ab_ref, z_ref, nega_ref, dtb_ref, gn_ref, s0_ref,
                       o_ref, s_out_ref, s_ref, *, nchunk):
    c = pl.program_id(1)

    @pl.when(c == 0)
    def _():
        s_ref[...] = s0_ref[0]

    items = [(ck, h) for ck in range(nchunk) for h in HEADS]
    rows = lambda ck: pl.ds(ck * CHUNK, CHUNK)
    log_a = [nega_ref[...] * _softplus(ab_ref[0, rows(ck), :] + dtb_ref[...])
             for ck in range(nchunk)]
    beta_all = [_sigmoid(ab_ref[0, rows(ck), :]) for ck in range(nchunk)]

    row = lax.broadcasted_iota(jnp.int32, (CHUNK, CHUNK), 0)
    col = lax.broadcasted_iota(jnp.int32, (CHUNK, CHUNK), 1)
    causal = row >= col
    strict = row > col
    ltri = jnp.where(causal, 1.0, 0.0).astype(bf16)

    def l2n(x, scale):
        return x * (lax.rsqrt(jnp.sum(x * x, axis=-1, keepdims=True) + EPS) * scale)

    def part(ck, h, which):
        lo = which * WIDTH_A + h * HEAD_DIM_A
        return cs_ref[0, rows(ck), lo:lo + HEAD_DIM_A]

    q = [l2n(part(ck, h, 0), HEAD_DIM_A ** -0.5) for ck, h in items]
    k = [l2n(part(ck, h, 1), 1.0) for ck, h in items]
    v = [part(ck, h, 2) for ck, h in items]
    beta = [beta_all[ck][:, N_HEADS_A + h:N_HEADS_A + h + 1] for ck, h in items]
    g3 = [_split3(jnp.broadcast_to(log_a[ck][:, h:h + 1], (CHUNK, HEAD_DIM_A))) for ck, h in items]
    cum = [_dot(ltri, t[0]) + _dot(ltri, t[1]) + _dot(ltri, t[2]) for t in g3]
    decay = [jnp.exp(jnp.where(causal, x - x.T, NEG)) for x in cum]
    e_g = [jnp.exp(x) for x in cum]
    g_last = [x[CHUNK - 1:CHUNK, :] for x in cum]
    kb = [a * b for a, b in zip(k, beta)]
    k_bf = [x.astype(bf16) for x in k]
    a_mat = [jnp.where(strict, _dot(x.astype(bf16), y, NT) * d, 0.0)
             for x, y, d in zip(kb, k_bf, decay)]
    t_inv = _tri_inverse_heads(a_mat, row, col)
    w = [_mm3s(t, _split(x * e)) for t, x, e in zip(t_inv, kb, e_g)]
    u = [_mm3s(t, _split(x * b)) for t, x, b in zip(t_inv, v, beta)]
    qk = [_dot(x.astype(bf16), y, NT) * d for x, y, d in zip(q, k_bf, decay)]
    qg = [x * e for x, e in zip(q, e_g)]
    kd = [x * jnp.exp(gl - cm) for x, gl, cm in zip(k, g_last, cum)]
    s_cur = [s_ref[h] for h in HEADS]
    for ck in range(nchunk):
        n0 = ck * N_HEADS_A
        s_sp = [_split(x) for x in s_cur]
        v_new = [u[n0 + h] - _mm3s(_split(w[n0 + h]), s_sp[h]) for h in HEADS]
        o = [_dot(qg[n0 + h].astype(bf16), s_sp[h][0]) + _mm1(qk[n0 + h], v_new[h]) for h in HEADS]
        s_cur = [s_cur[h] * jnp.exp(g_last[n0 + h]) + _mm3(kd[n0 + h], v_new[h], TN) for h in HEADS]
        for h in HEADS:
            lo = h * HEAD_DIM_A
            zz = z_ref[0, rows(ck), lo:lo + HEAD_DIM_A]
            o_ref[0, rows(ck), lo:lo + HEAD_DIM_A] = _rms(o[h], gn_ref[...]) * (zz * _sigmoid(zz))
    for h in HEADS:
        s_ref[h] = s_cur[h]

    @pl.when(c == pl.num_programs(1) - 1)
    def _():
        s_out_ref[0] = s_ref[...]


DELTA_CHUNKS_PER_STEP = 2


def _delta_prompt(cs, ab, z, nega, dtb, gn, s0):
    b, t, _ = cs.shape
    nchunk = DELTA_CHUNKS_PER_STEP if t % (DELTA_CHUNKS_PER_STEP * CHUNK) == 0 else 1
    tb = nchunk * CHUNK
    assert t % tb == 0
    return pl.pallas_call(
        functools.partial(_delta_prompt_body, nchunk=nchunk),
        grid=(b, t // tb),
        in_specs=[pl.BlockSpec((1, tb, 3 * WIDTH_A), lambda i, c: (i, c, 0)),
                  pl.BlockSpec((1, tb, LANES), lambda i, c: (i, c, 0)),
                  pl.BlockSpec((1, tb, WIDTH_A), lambda i, c: (i, c, 0)),
                  pl.BlockSpec((1, LANES), lambda i, c: (0, 0)),
                  pl.BlockSpec((1, LANES), lambda i, c: (0, 0)),
                  pl.BlockSpec((1, HEAD_DIM_A), lambda i, c: (0, 0)),
                  pl.BlockSpec((1, N_HEADS_A, HEAD_DIM_A, HEAD_DIM_A), lambda i, c: (i, 0, 0, 0))],
        out_specs=[pl.BlockSpec((1, tb, WIDTH_A), lambda i, c: (i, c, 0)),
                   pl.BlockSpec((1, N_HEADS_A, HEAD_DIM_A, HEAD_DIM_A), lambda i, c: (i, 0, 0, 0))],
        out_shape=[jax.ShapeDtypeStruct((b, t, WIDTH_A), f32),
                   jax.ShapeDtypeStruct((b, N_HEADS_A, HEAD_DIM_A, HEAD_DIM_A), f32)],
        scratch_shapes=[pltpu.VMEM((N_HEADS_A, HEAD_DIM_A, HEAD_DIM_A), f32)],
        compiler_params=pltpu.CompilerParams(
            dimension_semantics=("arbitrary", "arbitrary"), vmem_limit_bytes=VMEM_LIMIT),
        name="delta_prompt",
    )(cs, ab, z, nega, dtb, gn, s0)


def _delta_sample_body(*refs, nb, t, aliased):
    x_ref, st_ref, ab_ref, z_ref, cw_ref, nega_ref, dtb_ref, gn_ref, s0_ref = refs[:9]
    o_ref, s_out_ref = refs[-2:]
    rowi = lax.broadcasted_iota(jnp.int32, (SUBLANES, LANES), 0)
    real = rowi < t
    seqs = range(nb)
    items = [(b, h) for b in seqs for h in HEADS]
    conv = [sum((pltpu.roll(x_ref[b], CONV_W - 1 - j, axis=0)
                 + pltpu.roll(st_ref[0, b], (SUBLANES - j) % SUBLANES, axis=0)) * cw_ref[j:j + 1, :]
                for j in range(CONV_W)) for b in seqs]
    cs = [x * _sigmoid(x) for x in conv]
    log_a = [jnp.where(real, nega_ref[...] * _softplus(ab_ref[b] + dtb_ref[...]), 0.0) for b in seqs]
    beta_all = [jnp.where(real, _sigmoid(ab_ref[b]), 0.0) for b in seqs]

    def part(b, h, which):
        lo = which * WIDTH_A + h * HEAD_DIM_A
        return cs[b][:, lo:lo + HEAD_DIM_A]

    def l2n(x, scale):
        return x * (lax.rsqrt(jnp.sum(x * x, axis=-1, keepdims=True) + EPS) * scale)

    q = [l2n(part(b, h, 0), HEAD_DIM_A ** -0.5) for b, h in items]
    k = [l2n(part(b, h, 1), 1.0) for b, h in items]
    v = [part(b, h, 2) for b, h in items]
    beta = [beta_all[b][:, N_HEADS_A + h:N_HEADS_A + h + 1] for b, h in items]
    gb = [jnp.broadcast_to(log_a[b][:, h:h + 1], (SUBLANES, LANES)) for b, h in items]
    cum = [sum(jnp.where(rowi >= j, x[j:j + 1, :], 0.0) for j in range(t)) for x in gb]
    g_last = [x[t - 1:t, :] for x in cum]
    e_g = [jnp.exp(x) for x in cum]
    kb = [x * y for x, y in zip(k, beta)]
    w = [x * y for x, y in zip(kb, e_g)]
    u = [x * y for x, y in zip(v, beta)]
    dec = [[jnp.exp(jnp.where(rowi >= j, x - x[j:j + 1, :], NEG)) for j in range(t)] for x in cum]
    qk_cols = [[jnp.sum(q[n] * k[n][j:j + 1, :], axis=-1, keepdims=True) * dec[n][j]
                for j in range(t)] for n in range(len(items))]
    a_cols = [[jnp.where(rowi > j, jnp.sum(kb[n] * k[n][j:j + 1, :], axis=-1, keepdims=True)
                         * dec[n][j], 0.0) for j in range(t - 1)] for n in range(len(items))]
    for j in range(t - 1):
        w = [x - a[j] * x[j:j + 1, :] for x, a in zip(w, a_cols)]
        u = [x - a[j] * x[j:j + 1, :] for x, a in zip(u, a_cols)]
    qg = [x * y for x, y in zip(q, e_g)]
    kd = [x * jnp.exp(gl - cm) for x, gl, cm in zip(k, g_last, cum)]
    s_old = [s0_ref[0, b, h] for b, h in items]
    s_sp = [_split(x) for x in s_old]
    v_new = [uu - _mm3s(_split(ww), ss) for uu, ww, ss in zip(u, w, s_sp)]
    o = [_mm3s(_split(x), ss) for x, ss in zip(qg, s_sp)]
    o = [x + sum(c[j] * vn[j:j + 1, :] for j in range(t)) for x, c, vn in zip(o, qk_cols, v_new)]
    s_new = [so * jnp.exp(gl) + _mm3(x, vn, TN) for so, gl, x, vn in zip(s_old, g_last, kd, v_new)]
    for n, (b, h) in enumerate(items):
        lo = h * HEAD_DIM_A
        s_out_ref[0, b, h] = s_new[n]
        zz = z_ref[b, :, lo:lo + HEAD_DIM_A]
        o_ref[b, :, lo:lo + HEAD_DIM_A] = _rms(o[n], gn_ref[...]) * (zz * _sigmoid(zz))


def _delta_sample(x8, st8, ab, z, cw, nega, dtb, gn, s_all, s_prev, layer, t, nb=4):
    b = x8.shape[0]
    assert t + CONV_W - 1 <= SUBLANES
    aliased = s_prev is not None
    state_spec = pl.BlockSpec((1, nb, N_HEADS_A, HEAD_DIM_A, HEAD_DIM_A),
                              lambda i: (layer, i, 0, 0, 0))
    in_specs = [pl.BlockSpec((nb, SUBLANES, 3 * WIDTH_A), lambda i: (i, 0, 0)),
                pl.BlockSpec((1, nb, SUBLANES, 3 * WIDTH_A), lambda i: (layer, i, 0, 0)),
                pl.BlockSpec((nb, SUBLANES, LANES), lambda i: (i, 0, 0)),
                pl.BlockSpec((nb, SUBLANES, WIDTH_A), lambda i: (i, 0, 0)),
                pl.BlockSpec((CONV_W, 3 * WIDTH_A), lambda i: (0, 0)),
                pl.BlockSpec((1, LANES), lambda i: (0, 0)),
                pl.BlockSpec((1, LANES), lambda i: (0, 0)),
                pl.BlockSpec((1, HEAD_DIM_A), lambda i: (0, 0)),
                state_spec]
    args = [x8, st8, ab, z, cw, nega, dtb, gn, s_all]
    if aliased:
        in_specs.append(pl.BlockSpec(memory_space=pl.ANY))
        args.append(s_prev)
    return pl.pallas_call(
        functools.partial(_delta_sample_body, nb=nb, t=t, aliased=aliased),
        grid=(b // nb,),
        in_specs=in_specs,
        out_specs=[pl.BlockSpec((nb, SUBLANES, WIDTH_A), lambda i: (i, 0, 0)), state_spec],
        out_shape=[jax.ShapeDtypeStruct((b, SUBLANES, WIDTH_A), f32),
                   jax.ShapeDtypeStruct(s_all.shape, f32)],
        input_output_aliases={len(args) - 1: 1} if aliased else {},
        compiler_params=pltpu.CompilerParams(
            dimension_semantics=("arbitrary",), vmem_limit_bytes=VMEM_LIMIT),
        name="delta_sample",
    )(*args)


ATT_TB = 2048
Q_CHUNKS = WIDTH_G // LANES
KV_CHUNKS = 2 * WIDTH_G // LANES


def _attn_prompt_body(q_ref, kvh_ref, kvc_ref, o_ref, l_ref, q_scr, kv_scr, o_scr, l_scr, *,
                      dil, slopes, tb):
    i = pl.program_id(1)
    halo = N_OFF * dil
    for c in range(Q_CHUNKS):
        q_scr[c] = q_ref[0, :, c * LANES:(c + 1) * LANES]
    for c in range(KV_CHUNKS):
        kv_scr[c, pl.ds(0, halo), :] = kvh_ref[0, :, c * LANES:(c + 1) * LANES]
        kv_scr[c, pl.ds(halo, tb), :] = kvc_ref[0, :, c * LANES:(c + 1) * LANES]
    a = lax.broadcasted_iota(jnp.int32, (N_OFF, 2 * N_OFF), 0)
    cc = lax.broadcasted_iota(jnp.int32, (N_OFF, 2 * N_OFF), 1)
    delta = a - cc + N_OFF
    in_win = (delta >= 0) & (delta <= N_OFF)
    dist = delta.astype(f32) * float(dil)
    upper = lax.broadcasted_iota(jnp.int32, (N_OFF, LANES), 1) >= HEAD_DIM_B
    n_sub = tb // halo

    def sub_block(idx, carry):
        r = idx // n_sub
        u = idx - r * n_sub
        base = r + halo * u
        key_tok = i * tb - halo + base + dil * cc
        mask = in_win & (key_tok >= 0)
        cs_ = range(Q_CHUNKS)
        hs = range(HEADS_PER_GROUP)
        qc = [q_scr[c, pl.ds(base, N_OFF, stride=dil), :] * HEAD_DIM_B ** -0.5 for c in cs_]
        kc = [kv_scr[c, pl.ds(base, 2 * N_OFF, stride=dil), :].astype(bf16) for c in cs_]
        vc = [kv_scr[Q_CHUNKS + c, pl.ds(base, 2 * N_OFF, stride=dil), :].astype(bf16) for c in cs_]
        qm = [jnp.where(upper if h % 2 else jnp.logical_not(upper), qc[h // 2], 0.0).astype(bf16)
              for h in hs]
        s = [jnp.where(mask, _dot(qm[h], kc[h // 2], NT) - slopes[h] * dist, NEG) for h in hs]
        m = [jnp.max(s[h], axis=-1, keepdims=True) for h in hs]
        p = [jnp.exp(s[h] - m[h]) for h in hs]
        den = [jnp.sum(p[h], axis=-1, keepdims=True) for h in hs]
        pv = [_dot(p[h].astype(bf16), vc[h // 2]) / den[h] for h in hs]
        lse = [jnp.broadcast_to(m[h] + jnp.log(den[h]), (N_OFF, LANES)) for h in hs]
        for c in cs_:
            o_scr[c, pl.ds(base, N_OFF, stride=dil), :] = jnp.where(upper, pv[2 * c + 1], pv[2 * c])
            l_scr[c, pl.ds(base, N_OFF, stride=dil), :] = jnp.where(upper, lse[2 * c + 1], lse[2 * c])
        return carry

    lax.fori_loop(0, tb // N_OFF, sub_block, 0, unroll=2)
    for c in range(Q_CHUNKS):
        o_ref[0, :, c * LANES:(c + 1) * LANES] = o_scr[c]
        l_ref[0, :, c * LANES:(c + 1) * LANES] = l_scr[c]


def _attn_prompt(q, kv, g):
    b, t, _ = q.shape
    dil = DILATIONS[g]
    halo = N_OFF * dil
    tb = min(ATT_TB, t)
    assert tb % halo == 0 and t % tb == 0
    per = tb // halo
    return pl.pallas_call(
        functools.partial(_attn_prompt_body, dil=dil, slopes=_slopes(g), tb=tb),
        grid=(b, t // tb),
        in_specs=[pl.BlockSpec((1, tb, WIDTH_G), lambda bi, i: (bi, i, 0)),
                  pl.BlockSpec((1, halo, 2 * WIDTH_G),
                               lambda bi, i: (bi, jnp.maximum(i * per - 1, 0), 0)),
                  pl.BlockSpec((1, tb, 2 * WIDTH_G), lambda bi, i: (bi, i, 0))],
        out_specs=[pl.BlockSpec((1, tb, WIDTH_G), lambda bi, i: (bi, i, 0)),
                   pl.BlockSpec((1, tb, WIDTH_G), lambda bi, i: (bi, i, 0))],
        out_shape=[jax.ShapeDtypeStruct((b, t, WIDTH_G), f32),
                   jax.ShapeDtypeStruct((b, t, WIDTH_G), f32)],
        scratch_shapes=[pltpu.VMEM((Q_CHUNKS, tb, LANES), f32),
                        pltpu.VMEM((KV_CHUNKS, halo + tb, LANES), f32),
                        pltpu.VMEM((Q_CHUNKS, tb, LANES), f32),
                        pltpu.VMEM((Q_CHUNKS, tb, LANES), f32)],
        compiler_params=pltpu.CompilerParams(
            dimension_semantics=("arbitrary", "arbitrary"), vmem_limit_bytes=VMEM_LIMIT),
        name=f"attn_prompt_g{g}",
    )(q, kv, kv)


def _sample_consts(dil, t, win):
    rr = lax.broadcasted_iota(jnp.int32, (SUBLANES, win), 1)
    qi = lax.broadcasted_iota(jnp.int32, (SUBLANES, win), 0)
    nn = lax.broadcasted_iota(jnp.int32, (SUBLANES, SUBLANES), 1)
    qn = lax.broadcasted_iota(jnp.int32, (SUBLANES, SUBLANES), 0)
    lane = lax.broadcasted_iota(jnp.int32, (HEAD_DIM_B, LANES), 1)
    e_rows = lax.broadcasted_iota(jnp.int32, (SUBLANES, LANES), 0)
    e_lane = lax.broadcasted_iota(jnp.int32, (SUBLANES, LANES), 1)
    return dict(
        valid=((rr & (dil - 1)) == (qi & (dil - 1))) & (rr >= qi) & (qi < t),
        dist=(win + qi - rr).astype(f32),
        valid_n=(nn <= qn) & (((qn - nn) & (dil - 1)) == 0) & (qn < t),
        dist_n=(qn - nn).astype(f32),
        tail_lanes=lane >= LANES - t,
        real_rows=lax.broadcasted_iota(jnp.int32, (SUBLANES, HEAD_DIM_B), 0) < t,
        place=jnp.where((e_lane == e_rows + (LANES - t)) & (e_rows < t), 1.0, 0.0).astype(bf16))


def _sample_unit(c_in, c_out, q_b, kvn_b, o_b, l_b, cst, slopes, t, win):
    kvn = kvn_b[...]
    qs = q_b[...] * HEAD_DIM_B ** -0.5
    head = lambda a_, h, base=0: a_[:, base + h * HEAD_DIM_B:base + (h + 1) * HEAD_DIM_B]
    k3 = _split3(kvn)
    place = cst["place"]
    new_t = _dot(k3[0], place, TN) + _dot(k3[1], place, TN) + _dot(k3[2], place, TN)
    hs = range(HEADS_PER_GROUP)
    k_t = [c_in[0, h] for h in hs]
    v_t = [c_in[1, h] for h in hs]
    q8 = [head(qs, h).astype(bf16) for h in hs]
    s = [jnp.where(cst["valid"], _dot(q8[h], k_t[h].astype(bf16)) - slopes[h] * cst["dist"], NEG)
         for h in hs]
    sn = [jnp.where(cst["valid_n"],
                    _dot(q8[h], head(kvn, h).astype(bf16), NT) - slopes[h] * cst["dist_n"], NEG)
          for h in hs]
    m = [jnp.maximum(jnp.max(s[h], axis=-1, keepdims=True),
                     jnp.max(sn[h], axis=-1, keepdims=True)) for h in hs]
    p = [jnp.exp(s[h] - m[h]) for h in hs]
    pn = [jnp.exp(sn[h] - m[h]) for h in hs]
    den = [jnp.sum(p[h], axis=-1, keepdims=True) + jnp.sum(pn[h], axis=-1, keepdims=True)
           for h in hs]
    acc = [_dot(p[h].astype(bf16), v_t[h].astype(bf16), NT) + _mm1(pn[h], head(kvn, h, WIDTH_G))
           for h in hs]
    for h in hs:
        o_b[:, h * HEAD_DIM_B:(h + 1) * HEAD_DIM_B] = jnp.where(cst["real_rows"], acc[h] / den[h], 0.0)
        l_b[:, h * HEAD_DIM_B:(h + 1) * HEAD_DIM_B] = jnp.broadcast_to(
            m[h] + jnp.log(den[h]), (SUBLANES, HEAD_DIM_B))
        for kv, x in ((0, k_t[h]), (1, v_t[h])):
            rolled = pltpu.roll(x, win - t, axis=1)
            fresh = new_t[(kv * HEADS_PER_GROUP + h) * HEAD_DIM_B:
                          (kv * HEADS_PER_GROUP + h + 1) * HEAD_DIM_B, :]
            if win > LANES:
                c_out[kv, h, :, pl.ds(0, win - LANES)] = rolled[:, :win - LANES]
            c_out[kv, h, :, pl.ds(win - LANES, LANES)] = jnp.where(
                cst["tail_lanes"], fresh, rolled[:, win - LANES:])


def _attn_sample_body(*refs, dil, slopes, nb, t, win):
    q_ref, kvn_ref, c_ref = refs[:3]
    o_ref, l_ref, cn_ref = refs[-3:]
    cst = _sample_consts(dil, t, win)

    def one_sequence(b, carry):
        _sample_unit(c_ref.at[0, b], cn_ref.at[0, b], q_ref.at[b], kvn_ref.at[b], o_ref.at[b],
                     l_ref.at[b], cst, slopes, t, win)
        return carry

    lax.fori_loop(0, nb, one_sequence, 0, unroll=2 if nb % 2 == 0 else 1)


def _attn_sample(q, kvn, cache_t, prev, layer, g, t):
    bsz, win = cache_t.shape[1], cache_t.shape[-1]
    dil = DILATIONS[g]
    nb = max(1, min(16, 2048 // win))
    aliased = prev is not None
    cache_spec = pl.BlockSpec((1, nb, 2, HEADS_PER_GROUP, HEAD_DIM_B, win),
                              lambda i: (layer, i, 0, 0, 0, 0))
    out_spec = pl.BlockSpec((nb, SUBLANES, WIDTH_G), lambda i: (i, 0, 0))
    in_specs = [out_spec, pl.BlockSpec((nb, SUBLANES, 2 * WIDTH_G), lambda i: (i, 0, 0)), cache_spec]
    args = [q, kvn, cache_t]
    if aliased:
        in_specs.append(pl.BlockSpec(memory_space=pl.ANY))
        args.append(prev)
    o, l, cn = pl.pallas_call(
        functools.partial(_attn_sample_body, dil=dil, slopes=_slopes(g), nb=nb, t=t, win=win),
        grid=(bsz // nb,),
        in_specs=in_specs,
        out_specs=[out_spec, out_spec, cache_spec],
        out_shape=[jax.ShapeDtypeStruct((bsz, SUBLANES, WIDTH_G), f32),
                   jax.ShapeDtypeStruct((bsz, SUBLANES, WIDTH_G), f32),
                   jax.ShapeDtypeStruct(cache_t.shape, f32)],
        input_output_aliases={len(args) - 1: 2} if aliased else {},
        compiler_params=pltpu.CompilerParams(
            dimension_semantics=("arbitrary",), vmem_limit_bytes=VMEM_LIMIT),
        name=f"attn_sample_g{g}",
    )(*args)
    return o, l, cn


FF_CHUNK = 512


def _tail_front(x_ref, oa_ref, o_refs, l_refs, gate_ref, wa_ref, wb_ref, wo_ref, n2_ref):
    l0, l1, l2 = (r[...] for r in l_refs)
    lm = jnp.maximum(jnp.maximum(l0, l1), l2)
    e0, e1, e2 = jnp.exp(l0 - lm), jnp.exp(l1 - lm), jnp.exp(l2 - lm)
    den = e0 + e1 + e2
    ob = (e0 / den) * o_refs[0][...] + (e1 / den) * o_refs[1][...] + (e2 / den) * o_refs[2][...]
    ya = _mm1(oa_ref[...], wa_ref[...])
    yb = _mm1(ob, wb_ref[...])
    merged = (_sigmoid(gate_ref[:, :D_MODEL].astype(f32)) * ya
              + _sigmoid(gate_ref[:, D_MODEL:].astype(f32)) * yb)
    x1 = x_ref[...] + _mm1(merged, wo_ref[...])
    return x1, _rms(x1, n2_ref[...]).astype(bf16)


def _tail_ff(acc, h2, wu_ref, wd_ref, chunks):
    for j in chunks:
        up = _dot(h2, wu_ref[:, j * FF_CHUNK:(j + 1) * FF_CHUNK])
        act = jnp.square(jnp.maximum(up, 0.0)).astype(bf16)
        acc = acc + _dot(act, wd_ref[j * FF_CHUNK:(j + 1) * FF_CHUNK, :])
    return acc


def _tail_body(x_ref, oa_ref, o0_ref, o1_ref, o2_ref, l0_ref, l1_ref, l2_ref, gate_ref,
               wa_ref, wb_ref, wo_ref, n2_ref, wu_ref, wd_ref, fn_ref, y_ref, *, final):
    x1, h2 = _tail_front(x_ref, oa_ref, (o0_ref, o1_ref, o2_ref), (l0_ref, l1_ref, l2_ref),
                         gate_ref, wa_ref, wb_ref, wo_ref, n2_ref)
    acc = _tail_ff(x1, h2, wu_ref, wd_ref, range(D_FF // FF_CHUNK))
    if final:
        acc = _rms(acc, fn_ref[...])
    y_ref[...] = acc


def _tail_specs(tm):
    tok = lambda wd_: pl.BlockSpec((tm, wd_), lambda i: (i, 0))
    const = lambda shape: pl.BlockSpec(shape, lambda i: (0, 0), pipeline_mode=pl.Buffered(1))
    in_specs = [tok(D_MODEL), tok(WIDTH_A)] + [tok(WIDTH_G)] * 6 + [
        tok(2 * D_MODEL), const((WIDTH_A, D_MODEL)), const((WIDTH_G, D_MODEL)),
        const((D_MODEL, D_MODEL)), const((1, D_MODEL)), const((D_MODEL, D_FF)),
        const((D_FF, D_MODEL)), const((1, D_MODEL))]
    return in_specs, tok(D_MODEL)


def _tail(x, oa, outs, lses, gate, wa, wb, wo, n2, wu, wd, fn, final, tm=512):
    m = x.shape[0]
    tm = min(tm, m)
    assert m % tm == 0
    in_specs, out_spec = _tail_specs(tm)
    return pl.pallas_call(
        functools.partial(_tail_body, final=final),
        grid=(m // tm,),
        in_specs=in_specs,
        out_specs=out_spec,
        out_shape=jax.ShapeDtypeStruct((m, D_MODEL), f32),
        compiler_params=pltpu.CompilerParams(
            dimension_semantics=("arbitrary",), vmem_limit_bytes=VMEM_LIMIT),
        name="tail",
    )(x, oa, *outs, *lses, gate, wa, wb, wo, n2, wu, wd, fn)


FUSED_UNITS = 2
N_TAIL_IN = 16


def _tail_cache_body(*refs, final, dil, slopes, t, win, layer, aliased):
    (x_ref, oa_ref, o0_ref, o1_ref, o2_ref, l0_ref, l1_ref, l2_ref, gate_ref,
     wa_ref, wb_ref, wo_ref, n2_ref, wu_ref, wd_ref, fn_ref) = refs[:N_TAIL_IN]
    q_ref, kvn_ref, c_hbm = refs[N_TAIL_IN:N_TAIL_IN + 3]
    n_in = N_TAIL_IN + (4 if aliased else 3)
    y_ref, os_ref, ls_ref, cn_hbm = refs[n_in:n_in + 4]
    in_buf, out_buf, in_sem, out_sem = refs[n_in + 4:]
    i = pl.program_id(0)
    n = pl.num_programs(0)
    cst = _sample_consts(dil, t, win)

    def fetch(u, slot):
        return pltpu.make_async_copy(c_hbm.at[layer, u], in_buf.at[slot], in_sem.at[slot])

    def flush(u, slot):
        return pltpu.make_async_copy(out_buf.at[slot], cn_hbm.at[layer, u], out_sem.at[slot])

    @pl.when(i == 0)
    def _():
        fetch(0, 0).start()

    n_ff = D_FF // FF_CHUNK
    first_part = (3 * n_ff) // 8
    acc = h2 = None
    for j in range(FUSED_UNITS):
        u = i * FUSED_UNITS + j
        fetch(u, j).wait()
        if j + 1 < FUSED_UNITS:
            fetch(u + 1, j + 1).start()
        else:
            @pl.when(i + 1 < n)
            def _():
                fetch(u + 1, 0).start()

        @pl.when(i >= 1)
        def _():
            flush(u - FUSED_UNITS, j).wait()

        if j == 0:
            x1, h2 = _tail_front(x_ref, oa_ref, (o0_ref, o1_ref, o2_ref),
                                 (l0_ref, l1_ref, l2_ref), gate_ref, wa_ref, wb_ref, wo_ref, n2_ref)
            acc = _tail_ff(x1, h2, wu_ref, wd_ref, range(first_part))
        else:
            acc = _tail_ff(acc, h2, wu_ref, wd_ref, range(first_part, n_ff))
            if final:
                acc = _rms(acc, fn_ref[...])
            y_ref[...] = acc
        _sample_unit(in_buf.at[j], out_buf.at[j], q_ref.at[j], kvn_ref.at[j], os_ref.at[j],
                     ls_ref.at[j], cst, slopes, t, win)
        flush(u, j).start()

    @pl.when(i == n - 1)
    def _():
        for j in range(FUSED_UNITS):
            flush(i * FUSED_UNITS + j, j).wait()


def _tail_with_cache(x, oa, outs, lses, gate, wa, wb, wo, n2, wu, wd, fn, final,
                     q, kvn, cache_t, prev, layer, g, t):
    m = x.shape[0]
    bsz, win = cache_t.shape[1], cache_t.shape[-1]
    assert FUSED_UNITS == 2 and bsz % FUSED_UNITS == 0 and m % (bsz // FUSED_UNITS) == 0
    steps = bsz // FUSED_UNITS
    tm = m // steps
    assert tm % SUBLANES == 0
    in_specs, y_spec = _tail_specs(tm)
    o_spec = pl.BlockSpec((FUSED_UNITS, SUBLANES, WIDTH_G), lambda i: (i, 0, 0))
    hbm = pl.BlockSpec(memory_space=pl.ANY)
    in_specs = in_specs + [o_spec, pl.BlockSpec((FUSED_UNITS, SUBLANES, 2 * WIDTH_G),
                                                lambda i: (i, 0, 0)), hbm]
    args = [x, oa, *outs, *lses, gate, wa, wb, wo, n2, wu, wd, fn, q, kvn, cache_t]
    aliased = prev is not None
    if aliased:
        in_specs.append(hbm)
        args.append(prev)
    slab = (FUSED_UNITS, 2, HEADS_PER_GROUP, HEAD_DIM_B, win)
    y, o, l, cn = pl.pallas_call(
        functools.partial(_tail_cache_body, final=final, dil=DILATIONS[g], slopes=_slopes(g), t=t,
                          win=win, layer=layer, aliased=aliased),
        grid=(steps,),
        in_specs=in_specs,
        out_specs=[y_spec, o_spec, o_spec, hbm],
        out_shape=[jax.ShapeDtypeStruct((m, D_MODEL), f32),
                   jax.ShapeDtypeStruct((bsz, SUBLANES, WIDTH_G), f32),
                   jax.ShapeDtypeStruct((bsz, SUBLANES, WIDTH_G), f32),
                   jax.ShapeDtypeStruct(cache_t.shape, f32)],
        scratch_shapes=[pltpu.VMEM(slab, f32), pltpu.VMEM(slab, f32),
                        pltpu.SemaphoreType.DMA((FUSED_UNITS,)),
                        pltpu.SemaphoreType.DMA((FUSED_UNITS,))],
        input_output_aliases={len(args) - 1: 3} if aliased else {},
        compiler_params=pltpu.CompilerParams(
            dimension_semantics=("arbitrary",), vmem_limit_bytes=VMEM_LIMIT),
        name="tail_cache",
    )(*args)
    return y, o, l, cn


def _lane_row(vals):
    return jnp.zeros((1, LANES), f32).at[0, :vals.shape[0]].set(vals.astype(f32))


def _layer_params(l, norm1, w_in, conv_w, a_log, dt_bias, gnorm_a, w_br_a, w_br_b, w_out, norm2,
                  w_up, w_down):
    return dict(
        n1=norm1[l].reshape(1, D_MODEL), w_in=_permute_w_in(w_in[l]), cw=conv_w[l],
        nega=_lane_row(-jnp.exp(a_log[l].astype(f32))), dtb=_lane_row(dt_bias[l]),
        gn=gnorm_a[l].reshape(1, HEAD_DIM_A).astype(f32),
        wa=w_br_a[l].astype(bf16), wb=w_br_b[l].astype(bf16), wo=w_out[l].astype(bf16),
        n2=norm2[l].reshape(1, D_MODEL), wu=w_up[l].astype(bf16), wd=w_down[l].astype(bf16))


def _layer(xp, xs, ts, p, fn, final, layer, caches_t, prev_caches, s_all, s_prev, conv8):
    bp, tp, _ = xp.shape
    xpf = xp.reshape(bp * tp, D_MODEL)
    cs, z, q0, q1, q2, kv0, kv1, kv2, gate_p, ab, last = _inproj(xpf, p["n1"], p["w_in"],
                                                                 conv_w=p["cw"], seq_len=tp)
    s0 = jnp.zeros((bp, N_HEADS_A, HEAD_DIM_A, HEAD_DIM_A), f32)
    oa_p, s_new_p = _delta_prompt(cs.reshape(bp, tp, 3 * WIDTH_A), ab.reshape(bp, tp, LANES),
                                  z.reshape(bp, tp, WIDTH_A), p["nega"], p["dtb"], p["gn"], s0)
    outs_p, lses_p, wins_p = [], [], []
    for g, (qg, kvg) in enumerate(((q0, kv0), (q1, kv1), (q2, kv2))):
        kv3 = kvg.reshape(bp, tp, 2 * WIDTH_G)
        o, l = _attn_prompt(qg.reshape(bp, tp, WIDTH_G), kv3, g)
        outs_p.append(o.reshape(bp * tp, WIDTH_G))
        lses_p.append(l.reshape(bp * tp, WIDTH_G))
        keep = min(WINDOWS[g], tp)
        wins_p.append(kv3[:, tp - keep:].reshape(bp, keep, 2, HEADS_PER_GROUP, HEAD_DIM_B))
    conv_p = last[:, SUBLANES - (CONV_W - 1):]

    bs = xs.shape[0]
    xsf = xs.reshape(bs * SUBLANES, D_MODEL)
    qkv, z, q0, q1, q2, kv0, kv1, kv2, gate_s, ab = _inproj(xsf, p["n1"], p["w_in"])
    rows8 = lambda a_: a_.reshape(bs, SUBLANES, a_.shape[-1])
    qkv8 = rows8(qkv)
    oa_s, s_stack = _delta_sample(qkv8, conv8, rows8(ab), rows8(z), p["cw"], p["nega"], p["dtb"],
                                  p["gn"], s_all, s_prev, layer, ts)
    outs_s, lses_s, new_caches = [], [], []
    for g, (qg, kvg) in enumerate(((q0, kv0), (q1, kv1))):
        o, l, cn = _attn_sample(rows8(qg), rows8(kvg), caches_t[g], prev_caches[g], layer, g, ts)
        outs_s.append(o.reshape(bs * SUBLANES, WIDTH_G))
        lses_s.append(l.reshape(bs * SUBLANES, WIDTH_G))
        new_caches.append(cn)

    g = N_GROUPS - 1
    yp, o, l, cn = _tail_with_cache(xpf, oa_p.reshape(bp * tp, WIDTH_A), outs_p, lses_p, gate_p,
                                    p["wa"], p["wb"], p["wo"], p["n2"], p["wu"], p["wd"], fn, final,
                                    rows8(q2), rows8(kv2), caches_t[g], prev_caches[g], layer, g, ts)
    outs_s.append(o.reshape(bs * SUBLANES, WIDTH_G))
    lses_s.append(l.reshape(bs * SUBLANES, WIDTH_G))
    new_caches.append(cn)
    ys = _tail(xsf, oa_s.reshape(bs * SUBLANES, WIDTH_A), outs_s, lses_s, gate_s, p["wa"], p["wb"],
               p["wo"], p["n2"], p["wu"], p["wd"], fn, final)
    conv_s = qkv8[:, ts - (CONV_W - 1):ts]
    return (yp.reshape(bp, tp, D_MODEL), ys.reshape(bs, SUBLANES, D_MODEL), wins_p, s_new_p, conv_p,
            new_caches, s_stack, conv_s)


def kernel(x_prompt, x_sample, cache_win0, cache_win1, cache_win2, state_delta, state_conv, norm1,
           w_in, conv_w, a_log, dt_bias, gnorm_a, w_br_a, w_br_b, w_out, norm2, w_up, w_down,
           final_norm):
    depth = w_in.shape[0]
    ts = x_sample.shape[1]
    assert ts + CONV_W - 1 <= SUBLANES
    pad_rows = lambda a_, axis: jnp.pad(
        a_, [(0, SUBLANES - a_.shape[axis]) if d == axis else (0, 0) for d in range(a_.ndim)])
    conv8 = pad_rows(state_conv.astype(f32), 2)
    fn = final_norm.reshape(1, D_MODEL)
    caches_t = [c.transpose(0, 1, 3, 4, 5, 2) for c in (cache_win0, cache_win1, cache_win2)]
    new_caches = [None] * N_GROUPS
    s_stack = None
    xp, xs = x_prompt, pad_rows(x_sample, 1)
    wins_p = [[] for _ in range(N_GROUPS)]
    delta_p, conv_p, conv_s = [], [], []
    for l in range(depth):
        p = _layer_params(l, norm1, w_in, conv_w, a_log, dt_bias, gnorm_a, w_br_a, w_br_b, w_out,
                          norm2, w_up, w_down)
        xp, xs, wp, sp, cp, new_caches, s_stack, cs = _layer(
            xp, xs, ts, p, fn, l == depth - 1, l, caches_t, new_caches, state_delta, s_stack, conv8)
        for g in range(N_GROUPS):
            wins_p[g].append(wp[g])
        delta_p.append(sp)
        conv_p.append(cp)
        conv_s.append(cs)
    st = jnp.stack
    wins_s = [c.transpose(0, 1, 5, 2, 3, 4) for c in new_caches]
    return (xp, xs[:, :ts], st(wins_p[0]), st(wins_p[1]), st(wins_p[2]), st(delta_p), st(conv_p),
            wins_s[0], wins_s[1], wins_s[2], s_stack, st(conv_s))
```

```python
import functools

import jax
import jax.numpy as jnp
from jax import lax
from jax.experimental import pallas as pl
from jax.experimental.pallas import tpu as pltpu

f32 = jnp.float32
bf16 = jnp.bfloat16

D_MODEL = 1024
N_HEADS_A = 4
HEAD_DIM_A = 128
WIDTH_A = N_HEADS_A * HEAD_DIM_A
CONV_W = 4
WINDOWS = (128, 512, 2048)
DILATIONS = (1, 4, 16)
N_GROUPS = 3
HEADS_PER_GROUP = 4
HEAD_DIM_B = 64
N_HEADS_B = N_GROUPS * HEADS_PER_GROUP
WIDTH_G = HEADS_PER_GROUP * HEAD_DIM_B
D_FF = 4 * D_MODEL
EPS = 1e-6
N_OFF = 128
NEG = -1e30

SUBLANES = 8
LANES = 128
VMEM_LIMIT = 56 * 1024 * 1024

SEG_WIDTHS = (3 * WIDTH_A, WIDTH_A, WIDTH_G, WIDTH_G, WIDTH_G,
              2 * WIDTH_G, 2 * WIDTH_G, 2 * WIDTH_G, 2 * D_MODEL, LANES)
N_PERM = sum(SEG_WIDTHS)
SEG_DTYPES = (f32,) * 8 + (bf16, f32)

NN = (((1,), (0,)), ((), ()))
NT = (((1,), (1,)), ((), ()))
TN = (((0,), (0,)), ((), ()))


def _slopes(g):
    return tuple(2.0 ** (-8.0 * (g * HEADS_PER_GROUP + h + 1) / N_HEADS_B)
                 for h in range(HEADS_PER_GROUP))


def _dot(a, b, dims=NN):
    return lax.dot_general(a, b, dims, preferred_element_type=f32)


def _mm1(a, b, dims=NN):
    return _dot(a.astype(bf16), b.astype(bf16), dims)


def _split(a):
    hi = a.astype(bf16)
    lo = (a - hi.astype(f32)).astype(bf16)
    return hi, lo


def _split3(a):
    hi = a.astype(bf16)
    r = a - hi.astype(f32)
    mid = r.astype(bf16)
    lo = (r - mid.astype(f32)).astype(bf16)
    return hi, mid, lo


def _mm3s(a, b, dims=NN):
    return _dot(a[0], b[0], dims) + _dot(a[0], b[1], dims) + _dot(a[1], b[0], dims)


def _mm3(a, b, dims=NN):
    return _mm3s(_split(a), _split(b), dims)


def _sigmoid(x):
    return 1.0 / (1.0 + jnp.exp(-x))


def _softplus(x):
    return jnp.maximum(x, 0.0) + jnp.log1p(jnp.exp(-jnp.abs(x)))


def _rms(x, g):
    return x * lax.rsqrt(jnp.mean(x * x, axis=-1, keepdims=True) + EPS) * g


def _inproj_body(x_ref, g_ref, w_ref, *out_refs):
    h = _rms(x_ref[...], g_ref[...]).astype(bf16)
    off = 0
    for o_ref, width in zip(out_refs, SEG_WIDTHS):
        o_ref[...] = _dot(h, w_ref[:, off:off + width]).astype(o_ref.dtype)
        off += width


def _inproj_conv_body(x_ref, g_ref, w_ref, cw_ref, *refs, tm, tiles_per_seq):
    out_refs, last_ref, xs_ref = refs[:len(SEG_WIDTHS)], refs[-2], refs[-1]
    i = pl.program_id(0)

    @pl.when(lax.rem(i, tiles_per_seq) == 0)
    def _():
        xs_ref[pl.ds(0, SUBLANES), :] = jnp.zeros((SUBLANES, SEG_WIDTHS[0]), f32)

    h = _rms(x_ref[...], g_ref[...]).astype(bf16)
    xs_ref[pl.ds(SUBLANES, tm), :] = _dot(h, w_ref[:, :SEG_WIDTHS[0]])
    conv = xs_ref[pl.ds(SUBLANES - CONV_W + 1, tm), :] * cw_ref[0:1, :]
    for j in range(1, CONV_W):
        conv = conv + xs_ref[pl.ds(SUBLANES - CONV_W + 1 + j, tm), :] * cw_ref[j:j + 1, :]
    last = xs_ref[pl.ds(tm, SUBLANES), :]
    xs_ref[pl.ds(0, SUBLANES), :] = last
    last_ref[0] = last
    out_refs[0][...] = conv * _sigmoid(conv)
    off = SEG_WIDTHS[0]
    for o_ref, width in zip(out_refs[1:], SEG_WIDTHS[1:]):
        o_ref[...] = _dot(h, w_ref[:, off:off + width]).astype(o_ref.dtype)
        off += width


def _inproj(x, g, w, tm=512, conv_w=None, seq_len=None):
    m = x.shape[0]
    tm = min(tm, m)
    assert m % tm == 0
    in_specs = [pl.BlockSpec((tm, D_MODEL), lambda i: (i, 0)),
                pl.BlockSpec((1, D_MODEL), lambda i: (0, 0)),
                pl.BlockSpec((D_MODEL, N_PERM), lambda i: (0, 0), pipeline_mode=pl.Buffered(1))]
    out_specs = [pl.BlockSpec((tm, wd), lambda i: (i, 0)) for wd in SEG_WIDTHS]
    out_shape = [jax.ShapeDtypeStruct((m, wd), dt) for wd, dt in zip(SEG_WIDTHS, SEG_DTYPES)]
    args = [x, g, w]
    body, scratch = _inproj_body, []
    if conv_w is not None:
        assert seq_len % tm == 0 and tm >= SUBLANES
        per = seq_len // tm
        in_specs.append(pl.BlockSpec((CONV_W, SEG_WIDTHS[0]), lambda i: (0, 0)))
        out_specs.append(pl.BlockSpec((1, SUBLANES, SEG_WIDTHS[0]), lambda i: (i // per, 0, 0)))
        out_shape.append(jax.ShapeDtypeStruct((m // seq_len, SUBLANES, SEG_WIDTHS[0]), f32))
        args.append(conv_w)
        body = functools.partial(_inproj_conv_body, tm=tm, tiles_per_seq=per)
        scratch = [pltpu.VMEM((tm + SUBLANES, SEG_WIDTHS[0]), f32)]
    return pl.pallas_call(
        body,
        grid=(m // tm,),
        in_specs=in_specs,
        out_specs=out_specs,
        out_shape=out_shape,
        scratch_shapes=scratch,
        compiler_params=pltpu.CompilerParams(
            dimension_semantics=("arbitrary",), vmem_limit_bytes=VMEM_LIMIT),
        name="inproj",
    )(*args)


def _permute_w_in(w):
    o_z = 3 * WIDTH_A
    o_ab = o_z + WIDTH_A
    o_q = o_ab + 2 * N_HEADS_A
    o_k = o_q + N_GROUPS * WIDTH_G
    o_v = o_k + N_GROUPS * WIDTH_G
    o_gate = o_v + N_GROUPS * WIDTH_G
    parts = [w[:, :o_ab], w[:, o_q:o_k]]
    for g in range(N_GROUPS):
        parts.append(w[:, o_k + g * WIDTH_G:o_k + (g + 1) * WIDTH_G])
        parts.append(w[:, o_v + g * WIDTH_G:o_v + (g + 1) * WIDTH_G])
    parts.append(w[:, o_gate:])
    parts.append(w[:, o_ab:o_q])
    parts.append(jnp.zeros((w.shape[0], LANES - 2 * N_HEADS_A), w.dtype))
    return jnp.concatenate(parts, axis=1).astype(bf16)


CHUNK = 128
INV_BASE = 16
HEADS = tuple(range(N_HEADS_A))


def _tri_inverse_heads(a_list, row, col):
    eye = jnp.where(row == col, 1.0, 0.0)
    blk = lambda idx, size: jnp.right_shift(idx, size.bit_length() - 1)
    same16 = blk(row, INV_BASE) == blk(col, INV_BASE)
    ad = [jnp.where(same16, a, 0.0) for a in a_list]
    p = [eye - x for x in ad]
    xs = [_split(x) for x in ad]
    x = [_mm3s(s, s) for s in xs]
    xs = [_split(v) for v in x]
    p = [pv + _mm3s(_split(pv), xx) for pv, xx in zip(p, xs)]
    xb = [s[0] for s in xs]
    n = 4
    while n < INV_BASE:
        xb = [_dot(v, v).astype(bf16) for v in xb]
        p = [pv + _dot(pv.astype(bf16), xx) for pv, xx in zip(p, xb)]
        n *= 2
    s = INV_BASE
    while s < CHUNK:
        pair = (blk(row, 2 * s) == blk(col, 2 * s)) & (blk(row, s) != blk(col, s))
        es = [_split(jnp.where(pair, a, 0.0)) for a in a_list]
        ps = [_split(v) for v in p]
        pe = [_split(_mm3s(pp, ee)) for pp, ee in zip(ps, es)]
        p = [pv - _mm3s(x, pp) for pv, x, pp in zip(p, pe, ps)]
        s *= 2
    return [_split(v) for v in p]


def _delta_prompt_body(cs_ref, ab_ref, z_ref, nega_ref, dtb_ref, gn_ref, s0_ref,
                       o_ref, s_out_ref, s_ref, *, nchunk):
    c = pl.program_id(1)

    @pl.when(c == 0)
    def _():
        s_ref[...] = s0_ref[0]

    items = [(ck, h) for ck in range(nchunk) for h in HEADS]
    rows = lambda ck: pl.ds(ck * CHUNK, CHUNK)
    log_a = [nega_ref[...] * _softplus(ab_ref[0, rows(ck), :] + dtb_ref[...])
             for ck in range(nchunk)]
    beta_all = [_sigmoid(ab_ref[0, rows(ck), :]) for ck in range(nchunk)]

    row = lax.broadcasted_iota(jnp.int32, (CHUNK, CHUNK), 0)
    col = lax.broadcasted_iota(jnp.int32, (CHUNK, CHUNK), 1)
    causal = row >= col
    strict = row > col
    ltri = jnp.where(causal, 1.0, 0.0).astype(bf16)

    def l2n(x, scale):
        return x * (lax.rsqrt(jnp.sum(x * x, axis=-1, keepdims=True) + EPS) * scale)

    def part(ck, h, which):
        lo = which * WIDTH_A + h * HEAD_DIM_A
        return cs_ref[0, rows(ck), lo:lo + HEAD_DIM_A]

    q = [l2n(part(ck, h, 0), HEAD_DIM_A ** -0.5) for ck, h in items]
    k = [l2n(part(ck, h, 1), 1.0) for ck, h in items]
    v = [part(ck, h, 2) for ck, h in items]
    beta = [beta_all[ck][:, N_HEADS_A + h:N_HEADS_A + h + 1] for ck, h in items]
    g3 = [_split3(jnp.broadcast_to(log_a[ck][:, h:h + 1], (CHUNK, HEAD_DIM_A))) for ck, h in items]
    cum = [_dot(ltri, t[0]) + _dot(ltri, t[1]) + _dot(ltri, t[2]) for t in g3]
    decay = [jnp.exp(jnp.where(causal, x - x.T, NEG)) for x in cum]
    e_g = [jnp.exp(x) for x in cum]
    g_last = [x[CHUNK - 1:CHUNK, :] for x in cum]
    kb = [a * b for a, b in zip(k, beta)]
    k_bf = [x.astype(bf16) for x in k]
    a_mat = [jnp.where(strict, _dot(x.astype(bf16), y, NT) * d, 0.0)
             for x, y, d in zip(kb, k_bf, decay)]
    t_inv = _tri_inverse_heads(a_mat, row, col)
    w = [_mm3s(t, _split(x * e)) for t, x, e in zip(t_inv, kb, e_g)]
    u = [_mm3s(t, _split(x * b)) for t, x, b in zip(t_inv, v, beta)]
    qk = [_dot(x.astype(bf16), y, NT) * d for x, y, d in zip(q, k_bf, decay)]
    qg = [x * e for x, e in zip(q, e_g)]
    kd = [x * jnp.exp(gl - cm) for x, gl, cm in zip(k, g_last, cum)]
    s_cur = [s_ref[h] for h in HEADS]
    for ck in range(nchunk):
        n0 = ck * N_HEADS_A
        s_sp = [_split(x) for x in s_cur]
        v_new = [u[n0 + h] - _mm3s(_split(w[n0 + h]), s_sp[h]) for h in HEADS]
        o = [_dot(qg[n0 + h].astype(bf16), s_sp[h][0]) + _mm1(qk[n0 + h], v_new[h]) for h in HEADS]
        s_cur = [s_cur[h] * jnp.exp(g_last[n0 + h]) + _mm3(kd[n0 + h], v_new[h], TN) for h in HEADS]
        for h in HEADS:
            lo = h * HEAD_DIM_A
            zz = z_ref[0, rows(ck), lo:lo + HEAD_DIM_A]
            o_ref[0, rows(ck), lo:lo + HEAD_DIM_A] = _rms(o[h], gn_ref[...]) * (zz * _sigmoid(zz))
    for h in HEADS:
        s_ref[h] = s_cur[h]

    @pl.when(c == pl.num_programs(1) - 1)
    def _():
        s_out_ref[0] = s_ref[...]


DELTA_CHUNKS_PER_STEP = 2


def _delta_prompt(cs, ab, z, nega, dtb, gn, s0):
    b, t, _ = cs.shape
    nchunk = DELTA_CHUNKS_PER_STEP if t % (DELTA_CHUNKS_PER_STEP * CHUNK) == 0 else 1
    tb = nchunk * CHUNK
    assert t % tb == 0
    return pl.pallas_call(
        functools.partial(_delta_prompt_body, nchunk=nchunk),
        grid=(b, t // tb),
        in_specs=[pl.BlockSpec((1, tb, 3 * WIDTH_A), lambda i, c: (i, c, 0)),
                  pl.BlockSpec((1, tb, LANES), lambda i, c: (i, c, 0)),
                  pl.BlockSpec((1, tb, WIDTH_A), lambda i, c: (i, c, 0)),
                  pl.BlockSpec((1, LANES), lambda i, c: (0, 0)),
                  pl.BlockSpec((1, LANES), lambda i, c: (0, 0)),
                  pl.BlockSpec((1, HEAD_DIM_A), lambda i, c: (0, 0)),
                  pl.BlockSpec((1, N_HEADS_A, HEAD_DIM_A, HEAD_DIM_A), lambda i, c: (i, 0, 0, 0))],
        out_specs=[pl.BlockSpec((1, tb, WIDTH_A), lambda i, c: (i, c, 0)),
                   pl.BlockSpec((1, N_HEADS_A, HEAD_DIM_A, HEAD_DIM_A), lambda i, c: (i, 0, 0, 0))],
        out_shape=[jax.ShapeDtypeStruct((b, t, WIDTH_A), f32),
                   jax.ShapeDtypeStruct((b, N_HEADS_A, HEAD_DIM_A, HEAD_DIM_A), f32)],
        scratch_shapes=[pltpu.VMEM((N_HEADS_A, HEAD_DIM_A, HEAD_DIM_A), f32)],
        compiler_params=pltpu.CompilerParams(
            dimension_semantics=("arbitrary", "arbitrary"), vmem_limit_bytes=VMEM_LIMIT),
        name="delta_prompt",
    )(cs, ab, z, nega, dtb, gn, s0)


def _delta_sample_body(*refs, nb, t, aliased):
    x_ref, st_ref, ab_ref, z_ref, cw_ref, nega_ref, dtb_ref, gn_ref, s0_ref = refs[:9]
    o_ref, s_out_ref = refs[-2:]
    rowi = lax.broadcasted_iota(jnp.int32, (SUBLANES, LANES), 0)
    real = rowi < t
    seqs = range(nb)
    items = [(b, h) for b in seqs for h in HEADS]
    conv = [sum((pltpu.roll(x_ref[b], CONV_W - 1 - j, axis=0)
                 + pltpu.roll(st_ref[0, b], (SUBLANES - j) % SUBLANES, axis=0)) * cw_ref[j:j + 1, :]
                for j in range(CONV_W)) for b in seqs]
    cs = [x * _sigmoid(x) for x in conv]
    log_a = [jnp.where(real, nega_ref[...] * _softplus(ab_ref[b] + dtb_ref[...]), 0.0) for b in seqs]
    beta_all = [jnp.where(real, _sigmoid(ab_ref[b]), 0.0) for b in seqs]

    def part(b, h, which):
        lo = which * WIDTH_A + h * HEAD_DIM_A
        return cs[b][:, lo:lo + HEAD_DIM_A]

    def l2n(x, scale):
        return x * (lax.rsqrt(jnp.sum(x * x, axis=-1, keepdims=True) + EPS) * scale)

    q = [l2n(part(b, h, 0), HEAD_DIM_A ** -0.5) for b, h in items]
    k = [l2n(part(b, h, 1), 1.0) for b, h in items]
    v = [part(b, h, 2) for b, h in items]
    beta = [beta_all[b][:, N_HEADS_A + h:N_HEADS_A + h + 1] for b, h in items]
    gb = [jnp.broadcast_to(log_a[b][:, h:h + 1], (SUBLANES, LANES)) for b, h in items]
    cum = [sum(jnp.where(rowi >= j, x[j:j + 1, :], 0.0) for j in range(t)) for x in gb]
    g_last = [x[t - 1:t, :] for x in cum]
    e_g = [jnp.exp(x) for x in cum]
    kb = [x * y for x, y in zip(k, beta)]
    w = [x * y for x, y in zip(kb, e_g)]
    u = [x * y for x, y in zip(v, beta)]
    dec = [[jnp.exp(jnp.where(rowi >= j, x - x[j:j + 1, :], NEG)) for j in range(t)] for x in cum]
    qk_cols = [[jnp.sum(q[n] * k[n][j:j + 1, :], axis=-1, keepdims=True) * dec[n][j]
                for j in range(t)] for n in range(len(items))]
    a_cols = [[jnp.where(rowi > j, jnp.sum(kb[n] * k[n][j:j + 1, :], axis=-1, keepdims=True)
                         * dec[n][j], 0.0) for j in range(t - 1)] for n in range(len(items))]
    for j in range(t - 1):
        w = [x - a[j] * x[j:j + 1, :] for x, a in zip(w, a_cols)]
        u = [x - a[j] * x[j:j + 1, :] for x, a in zip(u, a_cols)]
    qg = [x * y for x, y in zip(q, e_g)]
    kd = [x * jnp.exp(gl - cm) for x, gl, cm in zip(k, g_last, cum)]
    s_old = [s0_ref[0, b, h] for b, h in items]
    s_sp = [_split(x) for x in s_old]
    v_new = [uu - _mm3s(_split(ww), ss) for uu, ww, ss in zip(u, w, s_sp)]
    o = [_mm3s(_split(x), ss) for x, ss in zip(qg, s_sp)]
    o = [x + sum(c[j] * vn[j:j + 1, :] for j in range(t)) for x, c, vn in zip(o, qk_cols, v_new)]
    s_new = [so * jnp.exp(gl) + _mm3(x, vn, TN) for so, gl, x, vn in zip(s_old, g_last, kd, v_new)]
    for n, (b, h) in enumerate(items):
        lo = h * HEAD_DIM_A
        s_out_ref[0, b, h] = s_new[n]
        zz = z_ref[b, :, lo:lo + HEAD_DIM_A]
        o_ref[b, :, lo:lo + HEAD_DIM_A] = _rms(o[n], gn_ref[...]) * (zz * _sigmoid(zz))


def _delta_sample(x8, st8, ab, z, cw, nega, dtb, gn, s_all, s_prev, layer, t, nb=4):
    b = x8.shape[0]
    assert t + CONV_W - 1 <= SUBLANES
    aliased = s_prev is not None
    state_spec = pl.BlockSpec((1, nb, N_HEADS_A, HEAD_DIM_A, HEAD_DIM_A),
                              lambda i: (layer, i, 0, 0, 0))
    in_specs = [pl.BlockSpec((nb, SUBLANES, 3 * WIDTH_A), lambda i: (i, 0, 0)),
                pl.BlockSpec((1, nb, SUBLANES, 3 * WIDTH_A), lambda i: (layer, i, 0, 0)),
                pl.BlockSpec((nb, SUBLANES, LANES), lambda i: (i, 0, 0)),
                pl.BlockSpec((nb, SUBLANES, WIDTH_A), lambda i: (i, 0, 0)),
                pl.BlockSpec((CONV_W, 3 * WIDTH_A), lambda i: (0, 0)),
                pl.BlockSpec((1, LANES), lambda i: (0, 0)),
                pl.BlockSpec((1, LANES), lambda i: (0, 0)),
                pl.BlockSpec((1, HEAD_DIM_A), lambda i: (0, 0)),
                state_spec]
    args = [x8, st8, ab, z, cw, nega, dtb, gn, s_all]
    if aliased:
        in_specs.append(pl.BlockSpec(memory_space=pl.ANY))
        args.append(s_prev)
    return pl.pallas_call(
        functools.partial(_delta_sample_body, nb=nb, t=t, aliased=aliased),
        grid=(b // nb,),
        in_specs=in_specs,
        out_specs=[pl.BlockSpec((nb, SUBLANES, WIDTH_A), lambda i: (i, 0, 0)), state_spec],
        out_shape=[jax.ShapeDtypeStruct((b, SUBLANES, WIDTH_A), f32),
                   jax.ShapeDtypeStruct(s_all.shape, f32)],
        input_output_aliases={len(args) - 1: 1} if aliased else {},
        compiler_params=pltpu.CompilerParams(
            dimension_semantics=("arbitrary",), vmem_limit_bytes=VMEM_LIMIT),
        name="delta_sample",
    )(*args)


ATT_TB = 2048
Q_CHUNKS = WIDTH_G // LANES
KV_CHUNKS = 2 * WIDTH_G // LANES


def _attn_prompt_body(q_ref, kvh_ref, kvc_ref, o_ref, l_ref, q_scr, kv_scr, o_scr, l_scr, *,
                      dil, slopes, tb):
    i = pl.program_id(1)
    halo = N_OFF * dil
    for c in range(Q_CHUNKS):
        q_scr[c] = q_ref[0, :, c * LANES:(c + 1) * LANES]
    for c in range(KV_CHUNKS):
        kv_scr[c, pl.ds(0, halo), :] = kvh_ref[0, :, c * LANES:(c + 1) * LANES]
        kv_scr[c, pl.ds(halo, tb), :] = kvc_ref[0, :, c * LANES:(c + 1) * LANES]
    a = lax.broadcasted_iota(jnp.int32, (N_OFF, 2 * N_OFF), 0)
    cc = lax.broadcasted_iota(jnp.int32, (N_OFF, 2 * N_OFF), 1)
    delta = a - cc + N_OFF
    in_win = (delta >= 0) & (delta <= N_OFF)
    bias = [jnp.where(in_win, delta.astype(f32) * (-slopes[h] * dil), NEG)
            for h in range(HEADS_PER_GROUP)]
    upper = lax.broadcasted_iota(jnp.int32, (N_OFF, LANES), 1) >= HEAD_DIM_B
    n_sub = tb // halo

    def sub_block(idx, carry):
        r = idx // n_sub
        u = idx - r * n_sub
        base = r + halo * u
        key_tok = i * tb - halo + base + dil * cc
        in_seq = key_tok >= 0
        cs_ = range(Q_CHUNKS)
        hs = range(HEADS_PER_GROUP)
        qc = [q_scr[c, pl.ds(base, N_OFF, stride=dil), :] * HEAD_DIM_B ** -0.5 for c in cs_]
        kc = [kv_scr[c, pl.ds(base, 2 * N_OFF, stride=dil), :].astype(bf16) for c in cs_]
        vc = [kv_scr[Q_CHUNKS + c, pl.ds(base, 2 * N_OFF, stride=dil), :].astype(bf16) for c in cs_]
        qm = [jnp.where(upper if h % 2 else jnp.logical_not(upper), qc[h // 2], 0.0).astype(bf16)
              for h in hs]
        s = [jnp.where(in_seq, _dot(qm[h], kc[h // 2], NT) + bias[h], NEG) for h in hs]
        m = [jnp.max(s[h], axis=-1, keepdims=True) for h in hs]
        p = [jnp.exp(s[h] - m[h]) for h in hs]
        den = [jnp.sum(p[h], axis=-1, keepdims=True) for h in hs]
        pv = [_dot(p[h].astype(bf16), vc[h // 2]) / den[h] for h in hs]
        lse = [jnp.broadcast_to(m[h] + jnp.log(den[h]), (N_OFF, LANES)) for h in hs]
        for c in cs_:
            o_scr[c, pl.ds(base, N_OFF, stride=dil), :] = jnp.where(upper, pv[2 * c + 1], pv[2 * c])
            l_scr[c, pl.ds(base, N_OFF, stride=dil), :] = jnp.where(upper, lse[2 * c + 1], lse[2 * c])
        return carry

    lax.fori_loop(0, tb // N_OFF, sub_block, 0, unroll=2)
    for c in range(Q_CHUNKS):
        o_ref[0, :, c * LANES:(c + 1) * LANES] = o_scr[c]
        l_ref[0, :, c * LANES:(c + 1) * LANES] = l_scr[c]


def _attn_prompt(q, kv, g):
    b, t, _ = q.shape
    dil = DILATIONS[g]
    halo = N_OFF * dil
    tb = min(ATT_TB, t)
    assert tb % halo == 0 and t % tb == 0
    per = tb // halo
    return pl.pallas_call(
        functools.partial(_attn_prompt_body, dil=dil, slopes=_slopes(g), tb=tb),
        grid=(b, t // tb),
        in_specs=[pl.BlockSpec((1, tb, WIDTH_G), lambda bi, i: (bi, i, 0)),
                  pl.BlockSpec((1, halo, 2 * WIDTH_G),
                               lambda bi, i: (bi, jnp.maximum(i * per - 1, 0), 0)),
                  pl.BlockSpec((1, tb, 2 * WIDTH_G), lambda bi, i: (bi, i, 0))],
        out_specs=[pl.BlockSpec((1, tb, WIDTH_G), lambda bi, i: (bi, i, 0)),
                   pl.BlockSpec((1, tb, WIDTH_G), lambda bi, i: (bi, i, 0))],
        out_shape=[jax.ShapeDtypeStruct((b, t, WIDTH_G), f32),
                   jax.ShapeDtypeStruct((b, t, WIDTH_G), f32)],
        scratch_shapes=[pltpu.VMEM((Q_CHUNKS, tb, LANES), f32),
                        pltpu.VMEM((KV_CHUNKS, halo + tb, LANES), f32),
                        pltpu.VMEM((Q_CHUNKS, tb, LANES), f32),
                        pltpu.VMEM((Q_CHUNKS, tb, LANES), f32)],
        compiler_params=pltpu.CompilerParams(
            dimension_semantics=("arbitrary", "arbitrary"), vmem_limit_bytes=VMEM_LIMIT),
        name=f"attn_prompt_g{g}",
    )(q, kv, kv)


def _sample_consts(dil, t, win):
    rr = lax.broadcasted_iota(jnp.int32, (SUBLANES, win), 1)
    qi = lax.broadcasted_iota(jnp.int32, (SUBLANES, win), 0)
    nn = lax.broadcasted_iota(jnp.int32, (SUBLANES, SUBLANES), 1)
    qn = lax.broadcasted_iota(jnp.int32, (SUBLANES, SUBLANES), 0)
    lane = lax.broadcasted_iota(jnp.int32, (HEAD_DIM_B, LANES), 1)
    e_rows = lax.broadcasted_iota(jnp.int32, (SUBLANES, LANES), 0)
    e_lane = lax.broadcasted_iota(jnp.int32, (SUBLANES, LANES), 1)
    return dict(
        valid=((rr & (dil - 1)) == (qi & (dil - 1))) & (rr >= qi) & (qi < t),
        dist=(win + qi - rr).astype(f32),
        valid_n=(nn <= qn) & (((qn - nn) & (dil - 1)) == 0) & (qn < t),
        dist_n=(qn - nn).astype(f32),
        tail_lanes=lane >= LANES - t,
        real_rows=lax.broadcasted_iota(jnp.int32, (SUBLANES, HEAD_DIM_B), 0) < t,
        place=jnp.where((e_lane == e_rows + (LANES - t)) & (e_rows < t), 1.0, 0.0).astype(bf16))


def _sample_unit(c_in, c_out, q_b, kvn_b, o_b, l_b, cst, slopes, t, win):
    kvn = kvn_b[...]
    qs = q_b[...] * HEAD_DIM_B ** -0.5
    head = lambda a_, h, base=0: a_[:, base + h * HEAD_DIM_B:base + (h + 1) * HEAD_DIM_B]
    k3 = _split3(kvn)
    place = cst["place"]
    new_t = _dot(k3[0], place, TN) + _dot(k3[1], place, TN) + _dot(k3[2], place, TN)
    hs = range(HEADS_PER_GROUP)
    k_t = [c_in[0, h] for h in hs]
    v_t = [c_in[1, h] for h in hs]
    q8 = [head(qs, h).astype(bf16) for h in hs]
    s = [jnp.where(cst["valid"], _dot(q8[h], k_t[h].astype(bf16)) - slopes[h] * cst["dist"], NEG)
         for h in hs]
    sn = [jnp.where(cst["valid_n"],
                    _dot(q8[h], head(kvn, h).astype(bf16), NT) - slopes[h] * cst["dist_n"], NEG)
          for h in hs]
    m = [jnp.maximum(jnp.max(s[h], axis=-1, keepdims=True),
                     jnp.max(sn[h], axis=-1, keepdims=True)) for h in hs]
    p = [jnp.exp(s[h] - m[h]) for h in hs]
    pn = [jnp.exp(sn[h] - m[h]) for h in hs]
    den = [jnp.sum(p[h], axis=-1, keepdims=True) + jnp.sum(pn[h], axis=-1, keepdims=True)
           for h in hs]
    acc = [_dot(p[h].astype(bf16), v_t[h].astype(bf16), NT) + _mm1(pn[h], head(kvn, h, WIDTH_G))
           for h in hs]
    for h in hs:
        o_b[:, h * HEAD_DIM_B:(h + 1) * HEAD_DIM_B] = jnp.where(cst["real_rows"], acc[h] / den[h], 0.0)
        l_b[:, h * HEAD_DIM_B:(h + 1) * HEAD_DIM_B] = jnp.broadcast_to(
            m[h] + jnp.log(den[h]), (SUBLANES, HEAD_DIM_B))
        for kv, x in ((0, k_t[h]), (1, v_t[h])):
            rolled = pltpu.roll(x, win - t, axis=1)
            fresh = new_t[(kv * HEADS_PER_GROUP + h) * HEAD_DIM_B:
                          (kv * HEADS_PER_GROUP + h + 1) * HEAD_DIM_B, :]
            if win > LANES:
                c_out[kv, h, :, pl.ds(0, win - LANES)] = rolled[:, :win - LANES]
            c_out[kv, h, :, pl.ds(win - LANES, LANES)] = jnp.where(
                cst["tail_lanes"], fresh, rolled[:, win - LANES:])


def _attn_sample_body(*refs, dil, slopes, nb, t, win):
    q_ref, kvn_ref, c_ref = refs[:3]
    o_ref, l_ref, cn_ref = refs[-3:]
    cst = _sample_consts(dil, t, win)

    def one_sequence(b, carry):
        _sample_unit(c_ref.at[0, b], cn_ref.at[0, b], q_ref.at[b], kvn_ref.at[b], o_ref.at[b],
                     l_ref.at[b], cst, slopes, t, win)
        return carry

    lax.fori_loop(0, nb, one_sequence, 0, unroll=2 if nb % 2 == 0 else 1)


def _attn_sample(q, kvn, cache_t, prev, layer, g, t):
    bsz, win = cache_t.shape[1], cache_t.shape[-1]
    dil = DILATIONS[g]
    nb = max(1, min(16, 2048 // win))
    aliased = prev is not None
    cache_spec = pl.BlockSpec((1, nb, 2, HEADS_PER_GROUP, HEAD_DIM_B, win),
                              lambda i: (layer, i, 0, 0, 0, 0))
    out_spec = pl.BlockSpec((nb, SUBLANES, WIDTH_G), lambda i: (i, 0, 0))
    in_specs = [out_spec, pl.BlockSpec((nb, SUBLANES, 2 * WIDTH_G), lambda i: (i, 0, 0)), cache_spec]
    args = [q, kvn, cache_t]
    if aliased:
        in_specs.append(pl.BlockSpec(memory_space=pl.ANY))
        args.append(prev)
    o, l, cn = pl.pallas_call(
        functools.partial(_attn_sample_body, dil=dil, slopes=_slopes(g), nb=nb, t=t, win=win),
        grid=(bsz // nb,),
        in_specs=in_specs,
        out_specs=[out_spec, out_spec, cache_spec],
        out_shape=[jax.ShapeDtypeStruct((bsz, SUBLANES, WIDTH_G), f32),
                   jax.ShapeDtypeStruct((bsz, SUBLANES, WIDTH_G), f32),
                   jax.ShapeDtypeStruct(cache_t.shape, f32)],
        input_output_aliases={len(args) - 1: 2} if aliased else {},
        compiler_params=pltpu.CompilerParams(
            dimension_semantics=("arbitrary",), vmem_limit_bytes=VMEM_LIMIT),
        name=f"attn_sample_g{g}",
    )(*args)
    return o, l, cn


FF_CHUNK = 512


def _tail_front(x_ref, oa_ref, o_refs, l_refs, gate_ref, wa_ref, wb_ref, wo_ref, n2_ref):
    l0, l1, l2 = (r[...] for r in l_refs)
    lm = jnp.maximum(jnp.maximum(l0, l1), l2)
    e0, e1, e2 = jnp.exp(l0 - lm), jnp.exp(l1 - lm), jnp.exp(l2 - lm)
    den = e0 + e1 + e2
    ob = (e0 / den) * o_refs[0][...] + (e1 / den) * o_refs[1][...] + (e2 / den) * o_refs[2][...]
    ya = _mm1(oa_ref[...], wa_ref[...])
    yb = _mm1(ob, wb_ref[...])
    merged = (_sigmoid(gate_ref[:, :D_MODEL].astype(f32)) * ya
              + _sigmoid(gate_ref[:, D_MODEL:].astype(f32)) * yb)
    x1 = x_ref[...] + _mm1(merged, wo_ref[...])
    return x1, _rms(x1, n2_ref[...]).astype(bf16)


def _tail_ff(acc, h2, wu_ref, wd_ref, chunks):
    for j in chunks:
        up = _dot(h2, wu_ref[:, j * FF_CHUNK:(j + 1) * FF_CHUNK])
        act = jnp.square(jnp.maximum(up, 0.0)).astype(bf16)
        acc = acc + _dot(act, wd_ref[j * FF_CHUNK:(j + 1) * FF_CHUNK, :])
    return acc


def _tail_body(x_ref, oa_ref, o0_ref, o1_ref, o2_ref, l0_ref, l1_ref, l2_ref, gate_ref,
               wa_ref, wb_ref, wo_ref, n2_ref, wu_ref, wd_ref, fn_ref, y_ref, *, final):
    x1, h2 = _tail_front(x_ref, oa_ref, (o0_ref, o1_ref, o2_ref), (l0_ref, l1_ref, l2_ref),
                         gate_ref, wa_ref, wb_ref, wo_ref, n2_ref)
    acc = _tail_ff(x1, h2, wu_ref, wd_ref, range(D_FF // FF_CHUNK))
    if final:
        acc = _rms(acc, fn_ref[...])
    y_ref[...] = acc


def _tail_specs(tm):
    tok = lambda wd_: pl.BlockSpec((tm, wd_), lambda i: (i, 0))
    const = lambda shape: pl.BlockSpec(shape, lambda i: (0, 0), pipeline_mode=pl.Buffered(1))
    in_specs = [tok(D_MODEL), tok(WIDTH_A)] + [tok(WIDTH_G)] * 6 + [
        tok(2 * D_MODEL), const((WIDTH_A, D_MODEL)), const((WIDTH_G, D_MODEL)),
        const((D_MODEL, D_MODEL)), const((1, D_MODEL)), const((D_MODEL, D_FF)),
        const((D_FF, D_MODEL)), const((1, D_MODEL))]
    return in_specs, tok(D_MODEL)


def _tail(x, oa, outs, lses, gate, wa, wb, wo, n2, wu, wd, fn, final, tm=512):
    m = x.shape[0]
    tm = min(tm, m)
    assert m % tm == 0
    in_specs, out_spec = _tail_specs(tm)
    return pl.pallas_call(
        functools.partial(_tail_body, final=final),
        grid=(m // tm,),
        in_specs=in_specs,
        out_specs=out_spec,
        out_shape=jax.ShapeDtypeStruct((m, D_MODEL), f32),
        compiler_params=pltpu.CompilerParams(
            dimension_semantics=("arbitrary",), vmem_limit_bytes=VMEM_LIMIT),
        name="tail",
    )(x, oa, *outs, *lses, gate, wa, wb, wo, n2, wu, wd, fn)


FUSED_UNITS = 2
IN_SLOTS = 3
N_TAIL_IN = 16


def _tail_cache_body(*refs, final, dil, slopes, t, win, layer, aliased):
    (x_ref, oa_ref, o0_ref, o1_ref, o2_ref, l0_ref, l1_ref, l2_ref, gate_ref,
     wa_ref, wb_ref, wo_ref, n2_ref, wu_ref, wd_ref, fn_ref) = refs[:N_TAIL_IN]
    q_ref, kvn_ref, c_hbm = refs[N_TAIL_IN:N_TAIL_IN + 3]
    n_in = N_TAIL_IN + (4 if aliased else 3)
    y_ref, os_ref, ls_ref, cn_hbm = refs[n_in:n_in + 4]
    in_buf, out_buf, in_sem, out_sem = refs[n_in + 4:]
    i = pl.program_id(0)
    n = pl.num_programs(0)
    cst = _sample_consts(dil, t, win)

    def fetch(u, slot):
        return pltpu.make_async_copy(c_hbm.at[layer, u], in_buf.at[slot], in_sem.at[slot])

    def flush(u, slot):
        return pltpu.make_async_copy(out_buf.at[slot], cn_hbm.at[layer, u], out_sem.at[slot])

    total = n * FUSED_UNITS

    @pl.when(i == 0)
    def _():
        for u0 in range(IN_SLOTS - 1):
            fetch(u0, u0).start()

    n_ff = D_FF // FF_CHUNK
    first_part = (3 * n_ff) // 8
    acc = h2 = None
    for j in range(FUSED_UNITS):
        u = i * FUSED_UNITS + j
        slot = lax.rem(u, IN_SLOTS)
        fetch(u, slot).wait()
        ahead = u + IN_SLOTS - 1

        @pl.when(ahead < total)
        def _():
            fetch(ahead, lax.rem(ahead, IN_SLOTS)).start()

        @pl.when(i >= 1)
        def _():
            flush(u - FUSED_UNITS, j).wait()

        if j == 0:
            x1, h2 = _tail_front(x_ref, oa_ref, (o0_ref, o1_ref, o2_ref),
                                 (l0_ref, l1_ref, l2_ref), gate_ref, wa_ref, wb_ref, wo_ref, n2_ref)
            acc = _tail_ff(x1, h2, wu_ref, wd_ref, range(first_part))
        else:
            acc = _tail_ff(acc, h2, wu_ref, wd_ref, range(first_part, n_ff))
            if final:
                acc = _rms(acc, fn_ref[...])
            y_ref[...] = acc
        _sample_unit(in_buf.at[slot], out_buf.at[j], q_ref.at[j], kvn_ref.at[j], os_ref.at[j],
                     ls_ref.at[j], cst, slopes, t, win)
        flush(u, j).start()

    @pl.when(i == n - 1)
    def _():
        for j in range(FUSED_UNITS):
            flush(i * FUSED_UNITS + j, j).wait()


def _tail_with_cache(x, oa, outs, lses, gate, wa, wb, wo, n2, wu, wd, fn, final,
                     q, kvn, cache_t, prev, layer, g, t):
    m = x.shape[0]
    bsz, win = cache_t.shape[1], cache_t.shape[-1]
    assert bsz % FUSED_UNITS == 0 and m % (bsz // FUSED_UNITS) == 0 and bsz >= IN_SLOTS
    steps = bsz // FUSED_UNITS
    tm = m // steps
    assert tm % SUBLANES == 0
    in_specs, y_spec = _tail_specs(tm)
    o_spec = pl.BlockSpec((FUSED_UNITS, SUBLANES, WIDTH_G), lambda i: (i, 0, 0))
    hbm = pl.BlockSpec(memory_space=pl.ANY)
    in_specs = in_specs + [o_spec, pl.BlockSpec((FUSED_UNITS, SUBLANES, 2 * WIDTH_G),
                                                lambda i: (i, 0, 0)), hbm]
    args = [x, oa, *outs, *lses, gate, wa, wb, wo, n2, wu, wd, fn, q, kvn, cache_t]
    aliased = prev is not None
    if aliased:
        in_specs.append(hbm)
        args.append(prev)
    slab = (2, HEADS_PER_GROUP, HEAD_DIM_B, win)
    y, o, l, cn = pl.pallas_call(
        functools.partial(_tail_cache_body, final=final, dil=DILATIONS[g], slopes=_slopes(g), t=t,
                          win=win, layer=layer, aliased=aliased),
        grid=(steps,),
        in_specs=in_specs,
        out_specs=[y_spec, o_spec, o_spec, hbm],
        out_shape=[jax.ShapeDtypeStruct((m, D_MODEL), f32),
                   jax.ShapeDtypeStruct((bsz, SUBLANES, WIDTH_G), f32),
                   jax.ShapeDtypeStruct((bsz, SUBLANES, WIDTH_G), f32),
                   jax.ShapeDtypeStruct(cache_t.shape, f32)],
        scratch_shapes=[pltpu.VMEM((IN_SLOTS,) + slab, f32), pltpu.VMEM((FUSED_UNITS,) + slab, f32),
                        pltpu.SemaphoreType.DMA((IN_SLOTS,)),
                        pltpu.SemaphoreType.DMA((FUSED_UNITS,))],
        input_output_aliases={len(args) - 1: 3} if aliased else {},
        compiler_params=pltpu.CompilerParams(
            dimension_semantics=("arbitrary",), vmem_limit_bytes=VMEM_LIMIT),
        name="tail_cache",
    )(*args)
    return y, o, l, cn


def _lane_row(vals):
    return jnp.zeros((1, LANES), f32).at[0, :vals.shape[0]].set(vals.astype(f32))


def _layer_params(l, norm1, w_in, conv_w, a_log, dt_bias, gnorm_a, w_br_a, w_br_b, w_out, norm2,
                  w_up, w_down):
    return dict(
        n1=norm1[l].reshape(1, D_MODEL), w_in=_permute_w_in(w_in[l]), cw=conv_w[l],
        nega=_lane_row(-jnp.exp(a_log[l].astype(f32))), dtb=_lane_row(dt_bias[l]),
        gn=gnorm_a[l].reshape(1, HEAD_DIM_A).astype(f32),
        wa=w_br_a[l].astype(bf16), wb=w_br_b[l].astype(bf16), wo=w_out[l].astype(bf16),
        n2=norm2[l].reshape(1, D_MODEL), wu=w_up[l].astype(bf16), wd=w_down[l].astype(bf16))


def _layer(xp, xs, ts, p, fn, final, layer, caches_t, prev_caches, s_all, s_prev, conv8):
    bp, tp, _ = xp.shape
    xpf = xp.reshape(bp * tp, D_MODEL)
    cs, z, q0, q1, q2, kv0, kv1, kv2, gate_p, ab, last = _inproj(xpf, p["n1"], p["w_in"],
                                                                 conv_w=p["cw"], seq_len=tp)
    s0 = jnp.zeros((bp, N_HEADS_A, HEAD_DIM_A, HEAD_DIM_A), f32)
    oa_p, s_new_p = _delta_prompt(cs.reshape(bp, tp, 3 * WIDTH_A), ab.reshape(bp, tp, LANES),
                                  z.reshape(bp, tp, WIDTH_A), p["nega"], p["dtb"], p["gn"], s0)
    outs_p, lses_p, wins_p = [], [], []
    for g, (qg, kvg) in enumerate(((q0, kv0), (q1, kv1), (q2, kv2))):
        kv3 = kvg.reshape(bp, tp, 2 * WIDTH_G)
        o, l = _attn_prompt(qg.reshape(bp, tp, WIDTH_G), kv3, g)
        outs_p.append(o.reshape(bp * tp, WIDTH_G))
        lses_p.append(l.reshape(bp * tp, WIDTH_G))
        keep = min(WINDOWS[g], tp)
        wins_p.append(kv3[:, tp - keep:].reshape(bp, keep, 2, HEADS_PER_GROUP, HEAD_DIM_B))
    conv_p = last[:, SUBLANES - (CONV_W - 1):]

    bs = xs.shape[0]
    xsf = xs.reshape(bs * SUBLANES, D_MODEL)
    qkv, z, q0, q1, q2, kv0, kv1, kv2, gate_s, ab = _inproj(xsf, p["n1"], p["w_in"])
    rows8 = lambda a_: a_.reshape(bs, SUBLANES, a_.shape[-1])
    qkv8 = rows8(qkv)
    oa_s, s_stack = _delta_sample(qkv8, conv8, rows8(ab), rows8(z), p["cw"], p["nega"], p["dtb"],
                                  p["gn"], s_all, s_prev, layer, ts)
    outs_s, lses_s, new_caches = [], [], []
    for g, (qg, kvg) in enumerate(((q0, kv0), (q1, kv1))):
        o, l, cn = _attn_sample(rows8(qg), rows8(kvg), caches_t[g], prev_caches[g], layer, g, ts)
        outs_s.append(o.reshape(bs * SUBLANES, WIDTH_G))
        lses_s.append(l.reshape(bs * SUBLANES, WIDTH_G))
        new_caches.append(cn)

    g = N_GROUPS - 1
    yp, o, l, cn = _tail_with_cache(xpf, oa_p.reshape(bp * tp, WIDTH_A), outs_p, lses_p, gate_p,
                                    p["wa"], p["wb"], p["wo"], p["n2"], p["wu"], p["wd"], fn, final,
                                    rows8(q2), rows8(kv2), caches_t[g], prev_caches[g], layer, g, ts)
    outs_s.append(o.reshape(bs * SUBLANES, WIDTH_G))
    lses_s.append(l.reshape(bs * SUBLANES, WIDTH_G))
    new_caches.append(cn)
    ys = _tail(xsf, oa_s.reshape(bs * SUBLANES, WIDTH_A), outs_s, lses_s, gate_s, p["wa"], p["wb"],
               p["wo"], p["n2"], p["wu"], p["wd"], fn, final)
    conv_s = qkv8[:, ts - (CONV_W - 1):ts]
    return (yp.reshape(bp, tp, D_MODEL), ys.reshape(bs, SUBLANES, D_MODEL), wins_p, s_new_p, conv_p,
            new_caches, s_stack, conv_s)


def kernel(x_prompt, x_sample, cache_win0, cache_win1, cache_win2, state_delta, state_conv, norm1,
           w_in, conv_w, a_log, dt_bias, gnorm_a, w_br_a, w_br_b, w_out, norm2, w_up, w_down,
           final_norm):
    depth = w_in.shape[0]
    ts = x_sample.shape[1]
    assert ts + CONV_W - 1 <= SUBLANES
    pad_rows = lambda a_, axis: jnp.pad(
        a_, [(0, SUBLANES - a_.shape[axis]) if d == axis else (0, 0) for d in range(a_.ndim)])
    conv8 = pad_rows(state_conv.astype(f32), 2)
    fn = final_norm.reshape(1, D_MODEL)
    caches_t = [c.transpose(0, 1, 3, 4, 5, 2) for c in (cache_win0, cache_win1, cache_win2)]
    new_caches = [None] * N_GROUPS
    s_stack = None
    xp, xs = x_prompt, pad_rows(x_sample, 1)
    wins_p = [[] for _ in range(N_GROUPS)]
    delta_p, conv_p, conv_s = [], [], []
    for l in range(depth):
        p = _layer_params(l, norm1, w_in, conv_w, a_log, dt_bias, gnorm_a, w_br_a, w_br_b, w_out,
                          norm2, w_up, w_down)
        xp, xs, wp, sp, cp, new_caches, s_stack, cs = _layer(
            xp, xs, ts, p, fn, l == depth - 1, l, caches_t, new_caches, state_delta, s_stack, conv8)
        for g in range(N_GROUPS):
            wins_p[g].append(wp[g])
        delta_p.append(sp)
        conv_p.append(cp)
        conv_s.append(cs)
    st = jnp.stack
    wins_s = [c.transpose(0, 1, 5, 2, 3, 4) for c in new_caches]
    return (xp, xs[:, :ts], st(wins_p[0]), st(wins_p[1]), st(wins_p[2]), st(delta_p), st(conv_p),
            wins_s[0], wins_s[1], wins_s[2], s_stack, st(conv_s))
```

```python
import functools

import jax
import jax.numpy as jnp
from jax import lax
from jax.experimental import pallas as pl
from jax.experimental.pallas import tpu as pltpu

f32 = jnp.float32
bf16 = jnp.bfloat16

D_MODEL = 1024
N_HEADS_A = 4
HEAD_DIM_A = 128
WIDTH_A = N_HEADS_A * HEAD_DIM_A
CONV_W = 4
WINDOWS = (128, 512, 2048)
DILATIONS = (1, 4, 16)
N_GROUPS = 3
HEADS_PER_GROUP = 4
HEAD_DIM_B = 64
N_HEADS_B = N_GROUPS * HEADS_PER_GROUP
WIDTH_G = HEADS_PER_GROUP * HEAD_DIM_B
D_FF = 4 * D_MODEL
EPS = 1e-6
N_OFF = 128
NEG = -1e30

SUBLANES = 8
LANES = 128
VMEM_LIMIT = 56 * 1024 * 1024

SEG_WIDTHS = (3 * WIDTH_A, WIDTH_A, WIDTH_G, WIDTH_G, WIDTH_G,
              2 * WIDTH_G, 2 * WIDTH_G, 2 * WIDTH_G, 2 * D_MODEL, LANES)
N_PERM = sum(SEG_WIDTHS)
SEG_DTYPES = (f32,) * 8 + (bf16, f32)

NN = (((1,), (0,)), ((), ()))
NT = (((1,), (1,)), ((), ()))
TN = (((0,), (0,)), ((), ()))


def _slopes(g):
    return tuple(2.0 ** (-8.0 * (g * HEADS_PER_GROUP + h + 1) / N_HEADS_B)
                 for h in range(HEADS_PER_GROUP))


def _dot(a, b, dims=NN):
    return lax.dot_general(a, b, dims, preferred_element_type=f32)


def _mm1(a, b, dims=NN):
    return _dot(a.astype(bf16), b.astype(bf16), dims)


def _split(a):
    hi = a.astype(bf16)
    lo = (a - hi.astype(f32)).astype(bf16)
    return hi, lo


def _split3(a):
    hi = a.astype(bf16)
    r = a - hi.astype(f32)
    mid = r.astype(bf16)
    lo = (r - mid.astype(f32)).astype(bf16)
    return hi, mid, lo


def _mm3s(a, b, dims=NN):
    return _dot(a[0], b[0], dims) + _dot(a[0], b[1], dims) + _dot(a[1], b[0], dims)


def _mm3(a, b, dims=NN):
    return _mm3s(_split(a), _split(b), dims)


def _mm2s(a, b_hi, dims=NN):
    return _dot(a[0], b_hi, dims) + _dot(a[1], b_hi, dims)


def _sigmoid(x):
    return 1.0 / (1.0 + jnp.exp(-x))


def _softplus(x):
    return jnp.maximum(x, 0.0) + jnp.log1p(jnp.exp(-jnp.abs(x)))


def _rms(x, g):
    return x * lax.rsqrt(jnp.mean(x * x, axis=-1, keepdims=True) + EPS) * g


def _inproj_body(x_ref, g_ref, w_ref, *out_refs):
    h = _rms(x_ref[...], g_ref[...]).astype(bf16)
    off = 0
    for o_ref, width in zip(out_refs, SEG_WIDTHS):
        o_ref[...] = _dot(h, w_ref[:, off:off + width]).astype(o_ref.dtype)
        off += width


def _inproj_conv_body(x_ref, g_ref, w_ref, cw_ref, *refs, tm, tiles_per_seq):
    out_refs, last_ref, xs_ref = refs[:len(SEG_WIDTHS)], refs[-2], refs[-1]
    i = pl.program_id(0)

    @pl.when(lax.rem(i, tiles_per_seq) == 0)
    def _():
        xs_ref[pl.ds(0, SUBLANES), :] = jnp.zeros((SUBLANES, SEG_WIDTHS[0]), f32)

    h = _rms(x_ref[...], g_ref[...]).astype(bf16)
    xs_ref[pl.ds(SUBLANES, tm), :] = _dot(h, w_ref[:, :SEG_WIDTHS[0]])
    conv = xs_ref[pl.ds(SUBLANES - CONV_W + 1, tm), :] * cw_ref[0:1, :]
    for j in range(1, CONV_W):
        conv = conv + xs_ref[pl.ds(SUBLANES - CONV_W + 1 + j, tm), :] * cw_ref[j:j + 1, :]
    last = xs_ref[pl.ds(tm, SUBLANES), :]
    xs_ref[pl.ds(0, SUBLANES), :] = last
    last_ref[0] = last
    out_refs[0][...] = conv * _sigmoid(conv)
    off = SEG_WIDTHS[0]
    for o_ref, width in zip(out_refs[1:], SEG_WIDTHS[1:]):
        o_ref[...] = _dot(h, w_ref[:, off:off + width]).astype(o_ref.dtype)
        off += width


def _inproj(x, g, w, tm=512, conv_w=None, seq_len=None):
    m = x.shape[0]
    tm = min(tm, m)
    assert m % tm == 0
    in_specs = [pl.BlockSpec((tm, D_MODEL), lambda i: (i, 0)),
                pl.BlockSpec((1, D_MODEL), lambda i: (0, 0)),
                pl.BlockSpec((D_MODEL, N_PERM), lambda i: (0, 0), pipeline_mode=pl.Buffered(1))]
    out_specs = [pl.BlockSpec((tm, wd), lambda i: (i, 0)) for wd in SEG_WIDTHS]
    out_shape = [jax.ShapeDtypeStruct((m, wd), dt) for wd, dt in zip(SEG_WIDTHS, SEG_DTYPES)]
    args = [x, g, w]
    body, scratch = _inproj_body, []
    if conv_w is not None:
        assert seq_len % tm == 0 and tm >= SUBLANES
        per = seq_len // tm
        in_specs.append(pl.BlockSpec((CONV_W, SEG_WIDTHS[0]), lambda i: (0, 0)))
        out_specs.append(pl.BlockSpec((1, SUBLANES, SEG_WIDTHS[0]), lambda i: (i // per, 0, 0)))
        out_shape.append(jax.ShapeDtypeStruct((m // seq_len, SUBLANES, SEG_WIDTHS[0]), f32))
        args.append(conv_w)
        body = functools.partial(_inproj_conv_body, tm=tm, tiles_per_seq=per)
        scratch = [pltpu.VMEM((tm + SUBLANES, SEG_WIDTHS[0]), f32)]
    return pl.pallas_call(
        body,
        grid=(m // tm,),
        in_specs=in_specs,
        out_specs=out_specs,
        out_shape=out_shape,
        scratch_shapes=scratch,
        compiler_params=pltpu.CompilerParams(
            dimension_semantics=("arbitrary",), vmem_limit_bytes=VMEM_LIMIT),
        name="inproj",
    )(*args)


def _permute_w_in(w):
    o_z = 3 * WIDTH_A
    o_ab = o_z + WIDTH_A
    o_q = o_ab + 2 * N_HEADS_A
    o_k = o_q + N_GROUPS * WIDTH_G
    o_v = o_k + N_GROUPS * WIDTH_G
    o_gate = o_v + N_GROUPS * WIDTH_G
    parts = [w[:, :o_ab], w[:, o_q:o_k]]
    for g in range(N_GROUPS):
        parts.append(w[:, o_k + g * WIDTH_G:o_k + (g + 1) * WIDTH_G])
        parts.append(w[:, o_v + g * WIDTH_G:o_v + (g + 1) * WIDTH_G])
    parts.append(w[:, o_gate:])
    parts.append(w[:, o_ab:o_q])
    parts.append(jnp.zeros((w.shape[0], LANES - 2 * N_HEADS_A), w.dtype))
    return jnp.concatenate(parts, axis=1).astype(bf16)


CHUNK = 128
INV_BASE = 16
HEADS = tuple(range(N_HEADS_A))


def _tri_inverse_heads(a_list, row, col):
    eye = jnp.where(row == col, 1.0, 0.0)
    blk = lambda idx, size: jnp.right_shift(idx, size.bit_length() - 1)
    same16 = blk(row, INV_BASE) == blk(col, INV_BASE)
    ad = [jnp.where(same16, a, 0.0) for a in a_list]
    p = [eye - x for x in ad]
    xs = [_split(x) for x in ad]
    x = [_mm3s(s, s) for s in xs]
    xs = [_split(v) for v in x]
    p = [pv + _mm3s(_split(pv), xx) for pv, xx in zip(p, xs)]
    xb = [s[0] for s in xs]
    n = 4
    while n < INV_BASE:
        xb = [_dot(v, v).astype(bf16) for v in xb]
        p = [pv + _dot(pv.astype(bf16), xx) for pv, xx in zip(p, xb)]
        n *= 2
    s = INV_BASE
    while s < CHUNK:
        pair = (blk(row, 2 * s) == blk(col, 2 * s)) & (blk(row, s) != blk(col, s))
        es = [jnp.where(pair, a, 0.0).astype(bf16) for a in a_list]
        ps = [_split(v) for v in p]
        pe = [_split(_mm2s(pp, ee)) for pp, ee in zip(ps, es)]
        p = [pv - _mm3s(x, pp) for pv, x, pp in zip(p, pe, ps)]
        s *= 2
    return [_split(v) for v in p]


def _delta_prompt_body(cs_ref, ab_ref, z_ref, nega_ref, dtb_ref, gn_ref, s0_ref,
                       o_ref, s_out_ref, s_ref, *, nchunk):
    c = pl.program_id(1)

    @pl.when(c == 0)
    def _():
        s_ref[...] = s0_ref[0]

    items = [(ck, h) for ck in range(nchunk) for h in HEADS]
    rows = lambda ck: pl.ds(ck * CHUNK, CHUNK)
    log_a = [nega_ref[...] * _softplus(ab_ref[0, rows(ck), :] + dtb_ref[...])
             for ck in range(nchunk)]
    beta_all = [_sigmoid(ab_ref[0, rows(ck), :]) for ck in range(nchunk)]

    row = lax.broadcasted_iota(jnp.int32, (CHUNK, CHUNK), 0)
    col = lax.broadcasted_iota(jnp.int32, (CHUNK, CHUNK), 1)
    causal = row >= col
    strict = row > col
    ltri = jnp.where(causal, 1.0, 0.0).astype(bf16)

    def l2n(x, scale):
        return x * (lax.rsqrt(jnp.sum(x * x, axis=-1, keepdims=True) + EPS) * scale)

    def part(ck, h, which):
        lo = which * WIDTH_A + h * HEAD_DIM_A
        return cs_ref[0, rows(ck), lo:lo + HEAD_DIM_A]

    q = [l2n(part(ck, h, 0), HEAD_DIM_A ** -0.5) for ck, h in items]
    k = [l2n(part(ck, h, 1), 1.0) for ck, h in items]
    v = [part(ck, h, 2) for ck, h in items]
    beta = [beta_all[ck][:, N_HEADS_A + h:N_HEADS_A + h + 1] for ck, h in items]
    g3 = [_split3(jnp.broadcast_to(log_a[ck][:, h:h + 1], (CHUNK, HEAD_DIM_A))) for ck, h in items]
    cum = [_dot(ltri, t[0]) + _dot(ltri, t[1]) + _dot(ltri, t[2]) for t in g3]
    decay = [jnp.exp(jnp.where(causal, x - x.T, NEG)) for x in cum]
    e_g = [jnp.exp(x) for x in cum]
    g_last = [x[CHUNK - 1:CHUNK, :] for x in cum]
    kb = [a * b for a, b in zip(k, beta)]
    k_bf = [x.astype(bf16) for x in k]
    a_mat = [jnp.where(strict, _dot(x.astype(bf16), y, NT) * d, 0.0)
             for x, y, d in zip(kb, k_bf, decay)]
    t_inv = _tri_inverse_heads(a_mat, row, col)
    w = [_mm2s(t, (x * e).astype(bf16)) for t, x, e in zip(t_inv, kb, e_g)]
    u = [_mm2s(t, (x * b).astype(bf16)) for t, x, b in zip(t_inv, v, beta)]
    qk = [_dot(x.astype(bf16), y, NT) * d for x, y, d in zip(q, k_bf, decay)]
    qg = [x * e for x, e in zip(q, e_g)]
    kd = [x * jnp.exp(gl - cm) for x, gl, cm in zip(k, g_last, cum)]
    s_cur = [s_ref[h] for h in HEADS]
    for ck in range(nchunk):
        n0 = ck * N_HEADS_A
        s_sp = [_split(x) for x in s_cur]
        v_new = [u[n0 + h] - _mm3s(_split(w[n0 + h]), s_sp[h]) for h in HEADS]
        o = [_dot(qg[n0 + h].astype(bf16), s_sp[h][0]) + _mm1(qk[n0 + h], v_new[h]) for h in HEADS]
        s_cur = [s_cur[h] * jnp.exp(g_last[n0 + h]) + _mm3(kd[n0 + h], v_new[h], TN) for h in HEADS]
        for h in HEADS:
            lo = h * HEAD_DIM_A
            zz = z_ref[0, rows(ck), lo:lo + HEAD_DIM_A]
            o_ref[0, rows(ck), lo:lo + HEAD_DIM_A] = _rms(o[h], gn_ref[...]) * (zz * _sigmoid(zz))
    for h in HEADS:
        s_ref[h] = s_cur[h]

    @pl.when(c == pl.num_programs(1) - 1)
    def _():
        s_out_ref[0] = s_ref[...]


DELTA_CHUNKS_PER_STEP = 2


def _delta_prompt(cs, ab, z, nega, dtb, gn, s0):
    b, t, _ = cs.shape
    nchunk = DELTA_CHUNKS_PER_STEP if t % (DELTA_CHUNKS_PER_STEP * CHUNK) == 0 else 1
    tb = nchunk * CHUNK
    assert t % tb == 0
    return pl.pallas_call(
        functools.partial(_delta_prompt_body, nchunk=nchunk),
        grid=(b, t // tb),
        in_specs=[pl.BlockSpec((1, tb, 3 * WIDTH_A), lambda i, c: (i, c, 0)),
                  pl.BlockSpec((1, tb, LANES), lambda i, c: (i, c, 0)),
                  pl.BlockSpec((1, tb, WIDTH_A), lambda i, c: (i, c, 0)),
                  pl.BlockSpec((1, LANES), lambda i, c: (0, 0)),
                  pl.BlockSpec((1, LANES), lambda i, c: (0, 0)),
                  pl.BlockSpec((1, HEAD_DIM_A), lambda i, c: (0, 0)),
                  pl.BlockSpec((1, N_HEADS_A, HEAD_DIM_A, HEAD_DIM_A), lambda i, c: (i, 0, 0, 0))],
        out_specs=[pl.BlockSpec((1, tb, WIDTH_A), lambda i, c: (i, c, 0)),
                   pl.BlockSpec((1, N_HEADS_A, HEAD_DIM_A, HEAD_DIM_A), lambda i, c: (i, 0, 0, 0))],
        out_shape=[jax.ShapeDtypeStruct((b, t, WIDTH_A), f32),
                   jax.ShapeDtypeStruct((b, N_HEADS_A, HEAD_DIM_A, HEAD_DIM_A), f32)],
        scratch_shapes=[pltpu.VMEM((N_HEADS_A, HEAD_DIM_A, HEAD_DIM_A), f32)],
        compiler_params=pltpu.CompilerParams(
            dimension_semantics=("arbitrary", "arbitrary"), vmem_limit_bytes=VMEM_LIMIT),
        name="delta_prompt",
    )(cs, ab, z, nega, dtb, gn, s0)


def _delta_sample_body(*refs, nb, t, aliased):
    x_ref, st_ref, ab_ref, z_ref, cw_ref, nega_ref, dtb_ref, gn_ref, s0_ref = refs[:9]
    o_ref, s_out_ref = refs[-2:]
    rowi = lax.broadcasted_iota(jnp.int32, (SUBLANES, LANES), 0)
    real = rowi < t
    seqs = range(nb)
    items = [(b, h) for b in seqs for h in HEADS]
    conv = [sum((pltpu.roll(x_ref[b], CONV_W - 1 - j, axis=0)
                 + pltpu.roll(st_ref[0, b], (SUBLANES - j) % SUBLANES, axis=0)) * cw_ref[j:j + 1, :]
                for j in range(CONV_W)) for b in seqs]
    cs = [x * _sigmoid(x) for x in conv]
    log_a = [jnp.where(real, nega_ref[...] * _softplus(ab_ref[b] + dtb_ref[...]), 0.0) for b in seqs]
    beta_all = [jnp.where(real, _sigmoid(ab_ref[b]), 0.0) for b in seqs]

    def part(b, h, which):
        lo = which * WIDTH_A + h * HEAD_DIM_A
        return cs[b][:, lo:lo + HEAD_DIM_A]

    def l2n(x, scale):
        return x * (lax.rsqrt(jnp.sum(x * x, axis=-1, keepdims=True) + EPS) * scale)

    q = [l2n(part(b, h, 0), HEAD_DIM_A ** -0.5) for b, h in items]
    k = [l2n(part(b, h, 1), 1.0) for b, h in items]
    v = [part(b, h, 2) for b, h in items]
    beta = [beta_all[b][:, N_HEADS_A + h:N_HEADS_A + h + 1] for b, h in items]
    gb = [jnp.broadcast_to(log_a[b][:, h:h + 1], (SUBLANES, LANES)) for b, h in items]
    cum = [sum(jnp.where(rowi >= j, x[j:j + 1, :], 0.0) for j in range(t)) for x in gb]
    g_last = [x[t - 1:t, :] for x in cum]
    e_g = [jnp.exp(x) for x in cum]
    kb = [x * y for x, y in zip(k, beta)]
    w = [x * y for x, y in zip(kb, e_g)]
    u = [x * y for x, y in zip(v, beta)]
    dec = [[jnp.exp(jnp.where(rowi >= j, x - x[j:j + 1, :], NEG)) for j in range(t)] for x in cum]
    qk_cols = [[jnp.sum(q[n] * k[n][j:j + 1, :], axis=-1, keepdims=True) * dec[n][j]
                for j in range(t)] for n in range(len(items))]
    a_cols = [[jnp.where(rowi > j, jnp.sum(kb[n] * k[n][j:j + 1, :], axis=-1, keepdims=True)
                         * dec[n][j], 0.0) for j in range(t - 1)] for n in range(len(items))]
    for j in range(t - 1):
        w = [x - a[j] * x[j:j + 1, :] for x, a in zip(w, a_cols)]
        u = [x - a[j] * x[j:j + 1, :] for x, a in zip(u, a_cols)]
    qg = [x * y for x, y in zip(q, e_g)]
    kd = [x * jnp.exp(gl - cm) for x, gl, cm in zip(k, g_last, cum)]
    s_old = [s0_ref[0, b, h] for b, h in items]
    s_sp = [_split(x) for x in s_old]
    v_new = [uu - _mm3s(_split(ww), ss) for uu, ww, ss in zip(u, w, s_sp)]
    o = [_mm3s(_split(x), ss) for x, ss in zip(qg, s_sp)]
    o = [x + sum(c[j] * vn[j:j + 1, :] for j in range(t)) for x, c, vn in zip(o, qk_cols, v_new)]
    s_new = [so * jnp.exp(gl) + _mm3(x, vn, TN) for so, gl, x, vn in zip(s_old, g_last, kd, v_new)]
    for n, (b, h) in enumerate(items):
        lo = h * HEAD_DIM_A
        s_out_ref[0, b, h] = s_new[n]
        zz = z_ref[b, :, lo:lo + HEAD_DIM_A]
        o_ref[b, :, lo:lo + HEAD_DIM_A] = _rms(o[n], gn_ref[...]) * (zz * _sigmoid(zz))


def _delta_sample(x8, st8, ab, z, cw, nega, dtb, gn, s_all, s_prev, layer, t, nb=4):
    b = x8.shape[0]
    assert t + CONV_W - 1 <= SUBLANES
    aliased = s_prev is not None
    state_spec = pl.BlockSpec((1, nb, N_HEADS_A, HEAD_DIM_A, HEAD_DIM_A),
                              lambda i: (layer, i, 0, 0, 0))
    in_specs = [pl.BlockSpec((nb, SUBLANES, 3 * WIDTH_A), lambda i: (i, 0, 0)),
                pl.BlockSpec((1, nb, SUBLANES, 3 * WIDTH_A), lambda i: (layer, i, 0, 0)),
                pl.BlockSpec((nb, SUBLANES, LANES), lambda i: (i, 0, 0)),
                pl.BlockSpec((nb, SUBLANES, WIDTH_A), lambda i: (i, 0, 0)),
                pl.BlockSpec((CONV_W, 3 * WIDTH_A), lambda i: (0, 0)),
                pl.BlockSpec((1, LANES), lambda i: (0, 0)),
                pl.BlockSpec((1, LANES), lambda i: (0, 0)),
                pl.BlockSpec((1, HEAD_DIM_A), lambda i: (0, 0)),
                state_spec]
    args = [x8, st8, ab, z, cw, nega, dtb, gn, s_all]
    if aliased:
        in_specs.append(pl.BlockSpec(memory_space=pl.ANY))
        args.append(s_prev)
    return pl.pallas_call(
        functools.partial(_delta_sample_body, nb=nb, t=t, aliased=aliased),
        grid=(b // nb,),
        in_specs=in_specs,
        out_specs=[pl.BlockSpec((nb, SUBLANES, WIDTH_A), lambda i: (i, 0, 0)), state_spec],
        out_shape=[jax.ShapeDtypeStruct((b, SUBLANES, WIDTH_A), f32),
                   jax.ShapeDtypeStruct(s_all.shape, f32)],
        input_output_aliases={len(args) - 1: 1} if aliased else {},
        compiler_params=pltpu.CompilerParams(
            dimension_semantics=("arbitrary",), vmem_limit_bytes=VMEM_LIMIT),
        name="delta_sample",
    )(*args)


ATT_TB = 2048
Q_CHUNKS = WIDTH_G // LANES
KV_CHUNKS = 2 * WIDTH_G // LANES


def _attn_prompt_body(q_ref, kvh_ref, kvc_ref, o_ref, l_ref, q_scr, kv_scr, o_scr, l_scr, *,
                      dil, slopes, tb):
    i = pl.program_id(1)
    halo = N_OFF * dil
    for c in range(Q_CHUNKS):
        q_scr[c] = q_ref[0, :, c * LANES:(c + 1) * LANES]
    for c in range(KV_CHUNKS):
        kv_scr[c, pl.ds(0, halo), :] = kvh_ref[0, :, c * LANES:(c + 1) * LANES]
        kv_scr[c, pl.ds(halo, tb), :] = kvc_ref[0, :, c * LANES:(c + 1) * LANES]
    a = lax.broadcasted_iota(jnp.int32, (N_OFF, 2 * N_OFF), 0)
    cc = lax.broadcasted_iota(jnp.int32, (N_OFF, 2 * N_OFF), 1)
    delta = a - cc + N_OFF
    in_win = (delta >= 0) & (delta <= N_OFF)
    bias = [jnp.where(in_win, delta.astype(f32) * (-slopes[h] * dil), NEG)
            for h in range(HEADS_PER_GROUP)]
    upper = lax.broadcasted_iota(jnp.int32, (N_OFF, LANES), 1) >= HEAD_DIM_B
    n_sub = tb // halo

    def sub_block(idx, carry):
        r = idx // n_sub
        u = idx - r * n_sub
        base = r + halo * u
        key_tok = i * tb - halo + base + dil * cc
        in_seq = key_tok >= 0
        cs_ = range(Q_CHUNKS)
        hs = range(HEADS_PER_GROUP)
        qc = [q_scr[c, pl.ds(base, N_OFF, stride=dil), :] * HEAD_DIM_B ** -0.5 for c in cs_]
        kc = [kv_scr[c, pl.ds(base, 2 * N_OFF, stride=dil), :].astype(bf16) for c in cs_]
        vc = [kv_scr[Q_CHUNKS + c, pl.ds(base, 2 * N_OFF, stride=dil), :].astype(bf16) for c in cs_]
        qm = [jnp.where(upper if h % 2 else jnp.logical_not(upper), qc[h // 2], 0.0).astype(bf16)
              for h in hs]
        s = [jnp.where(in_seq, _dot(qm[h], kc[h // 2], NT) + bias[h], NEG) for h in hs]
        m = [jnp.max(s[h], axis=-1, keepdims=True) for h in hs]
        p = [jnp.exp(s[h] - m[h]) for h in hs]
        den = [jnp.sum(p[h], axis=-1, keepdims=True) for h in hs]
        pv = [_dot(p[h].astype(bf16), vc[h // 2]) / den[h] for h in hs]
        lse = [jnp.broadcast_to(m[h] + jnp.log(den[h]), (N_OFF, LANES)) for h in hs]
        for c in cs_:
            o_scr[c, pl.ds(base, N_OFF, stride=dil), :] = jnp.where(upper, pv[2 * c + 1], pv[2 * c])
            l_scr[c, pl.ds(base, N_OFF, stride=dil), :] = jnp.where(upper, lse[2 * c + 1], lse[2 * c])
        return carry

    lax.fori_loop(0, tb // N_OFF, sub_block, 0, unroll=2)
    for c in range(Q_CHUNKS):
        o_ref[0, :, c * LANES:(c + 1) * LANES] = o_scr[c]
        l_ref[0, :, c * LANES:(c + 1) * LANES] = l_scr[c]


def _attn_prompt(q, kv, g):
    b, t, _ = q.shape
    dil = DILATIONS[g]
    halo = N_OFF * dil
    tb = min(ATT_TB, t)
    assert tb % halo == 0 and t % tb == 0
    per = tb // halo
    return pl.pallas_call(
        functools.partial(_attn_prompt_body, dil=dil, slopes=_slopes(g), tb=tb),
        grid=(b, t // tb),
        in_specs=[pl.BlockSpec((1, tb, WIDTH_G), lambda bi, i: (bi, i, 0)),
                  pl.BlockSpec((1, halo, 2 * WIDTH_G),
                               lambda bi, i: (bi, jnp.maximum(i * per - 1, 0), 0)),
                  pl.BlockSpec((1, tb, 2 * WIDTH_G), lambda bi, i: (bi, i, 0))],
        out_specs=[pl.BlockSpec((1, tb, WIDTH_G), lambda bi, i: (bi, i, 0)),
                   pl.BlockSpec((1, tb, WIDTH_G), lambda bi, i: (bi, i, 0))],
        out_shape=[jax.ShapeDtypeStruct((b, t, WIDTH_G), f32),
                   jax.ShapeDtypeStruct((b, t, WIDTH_G), f32)],
        scratch_shapes=[pltpu.VMEM((Q_CHUNKS, tb, LANES), f32),
                        pltpu.VMEM((KV_CHUNKS, halo + tb, LANES), f32),
                        pltpu.VMEM((Q_CHUNKS, tb, LANES), f32),
                        pltpu.VMEM((Q_CHUNKS, tb, LANES), f32)],
        compiler_params=pltpu.CompilerParams(
            dimension_semantics=("arbitrary", "arbitrary"), vmem_limit_bytes=VMEM_LIMIT),
        name=f"attn_prompt_g{g}",
    )(q, kv, kv)


def _sample_consts(dil, t, win):
    rr = lax.broadcasted_iota(jnp.int32, (SUBLANES, win), 1)
    qi = lax.broadcasted_iota(jnp.int32, (SUBLANES, win), 0)
    nn = lax.broadcasted_iota(jnp.int32, (SUBLANES, SUBLANES), 1)
    qn = lax.broadcasted_iota(jnp.int32, (SUBLANES, SUBLANES), 0)
    lane = lax.broadcasted_iota(jnp.int32, (HEAD_DIM_B, LANES), 1)
    e_rows = lax.broadcasted_iota(jnp.int32, (SUBLANES, LANES), 0)
    e_lane = lax.broadcasted_iota(jnp.int32, (SUBLANES, LANES), 1)
    return dict(
        valid=((rr & (dil - 1)) == (qi & (dil - 1))) & (rr >= qi) & (qi < t),
        dist=(win + qi - rr).astype(f32),
        valid_n=(nn <= qn) & (((qn - nn) & (dil - 1)) == 0) & (qn < t),
        dist_n=(qn - nn).astype(f32),
        tail_lanes=lane >= LANES - t,
        real_rows=lax.broadcasted_iota(jnp.int32, (SUBLANES, HEAD_DIM_B), 0) < t,
        place=jnp.where((e_lane == e_rows + (LANES - t)) & (e_rows < t), 1.0, 0.0).astype(bf16))


def _sample_unit(c_in, c_out, q_b, kvn_b, o_b, l_b, cst, slopes, t, win):
    kvn = kvn_b[...]
    qs = q_b[...] * HEAD_DIM_B ** -0.5
    head = lambda a_, h, base=0: a_[:, base + h * HEAD_DIM_B:base + (h + 1) * HEAD_DIM_B]
    k3 = _split3(kvn)
    place = cst["place"]
    new_t = _dot(k3[0], place, TN) + _dot(k3[1], place, TN) + _dot(k3[2], place, TN)
    hs = range(HEADS_PER_GROUP)
    k_t = [c_in[0, h] for h in hs]
    v_t = [c_in[1, h] for h in hs]
    q8 = [head(qs, h).astype(bf16) for h in hs]
    s = [jnp.where(cst["valid"], _dot(q8[h], k_t[h].astype(bf16)) - slopes[h] * cst["dist"], NEG)
         for h in hs]
    sn = [jnp.where(cst["valid_n"],
                    _dot(q8[h], head(kvn, h).astype(bf16), NT) - slopes[h] * cst["dist_n"], NEG)
          for h in hs]
    m = [jnp.maximum(jnp.max(s[h], axis=-1, keepdims=True),
                     jnp.max(sn[h], axis=-1, keepdims=True)) for h in hs]
    p = [jnp.exp(s[h] - m[h]) for h in hs]
    pn = [jnp.exp(sn[h] - m[h]) for h in hs]
    den = [jnp.sum(p[h], axis=-1, keepdims=True) + jnp.sum(pn[h], axis=-1, keepdims=True)
           for h in hs]
    acc = [_dot(p[h].astype(bf16), v_t[h].astype(bf16), NT) + _mm1(pn[h], head(kvn, h, WIDTH_G))
           for h in hs]
    for h in hs:
        o_b[:, h * HEAD_DIM_B:(h + 1) * HEAD_DIM_B] = jnp.where(cst["real_rows"], acc[h] / den[h], 0.0)
        l_b[:, h * HEAD_DIM_B:(h + 1) * HEAD_DIM_B] = jnp.broadcast_to(
            m[h] + jnp.log(den[h]), (SUBLANES, HEAD_DIM_B))
        for kv, x in ((0, k_t[h]), (1, v_t[h])):
            rolled = pltpu.roll(x, win - t, axis=1)
            fresh = new_t[(kv * HEADS_PER_GROUP + h) * HEAD_DIM_B:
                          (kv * HEADS_PER_GROUP + h + 1) * HEAD_DIM_B, :]
            if win > LANES:
                c_out[kv, h, :, pl.ds(0, win - LANES)] = rolled[:, :win - LANES]
            c_out[kv, h, :, pl.ds(win - LANES, LANES)] = jnp.where(
                cst["tail_lanes"], fresh, rolled[:, win - LANES:])


def _attn_sample_body(*refs, dil, slopes, nb, t, win):
    q_ref, kvn_ref, c_ref = refs[:3]
    o_ref, l_ref, cn_ref = refs[-3:]
    cst = _sample_consts(dil, t, win)

    def one_sequence(b, carry):
        _sample_unit(c_ref.at[0, b], cn_ref.at[0, b], q_ref.at[b], kvn_ref.at[b], o_ref.at[b],
                     l_ref.at[b], cst, slopes, t, win)
        return carry

    lax.fori_loop(0, nb, one_sequence, 0, unroll=2 if nb % 2 == 0 else 1)


def _attn_sample(q, kvn, cache_t, prev, layer, g, t):
    bsz, win = cache_t.shape[1], cache_t.shape[-1]
    dil = DILATIONS[g]
    nb = max(1, min(16, 2048 // win))
    aliased = prev is not None
    cache_spec = pl.BlockSpec((1, nb, 2, HEADS_PER_GROUP, HEAD_DIM_B, win),
                              lambda i: (layer, i, 0, 0, 0, 0))
    out_spec = pl.BlockSpec((nb, SUBLANES, WIDTH_G), lambda i: (i, 0, 0))
    in_specs = [out_spec, pl.BlockSpec((nb, SUBLANES, 2 * WIDTH_G), lambda i: (i, 0, 0)), cache_spec]
    args = [q, kvn, cache_t]
    if aliased:
        in_specs.append(pl.BlockSpec(memory_space=pl.ANY))
        args.append(prev)
    o, l, cn = pl.pallas_call(
        functools.partial(_attn_sample_body, dil=dil, slopes=_slopes(g), nb=nb, t=t, win=win),
        grid=(bsz // nb,),
        in_specs=in_specs,
        out_specs=[out_spec, out_spec, cache_spec],
        out_shape=[jax.ShapeDtypeStruct((bsz, SUBLANES, WIDTH_G), f32),
                   jax.ShapeDtypeStruct((bsz, SUBLANES, WIDTH_G), f32),
                   jax.ShapeDtypeStruct(cache_t.shape, f32)],
        input_output_aliases={len(args) - 1: 2} if aliased else {},
        compiler_params=pltpu.CompilerParams(
            dimension_semantics=("arbitrary",), vmem_limit_bytes=VMEM_LIMIT),
        name=f"attn_sample_g{g}",
    )(*args)
    return o, l, cn


class _CachePipe:
    def __init__(self, c_hbm, cn_hbm, in_buf, out_buf, in_sem, out_sem, layer, units):
        self.c_hbm, self.cn_hbm, self.in_buf, self.out_buf = c_hbm, cn_hbm, in_buf, out_buf
        self.in_sem, self.out_sem, self.layer, self.units = in_sem, out_sem, layer, units
        self.in_slots = in_buf.shape[0]
        self.i = pl.program_id(0)
        self.n = pl.num_programs(0)

    def _fetch(self, u, slot):
        return pltpu.make_async_copy(self.c_hbm.at[self.layer, u], self.in_buf.at[slot],
                                     self.in_sem.at[slot])

    def _flush(self, u, slot):
        return pltpu.make_async_copy(self.out_buf.at[slot], self.cn_hbm.at[self.layer, u],
                                     self.out_sem.at[slot])

    def prologue(self):
        @pl.when(self.i == 0)
        def _():
            for u0 in range(self.in_slots - 1):
                self._fetch(u0, u0).start()

    def open(self, j):
        u = self.i * self.units + j
        slot = lax.rem(u, self.in_slots)
        self._fetch(u, slot).wait()
        ahead = u + self.in_slots - 1

        @pl.when(ahead < self.n * self.units)
        def _():
            self._fetch(ahead, lax.rem(ahead, self.in_slots)).start()

        @pl.when(self.i >= 1)
        def _():
            self._flush(u - self.units, j).wait()

        return self.in_buf.at[slot], self.out_buf.at[j]

    def close(self, j):
        self._flush(self.i * self.units + j, j).start()

    def epilogue(self):
        @pl.when(self.i == self.n - 1)
        def _():
            for j in range(self.units):
                self._flush(self.i * self.units + j, j).wait()

    @staticmethod
    def scratch(units, in_slots, win):
        slab = (2, HEADS_PER_GROUP, HEAD_DIM_B, win)
        return [pltpu.VMEM((in_slots,) + slab, f32), pltpu.VMEM((units,) + slab, f32),
                pltpu.SemaphoreType.DMA((in_slots,)), pltpu.SemaphoreType.DMA((units,))]


FF_CHUNK = 512


def _tail_front(x_ref, oa_ref, o_refs, l_refs, gate_ref, wa_ref, wb_ref, wo_ref, n2_ref):
    l0, l1, l2 = (r[...] for r in l_refs)
    lm = jnp.maximum(jnp.maximum(l0, l1), l2)
    e0, e1, e2 = jnp.exp(l0 - lm), jnp.exp(l1 - lm), jnp.exp(l2 - lm)
    den = e0 + e1 + e2
    ob = (e0 / den) * o_refs[0][...] + (e1 / den) * o_refs[1][...] + (e2 / den) * o_refs[2][...]
    ya = _mm1(oa_ref[...], wa_ref[...])
    yb = _mm1(ob, wb_ref[...])
    merged = (_sigmoid(gate_ref[:, :D_MODEL].astype(f32)) * ya
              + _sigmoid(gate_ref[:, D_MODEL:].astype(f32)) * yb)
    x1 = x_ref[...] + _mm1(merged, wo_ref[...])
    return x1, _rms(x1, n2_ref[...]).astype(bf16)


def _tail_ff(acc, h2, wu_ref, wd_ref, chunks):
    for j in chunks:
        up = _dot(h2, wu_ref[:, j * FF_CHUNK:(j + 1) * FF_CHUNK])
        act = jnp.square(jnp.maximum(up, 0.0)).astype(bf16)
        acc = acc + _dot(act, wd_ref[j * FF_CHUNK:(j + 1) * FF_CHUNK, :])
    return acc


def _tail_body(x_ref, oa_ref, o0_ref, o1_ref, o2_ref, l0_ref, l1_ref, l2_ref, gate_ref,
               wa_ref, wb_ref, wo_ref, n2_ref, wu_ref, wd_ref, fn_ref, y_ref, *, final):
    x1, h2 = _tail_front(x_ref, oa_ref, (o0_ref, o1_ref, o2_ref), (l0_ref, l1_ref, l2_ref),
                         gate_ref, wa_ref, wb_ref, wo_ref, n2_ref)
    acc = _tail_ff(x1, h2, wu_ref, wd_ref, range(D_FF // FF_CHUNK))
    if final:
        acc = _rms(acc, fn_ref[...])
    y_ref[...] = acc


def _tail_specs(tm):
    tok = lambda wd_: pl.BlockSpec((tm, wd_), lambda i: (i, 0))
    const = lambda shape: pl.BlockSpec(shape, lambda i: (0, 0), pipeline_mode=pl.Buffered(1))
    in_specs = [tok(D_MODEL), tok(WIDTH_A)] + [tok(WIDTH_G)] * 6 + [
        tok(2 * D_MODEL), const((WIDTH_A, D_MODEL)), const((WIDTH_G, D_MODEL)),
        const((D_MODEL, D_MODEL)), const((1, D_MODEL)), const((D_MODEL, D_FF)),
        const((D_FF, D_MODEL)), const((1, D_MODEL))]
    return in_specs, tok(D_MODEL)


def _tail(x, oa, outs, lses, gate, wa, wb, wo, n2, wu, wd, fn, final, tm=512):
    m = x.shape[0]
    tm = min(tm, m)
    assert m % tm == 0
    in_specs, out_spec = _tail_specs(tm)
    return pl.pallas_call(
        functools.partial(_tail_body, final=final),
        grid=(m // tm,),
        in_specs=in_specs,
        out_specs=out_spec,
        out_shape=jax.ShapeDtypeStruct((m, D_MODEL), f32),
        compiler_params=pltpu.CompilerParams(
            dimension_semantics=("arbitrary",), vmem_limit_bytes=VMEM_LIMIT),
        name="tail",
    )(x, oa, *outs, *lses, gate, wa, wb, wo, n2, wu, wd, fn)


FUSED_UNITS = 2
IN_SLOTS = 3
N_TAIL_IN = 16


def _tail_cache_body(*refs, final, dil, slopes, t, win, layer, aliased):
    (x_ref, oa_ref, o0_ref, o1_ref, o2_ref, l0_ref, l1_ref, l2_ref, gate_ref,
     wa_ref, wb_ref, wo_ref, n2_ref, wu_ref, wd_ref, fn_ref) = refs[:N_TAIL_IN]
    q_ref, kvn_ref, c_hbm = refs[N_TAIL_IN:N_TAIL_IN + 3]
    n_in = N_TAIL_IN + (4 if aliased else 3)
    y_ref, os_ref, ls_ref, cn_hbm = refs[n_in:n_in + 4]
    in_buf, out_buf, in_sem, out_sem = refs[n_in + 4:]
    cst = _sample_consts(dil, t, win)
    pipe = _CachePipe(c_hbm, cn_hbm, in_buf, out_buf, in_sem, out_sem, layer, FUSED_UNITS)
    pipe.prologue()
    n_ff = D_FF // FF_CHUNK
    first_part = (3 * n_ff) // 8
    acc = h2 = None
    for j in range(FUSED_UNITS):
        c_in, c_out = pipe.open(j)
        if j == 0:
            x1, h2 = _tail_front(x_ref, oa_ref, (o0_ref, o1_ref, o2_ref),
                                 (l0_ref, l1_ref, l2_ref), gate_ref, wa_ref, wb_ref, wo_ref, n2_ref)
            acc = _tail_ff(x1, h2, wu_ref, wd_ref, range(first_part))
        else:
            acc = _tail_ff(acc, h2, wu_ref, wd_ref, range(first_part, n_ff))
            if final:
                acc = _rms(acc, fn_ref[...])
            y_ref[...] = acc
        _sample_unit(c_in, c_out, q_ref.at[j], kvn_ref.at[j], os_ref.at[j], ls_ref.at[j], cst,
                     slopes, t, win)
        pipe.close(j)
    pipe.epilogue()


def _tail_with_cache(x, oa, outs, lses, gate, wa, wb, wo, n2, wu, wd, fn, final,
                     q, kvn, cache_t, prev, layer, g, t):
    m = x.shape[0]
    bsz, win = cache_t.shape[1], cache_t.shape[-1]
    assert bsz % FUSED_UNITS == 0 and m % (bsz // FUSED_UNITS) == 0 and bsz >= IN_SLOTS
    steps = bsz // FUSED_UNITS
    tm = m // steps
    assert tm % SUBLANES == 0
    in_specs, y_spec = _tail_specs(tm)
    o_spec = pl.BlockSpec((FUSED_UNITS, SUBLANES, WIDTH_G), lambda i: (i, 0, 0))
    hbm = pl.BlockSpec(memory_space=pl.ANY)
    in_specs = in_specs + [o_spec, pl.BlockSpec((FUSED_UNITS, SUBLANES, 2 * WIDTH_G),
                                                lambda i: (i, 0, 0)), hbm]
    args = [x, oa, *outs, *lses, gate, wa, wb, wo, n2, wu, wd, fn, q, kvn, cache_t]
    aliased = prev is not None
    if aliased:
        in_specs.append(hbm)
        args.append(prev)
    y, o, l, cn = pl.pallas_call(
        functools.partial(_tail_cache_body, final=final, dil=DILATIONS[g], slopes=_slopes(g), t=t,
                          win=win, layer=layer, aliased=aliased),
        grid=(steps,),
        in_specs=in_specs,
        out_specs=[y_spec, o_spec, o_spec, hbm],
        out_shape=[jax.ShapeDtypeStruct((m, D_MODEL), f32),
                   jax.ShapeDtypeStruct((bsz, SUBLANES, WIDTH_G), f32),
                   jax.ShapeDtypeStruct((bsz, SUBLANES, WIDTH_G), f32),
                   jax.ShapeDtypeStruct(cache_t.shape, f32)],
        scratch_shapes=_CachePipe.scratch(FUSED_UNITS, IN_SLOTS, win),
        input_output_aliases={len(args) - 1: 3} if aliased else {},
        compiler_params=pltpu.CompilerParams(
            dimension_semantics=("arbitrary",), vmem_limit_bytes=VMEM_LIMIT),
        name="tail_cache",
    )(*args)
    return y, o, l, cn


INPROJ_UNITS = 4
INPROJ_SEG_GROUPS = ((0,), (1, 2, 3, 4), (5, 6, 7), (8, 9))
SEG_OFFSETS = tuple(sum(SEG_WIDTHS[:s]) for s in range(len(SEG_WIDTHS)))


def _inproj_cache_body(*refs, tm, tiles_per_seq, dil, slopes, t, win, layer, aliased):
    x_ref, g_ref, w_ref, cw_ref, q_ref, kvn_ref, c_hbm = refs[:7]
    n_in = 8 if aliased else 7
    nseg = len(SEG_WIDTHS)
    out_refs = refs[n_in:n_in + nseg]
    last_ref, os_ref, ls_ref, cn_hbm = refs[n_in + nseg:n_in + nseg + 4]
    xs_ref, in_buf, out_buf, in_sem, out_sem = refs[n_in + nseg + 4:]
    cst = _sample_consts(dil, t, win)
    pipe = _CachePipe(c_hbm, cn_hbm, in_buf, out_buf, in_sem, out_sem, layer, INPROJ_UNITS)
    pipe.prologue()

    @pl.when(lax.rem(pl.program_id(0), tiles_per_seq) == 0)
    def _():
        xs_ref[pl.ds(0, SUBLANES), :] = jnp.zeros((SUBLANES, SEG_WIDTHS[0]), f32)

    h = _rms(x_ref[...], g_ref[...]).astype(bf16)
    for j in range(INPROJ_UNITS):
        c_in, c_out = pipe.open(j)
        for s in INPROJ_SEG_GROUPS[j]:
            y = _dot(h, w_ref[:, SEG_OFFSETS[s]:SEG_OFFSETS[s] + SEG_WIDTHS[s]])
            if s == 0:
                xs_ref[pl.ds(SUBLANES, tm), :] = y
                conv = xs_ref[pl.ds(SUBLANES - CONV_W + 1, tm), :] * cw_ref[0:1, :]
                for k in range(1, CONV_W):
                    conv = conv + xs_ref[pl.ds(SUBLANES - CONV_W + 1 + k, tm), :] * cw_ref[k:k + 1, :]
                last = xs_ref[pl.ds(tm, SUBLANES), :]
                xs_ref[pl.ds(0, SUBLANES), :] = last
                last_ref[0] = last
                y = conv * _sigmoid(conv)
            out_refs[s][...] = y.astype(out_refs[s].dtype)
        _sample_unit(c_in, c_out, q_ref.at[j], kvn_ref.at[j], os_ref.at[j], ls_ref.at[j], cst,
                     slopes, t, win)
        pipe.close(j)
    pipe.epilogue()


def _inproj_with_cache(x, g, w, conv_w, seq_len, q, kvn, cache_t, prev, layer, grp, t):
    m = x.shape[0]
    bsz, win = cache_t.shape[1], cache_t.shape[-1]
    assert bsz % INPROJ_UNITS == 0 and m % (bsz // INPROJ_UNITS) == 0 and bsz >= IN_SLOTS
    steps = bsz // INPROJ_UNITS
    tm = m // steps
    assert seq_len % tm == 0 and tm % SUBLANES == 0
    per = seq_len // tm
    o_spec = pl.BlockSpec((INPROJ_UNITS, SUBLANES, WIDTH_G), lambda i: (i, 0, 0))
    hbm = pl.BlockSpec(memory_space=pl.ANY)
    in_specs = [pl.BlockSpec((tm, D_MODEL), lambda i: (i, 0)),
                pl.BlockSpec((1, D_MODEL), lambda i: (0, 0)),
                pl.BlockSpec((D_MODEL, N_PERM), lambda i: (0, 0), pipeline_mode=pl.Buffered(1)),
                pl.BlockSpec((CONV_W, SEG_WIDTHS[0]), lambda i: (0, 0)),
                o_spec, pl.BlockSpec((INPROJ_UNITS, SUBLANES, 2 * WIDTH_G), lambda i: (i, 0, 0)), hbm]
    args = [x, g, w, conv_w, q, kvn, cache_t]
    aliased = prev is not None
    if aliased:
        in_specs.append(hbm)
        args.append(prev)
    nseg = len(SEG_WIDTHS)
    out_specs = [pl.BlockSpec((tm, wd), lambda i: (i, 0)) for wd in SEG_WIDTHS] + [
        pl.BlockSpec((1, SUBLANES, SEG_WIDTHS[0]), lambda i: (i // per, 0, 0)), o_spec, o_spec, hbm]
    out_shape = [jax.ShapeDtypeStruct((m, wd), dt) for wd, dt in zip(SEG_WIDTHS, SEG_DTYPES)] + [
        jax.ShapeDtypeStruct((m // seq_len, SUBLANES, SEG_WIDTHS[0]), f32),
        jax.ShapeDtypeStruct((bsz, SUBLANES, WIDTH_G), f32),
        jax.ShapeDtypeStruct((bsz, SUBLANES, WIDTH_G), f32),
        jax.ShapeDtypeStruct(cache_t.shape, f32)]
    res = pl.pallas_call(
        functools.partial(_inproj_cache_body, tm=tm, tiles_per_seq=per, dil=DILATIONS[grp],
                          slopes=_slopes(grp), t=t, win=win, layer=layer, aliased=aliased),
        grid=(steps,),
        in_specs=in_specs,
        out_specs=out_specs,
        out_shape=out_shape,
        scratch_shapes=[pltpu.VMEM((tm + SUBLANES, SEG_WIDTHS[0]), f32)]
        + _CachePipe.scratch(INPROJ_UNITS, IN_SLOTS, win),
        input_output_aliases={len(args) - 1: nseg + 3} if aliased else {},
        compiler_params=pltpu.CompilerParams(
            dimension_semantics=("arbitrary",), vmem_limit_bytes=VMEM_LIMIT),
        name="inproj_cache",
    )(*args)
    return res[:nseg + 1], res[nseg + 1], res[nseg + 2], res[nseg + 3]


def _lane_row(vals):
    return jnp.zeros((1, LANES), f32).at[0, :vals.shape[0]].set(vals.astype(f32))


def _layer_params(l, norm1, w_in, conv_w, a_log, dt_bias, gnorm_a, w_br_a, w_br_b, w_out, norm2,
                  w_up, w_down):
    return dict(
        n1=norm1[l].reshape(1, D_MODEL), w_in=_permute_w_in(w_in[l]), cw=conv_w[l],
        nega=_lane_row(-jnp.exp(a_log[l].astype(f32))), dtb=_lane_row(dt_bias[l]),
        gn=gnorm_a[l].reshape(1, HEAD_DIM_A).astype(f32),
        wa=w_br_a[l].astype(bf16), wb=w_br_b[l].astype(bf16), wo=w_out[l].astype(bf16),
        n2=norm2[l].reshape(1, D_MODEL), wu=w_up[l].astype(bf16), wd=w_down[l].astype(bf16))


def _layer(xp, xs, ts, p, fn, final, layer, caches_t, prev_caches, s_all, s_prev, conv8):
    bs = xs.shape[0]
    xsf = xs.reshape(bs * SUBLANES, D_MODEL)
    qkv, z_s, q0s, q1s, q2s, kv0s, kv1s, kv2s, gate_s, ab_s = _inproj(xsf, p["n1"], p["w_in"])
    rows8 = lambda a_: a_.reshape(bs, SUBLANES, a_.shape[-1])
    qkv8 = rows8(qkv)

    bp, tp, _ = xp.shape
    xpf = xp.reshape(bp * tp, D_MODEL)
    (cs, z, q0, q1, q2, kv0, kv1, kv2, gate_p, ab, last), o1s, l1s, cn1 = _inproj_with_cache(
        xpf, p["n1"], p["w_in"], p["cw"], tp, rows8(q1s), rows8(kv1s), caches_t[1],
        prev_caches[1], layer, 1, ts)
    s0 = jnp.zeros((bp, N_HEADS_A, HEAD_DIM_A, HEAD_DIM_A), f32)
    oa_p, s_new_p = _delta_prompt(cs.reshape(bp, tp, 3 * WIDTH_A), ab.reshape(bp, tp, LANES),
                                  z.reshape(bp, tp, WIDTH_A), p["nega"], p["dtb"], p["gn"], s0)
    outs_p, lses_p, wins_p = [], [], []
    for g, (qg, kvg) in enumerate(((q0, kv0), (q1, kv1), (q2, kv2))):
        kv3 = kvg.reshape(bp, tp, 2 * WIDTH_G)
        o, l = _attn_prompt(qg.reshape(bp, tp, WIDTH_G), kv3, g)
        outs_p.append(o.reshape(bp * tp, WIDTH_G))
        lses_p.append(l.reshape(bp * tp, WIDTH_G))
        keep = min(WINDOWS[g], tp)
        wins_p.append(kv3[:, tp - keep:].reshape(bp, keep, 2, HEADS_PER_GROUP, HEAD_DIM_B))
    conv_p = last[:, SUBLANES - (CONV_W - 1):]

    oa_s, s_stack = _delta_sample(qkv8, conv8, rows8(ab_s), rows8(z_s), p["cw"], p["nega"], p["dtb"],
                                  p["gn"], s_all, s_prev, layer, ts)
    o0s, l0s, cn0 = _attn_sample(rows8(q0s), rows8(kv0s), caches_t[0], prev_caches[0], layer, 0, ts)
    flat = lambda a_: a_.reshape(bs * SUBLANES, WIDTH_G)
    outs_s, lses_s, new_caches = [flat(o0s), flat(o1s)], [flat(l0s), flat(l1s)], [cn0, cn1]

    g = N_GROUPS - 1
    yp, o, l, cn = _tail_with_cache(xpf, oa_p.reshape(bp * tp, WIDTH_A), outs_p, lses_p, gate_p,
                                    p["wa"], p["wb"], p["wo"], p["n2"], p["wu"], p["wd"], fn, final,
                                    rows8(q2s), rows8(kv2s), caches_t[g], prev_caches[g], layer, g, ts)
    outs_s.append(flat(o))
    lses_s.append(flat(l))
    new_caches.append(cn)
    ys = _tail(xsf, oa_s.reshape(bs * SUBLANES, WIDTH_A), outs_s, lses_s, gate_s, p["wa"], p["wb"],
               p["wo"], p["n2"], p["wu"], p["wd"], fn, final)
    conv_s = qkv8[:, ts - (CONV_W - 1):ts]
    return (yp.reshape(bp, tp, D_MODEL), ys.reshape(bs, SUBLANES, D_MODEL), wins_p, s_new_p, conv_p,
            new_caches, s_stack, conv_s)


def kernel(x_prompt, x_sample, cache_win0, cache_win1, cache_win2, state_delta, state_conv, norm1,
           w_in, conv_w, a_log, dt_bias, gnorm_a, w_br_a, w_br_b, w_out, norm2, w_up, w_down,
           final_norm):
    depth = w_in.shape[0]
    ts = x_sample.shape[1]
    assert ts + CONV_W - 1 <= SUBLANES
    pad_rows = lambda a_, axis: jnp.pad(
        a_, [(0, SUBLANES - a_.shape[axis]) if d == axis else (0, 0) for d in range(a_.ndim)])
    conv8 = pad_rows(state_conv.astype(f32), 2)
    fn = final_norm.reshape(1, D_MODEL)
    caches_t = [c.transpose(0, 1, 3, 4, 5, 2) for c in (cache_win0, cache_win1, cache_win2)]
    new_caches = [None] * N_GROUPS
    s_stack = None
    xp, xs = x_prompt, pad_rows(x_sample, 1)
    wins_p = [[] for _ in range(N_GROUPS)]
    delta_p, conv_p, conv_s = [], [], []
    for l in range(depth):
        p = _layer_params(l, norm1, w_in, conv_w, a_log, dt_bias, gnorm_a, w_br_a, w_br_b, w_out,
                          norm2, w_up, w_down)
        xp, xs, wp, sp, cp, new_caches, s_stack, cs = _layer(
            xp, xs, ts, p, fn, l == depth - 1, l, caches_t, new_caches, state_delta, s_stack, conv8)
        for g in range(N_GROUPS):
            wins_p[g].append(wp[g])
        delta_p.append(sp)
        conv_p.append(cp)
        conv_s.append(cs)
    st = jnp.stack
    wins_s = [c.transpose(0, 1, 5, 2, 3, 4) for c in new_caches]
    return (xp, xs[:, :ts], st(wins_p[0]), st(wins_p[1]), st(wins_p[2]), st(delta_p), st(conv_p),
            wins_s[0], wins_s[1], wins_s[2], s_stack, st(conv_s))
```

```python
import functools

import jax
import jax.numpy as jnp
from jax import lax
from jax.experimental import pallas as pl
from jax.experimental.pallas import tpu as pltpu

f32 = jnp.float32
bf16 = jnp.bfloat16

D_MODEL = 1024
N_HEADS_A = 4
HEAD_DIM_A = 128
WIDTH_A = N_HEADS_A * HEAD_DIM_A
CONV_W = 4
WINDOWS = (128, 512, 2048)
DILATIONS = (1, 4, 16)
N_GROUPS = 3
HEADS_PER_GROUP = 4
HEAD_DIM_B = 64
N_HEADS_B = N_GROUPS * HEADS_PER_GROUP
WIDTH_G = HEADS_PER_GROUP * HEAD_DIM_B
D_FF = 4 * D_MODEL
EPS = 1e-6
N_OFF = 128
NEG = -1e30

SUBLANES = 8
LANES = 128
VMEM_LIMIT = 56 * 1024 * 1024

SEG_WIDTHS = (3 * WIDTH_A, WIDTH_A, WIDTH_G, WIDTH_G, WIDTH_G,
              2 * WIDTH_G, 2 * WIDTH_G, 2 * WIDTH_G, 2 * D_MODEL, LANES)
N_PERM = sum(SEG_WIDTHS)
SEG_DTYPES = (f32,) * 8 + (bf16, f32)

NN = (((1,), (0,)), ((), ()))
NT = (((1,), (1,)), ((), ()))
TN = (((0,), (0,)), ((), ()))


def _slopes(g):
    return tuple(2.0 ** (-8.0 * (g * HEADS_PER_GROUP + h + 1) / N_HEADS_B)
                 for h in range(HEADS_PER_GROUP))


def _dot(a, b, dims=NN):
    return lax.dot_general(a, b, dims, preferred_element_type=f32)


def _mm1(a, b, dims=NN):
    return _dot(a.astype(bf16), b.astype(bf16), dims)


def _split(a):
    hi = a.astype(bf16)
    lo = (a - hi.astype(f32)).astype(bf16)
    return hi, lo


def _split3(a):
    hi = a.astype(bf16)
    r = a - hi.astype(f32)
    mid = r.astype(bf16)
    lo = (r - mid.astype(f32)).astype(bf16)
    return hi, mid, lo


def _mm3s(a, b, dims=NN):
    return _dot(a[0], b[0], dims) + _dot(a[0], b[1], dims) + _dot(a[1], b[0], dims)


def _mm3(a, b, dims=NN):
    return _mm3s(_split(a), _split(b), dims)


def _mm2s(a, b_hi, dims=NN):
    return _dot(a[0], b_hi, dims) + _dot(a[1], b_hi, dims)


def _sigmoid(x):
    return 1.0 / (1.0 + jnp.exp(-x))


def _softplus(x):
    return jnp.maximum(x, 0.0) + jnp.log1p(jnp.exp(-jnp.abs(x)))


def _rms(x, g):
    return x * lax.rsqrt(jnp.mean(x * x, axis=-1, keepdims=True) + EPS) * g


def _inproj_body(x_ref, g_ref, w_ref, *out_refs):
    h = _rms(x_ref[...], g_ref[...]).astype(bf16)
    off = 0
    for o_ref, width in zip(out_refs, SEG_WIDTHS):
        o_ref[...] = _dot(h, w_ref[:, off:off + width]).astype(o_ref.dtype)
        off += width


def _inproj_conv_body(x_ref, g_ref, w_ref, cw_ref, *refs, tm, tiles_per_seq):
    out_refs, last_ref, xs_ref = refs[:len(SEG_WIDTHS)], refs[-2], refs[-1]
    i = pl.program_id(0)

    @pl.when(lax.rem(i, tiles_per_seq) == 0)
    def _():
        xs_ref[pl.ds(0, SUBLANES), :] = jnp.zeros((SUBLANES, SEG_WIDTHS[0]), f32)

    h = _rms(x_ref[...], g_ref[...]).astype(bf16)
    xs_ref[pl.ds(SUBLANES, tm), :] = _dot(h, w_ref[:, :SEG_WIDTHS[0]])
    conv = xs_ref[pl.ds(SUBLANES - CONV_W + 1, tm), :] * cw_ref[0:1, :]
    for j in range(1, CONV_W):
        conv = conv + xs_ref[pl.ds(SUBLANES - CONV_W + 1 + j, tm), :] * cw_ref[j:j + 1, :]
    last = xs_ref[pl.ds(tm, SUBLANES), :]
    xs_ref[pl.ds(0, SUBLANES), :] = last
    last_ref[0] = last
    out_refs[0][...] = conv * _sigmoid(conv)
    off = SEG_WIDTHS[0]
    for o_ref, width in zip(out_refs[1:], SEG_WIDTHS[1:]):
        o_ref[...] = _dot(h, w_ref[:, off:off + width]).astype(o_ref.dtype)
        off += width


def _inproj(x, g, w, tm=512, conv_w=None, seq_len=None):
    m = x.shape[0]
    tm = min(tm, m)
    assert m % tm == 0
    in_specs = [pl.BlockSpec((tm, D_MODEL), lambda i: (i, 0)),
                pl.BlockSpec((1, D_MODEL), lambda i: (0, 0)),
                pl.BlockSpec((D_MODEL, N_PERM), lambda i: (0, 0), pipeline_mode=pl.Buffered(1))]
    out_specs = [pl.BlockSpec((tm, wd), lambda i: (i, 0)) for wd in SEG_WIDTHS]
    out_shape = [jax.ShapeDtypeStruct((m, wd), dt) for wd, dt in zip(SEG_WIDTHS, SEG_DTYPES)]
    args = [x, g, w]
    body, scratch = _inproj_body, []
    if conv_w is not None:
        assert seq_len % tm == 0 and tm >= SUBLANES
        per = seq_len // tm
        in_specs.append(pl.BlockSpec((CONV_W, SEG_WIDTHS[0]), lambda i: (0, 0)))
        out_specs.append(pl.BlockSpec((1, SUBLANES, SEG_WIDTHS[0]), lambda i: (i // per, 0, 0)))
        out_shape.append(jax.ShapeDtypeStruct((m // seq_len, SUBLANES, SEG_WIDTHS[0]), f32))
        args.append(conv_w)
        body = functools.partial(_inproj_conv_body, tm=tm, tiles_per_seq=per)
        scratch = [pltpu.VMEM((tm + SUBLANES, SEG_WIDTHS[0]), f32)]
    return pl.pallas_call(
        body,
        grid=(m // tm,),
        in_specs=in_specs,
        out_specs=out_specs,
        out_shape=out_shape,
        scratch_shapes=scratch,
        compiler_params=pltpu.CompilerParams(
            dimension_semantics=("arbitrary",), vmem_limit_bytes=VMEM_LIMIT),
        name="inproj",
    )(*args)


def _permute_w_in(w):
    o_z = 3 * WIDTH_A
    o_ab = o_z + WIDTH_A
    o_q = o_ab + 2 * N_HEADS_A
    o_k = o_q + N_GROUPS * WIDTH_G
    o_v = o_k + N_GROUPS * WIDTH_G
    o_gate = o_v + N_GROUPS * WIDTH_G
    w = w.astype(bf16)
    parts = [w[:, :o_ab], w[:, o_q:o_k]]
    for g in range(N_GROUPS):
        parts.append(w[:, o_k + g * WIDTH_G:o_k + (g + 1) * WIDTH_G])
        parts.append(w[:, o_v + g * WIDTH_G:o_v + (g + 1) * WIDTH_G])
    parts.append(w[:, o_gate:])
    parts.append(w[:, o_ab:o_q])
    parts.append(jnp.zeros((w.shape[0], LANES - 2 * N_HEADS_A), w.dtype))
    return jnp.concatenate(parts, axis=1)


CHUNK = 128
INV_BASE = 16
HEADS = tuple(range(N_HEADS_A))


def _tri_inverse_heads(a_list, row, col):
    eye = jnp.where(row == col, 1.0, 0.0)
    blk = lambda idx, size: jnp.right_shift(idx, size.bit_length() - 1)
    same16 = blk(row, INV_BASE) == blk(col, INV_BASE)
    ad = [jnp.where(same16, a, 0.0) for a in a_list]
    p = [eye - x for x in ad]
    xs = [_split(x) for x in ad]
    x = [_mm3s(s, s) for s in xs]
    xs = [_split(v) for v in x]
    p = [pv + _mm3s(_split(pv), xx) for pv, xx in zip(p, xs)]
    xb = [s[0] for s in xs]
    n = 4
    while n < INV_BASE:
        xb = [_dot(v, v).astype(bf16) for v in xb]
        p = [pv + _dot(pv.astype(bf16), xx) for pv, xx in zip(p, xb)]
        n *= 2
    s = INV_BASE
    while s < CHUNK:
        pair = (blk(row, 2 * s) == blk(col, 2 * s)) & (blk(row, s) != blk(col, s))
        es = [jnp.where(pair, a, 0.0).astype(bf16) for a in a_list]
        ps = [_split(v) for v in p]
        pe = [_split(_mm2s(pp, ee)) for pp, ee in zip(ps, es)]
        p = [pv - _mm3s(x, pp) for pv, x, pp in zip(p, pe, ps)]
        s *= 2
    return [_split(v) for v in p]


def _delta_prompt_body(cs_ref, ab_ref, z_ref, nega_ref, dtb_ref, gn_ref, s0_ref,
                       o_ref, s_out_ref, s_ref, *, nchunk):
    c = pl.program_id(1)

    @pl.when(c == 0)
    def _():
        s_ref[...] = s0_ref[0]

    items = [(ck, h) for ck in range(nchunk) for h in HEADS]
    rows = lambda ck: pl.ds(ck * CHUNK, CHUNK)
    log_a = [nega_ref[...] * _softplus(ab_ref[0, rows(ck), :] + dtb_ref[...])
             for ck in range(nchunk)]
    beta_all = [_sigmoid(ab_ref[0, rows(ck), :]) for ck in range(nchunk)]

    row = lax.broadcasted_iota(jnp.int32, (CHUNK, CHUNK), 0)
    col = lax.broadcasted_iota(jnp.int32, (CHUNK, CHUNK), 1)
    causal = row >= col
    strict = row > col
    ltri = jnp.where(causal, 1.0, 0.0).astype(bf16)

    def l2n(x, scale):
        return x * (lax.rsqrt(jnp.sum(x * x, axis=-1, keepdims=True) + EPS) * scale)

    def part(ck, h, which):
        lo = which * WIDTH_A + h * HEAD_DIM_A
        return cs_ref[0, rows(ck), lo:lo + HEAD_DIM_A]

    q = [l2n(part(ck, h, 0), HEAD_DIM_A ** -0.5) for ck, h in items]
    k = [l2n(part(ck, h, 1), 1.0) for ck, h in items]
    v = [part(ck, h, 2) for ck, h in items]
    beta = [beta_all[ck][:, N_HEADS_A + h:N_HEADS_A + h + 1] for ck, h in items]
    g3 = [_split3(jnp.broadcast_to(log_a[ck][:, h:h + 1], (CHUNK, HEAD_DIM_A))) for ck, h in items]
    cum = [_dot(ltri, t[0]) + _dot(ltri, t[1]) + _dot(ltri, t[2]) for t in g3]
    decay = [jnp.exp(jnp.where(causal, x - x.T, NEG)) for x in cum]
    e_g = [jnp.exp(x) for x in cum]
    g_last = [x[CHUNK - 1:CHUNK, :] for x in cum]
    kb = [a * b for a, b in zip(k, beta)]
    k_bf = [x.astype(bf16) for x in k]
    a_mat = [jnp.where(strict, _dot(x.astype(bf16), y, NT) * d, 0.0)
             for x, y, d in zip(kb, k_bf, decay)]
    t_inv = _tri_inverse_heads(a_mat, row, col)
    w = [_mm2s(t, (x * e).astype(bf16)) for t, x, e in zip(t_inv, kb, e_g)]
    u = [_mm2s(t, (x * b).astype(bf16)) for t, x, b in zip(t_inv, v, beta)]
    qk = [_dot(x.astype(bf16), y, NT) * d for x, y, d in zip(q, k_bf, decay)]
    qg = [x * e for x, e in zip(q, e_g)]
    kd = [x * jnp.exp(gl - cm) for x, gl, cm in zip(k, g_last, cum)]
    s_cur = [s_ref[h] for h in HEADS]
    for ck in range(nchunk):
        n0 = ck * N_HEADS_A
        s_sp = [_split(x) for x in s_cur]
        v_new = [u[n0 + h] - _mm3s(_split(w[n0 + h]), s_sp[h]) for h in HEADS]
        o = [_dot(qg[n0 + h].astype(bf16), s_sp[h][0]) + _mm1(qk[n0 + h], v_new[h]) for h in HEADS]
        s_cur = [s_cur[h] * jnp.exp(g_last[n0 + h]) + _mm3(kd[n0 + h], v_new[h], TN) for h in HEADS]
        for h in HEADS:
            lo = h * HEAD_DIM_A
            zz = z_ref[0, rows(ck), lo:lo + HEAD_DIM_A]
            o_ref[0, rows(ck), lo:lo + HEAD_DIM_A] = _rms(o[h], gn_ref[...]) * (zz * _sigmoid(zz))
    for h in HEADS:
        s_ref[h] = s_cur[h]

    @pl.when(c == pl.num_programs(1) - 1)
    def _():
        s_out_ref[0] = s_ref[...]


DELTA_CHUNKS_PER_STEP = 2


def _delta_prompt(cs, ab, z, nega, dtb, gn, s0):
    b, t, _ = cs.shape
    nchunk = DELTA_CHUNKS_PER_STEP if t % (DELTA_CHUNKS_PER_STEP * CHUNK) == 0 else 1
    tb = nchunk * CHUNK
    assert t % tb == 0
    return pl.pallas_call(
        functools.partial(_delta_prompt_body, nchunk=nchunk),
        grid=(b, t // tb),
        in_specs=[pl.BlockSpec((1, tb, 3 * WIDTH_A), lambda i, c: (i, c, 0)),
                  pl.BlockSpec((1, tb, LANES), lambda i, c: (i, c, 0)),
                  pl.BlockSpec((1, tb, WIDTH_A), lambda i, c: (i, c, 0)),
                  pl.BlockSpec((1, LANES), lambda i, c: (0, 0)),
                  pl.BlockSpec((1, LANES), lambda i, c: (0, 0)),
                  pl.BlockSpec((1, HEAD_DIM_A), lambda i, c: (0, 0)),
                  pl.BlockSpec((1, N_HEADS_A, HEAD_DIM_A, HEAD_DIM_A), lambda i, c: (i, 0, 0, 0))],
        out_specs=[pl.BlockSpec((1, tb, WIDTH_A), lambda i, c: (i, c, 0)),
                   pl.BlockSpec((1, N_HEADS_A, HEAD_DIM_A, HEAD_DIM_A), lambda i, c: (i, 0, 0, 0))],
        out_shape=[jax.ShapeDtypeStruct((b, t, WIDTH_A), f32),
                   jax.ShapeDtypeStruct((b, N_HEADS_A, HEAD_DIM_A, HEAD_DIM_A), f32)],
        scratch_shapes=[pltpu.VMEM((N_HEADS_A, HEAD_DIM_A, HEAD_DIM_A), f32)],
        compiler_params=pltpu.CompilerParams(
            dimension_semantics=("arbitrary", "arbitrary"), vmem_limit_bytes=VMEM_LIMIT),
        name="delta_prompt",
    )(cs, ab, z, nega, dtb, gn, s0)


def _delta_sample_body(*refs, nb, t):
    x_ref, st_ref, ab_ref, z_ref, cw_ref, nega_ref, dtb_ref, gn_ref, s0_ref = refs[:9]
    o_ref, s_out_ref = refs[-2:]
    rowi = lax.broadcasted_iota(jnp.int32, (SUBLANES, LANES), 0)
    real = rowi < t
    seqs = range(nb)
    items = [(b, h) for b in seqs for h in HEADS]
    conv = [sum((pltpu.roll(x_ref[b], CONV_W - 1 - j, axis=0)
                 + pltpu.roll(st_ref[0, b], (SUBLANES - j) % SUBLANES, axis=0)) * cw_ref[j:j + 1, :]
                for j in range(CONV_W)) for b in seqs]
    cs = [x * _sigmoid(x) for x in conv]
    log_a = [jnp.where(real, nega_ref[...] * _softplus(ab_ref[b] + dtb_ref[...]), 0.0) for b in seqs]
    beta_all = [jnp.where(real, _sigmoid(ab_ref[b]), 0.0) for b in seqs]

    def part(b, h, which):
        lo = which * WIDTH_A + h * HEAD_DIM_A
        return cs[b][:, lo:lo + HEAD_DIM_A]

    def l2n(x, scale):
        return x * (lax.rsqrt(jnp.sum(x * x, axis=-1, keepdims=True) + EPS) * scale)

    q = [l2n(part(b, h, 0), HEAD_DIM_A ** -0.5) for b, h in items]
    k = [l2n(part(b, h, 1), 1.0) for b, h in items]
    v = [part(b, h, 2) for b, h in items]
    beta = [beta_all[b][:, N_HEADS_A + h:N_HEADS_A + h + 1] for b, h in items]
    gb = [jnp.broadcast_to(log_a[b][:, h:h + 1], (SUBLANES, LANES)) for b, h in items]
    cum = [sum(jnp.where(rowi >= j, x[j:j + 1, :], 0.0) for j in range(t)) for x in gb]
    g_last = [x[t - 1:t, :] for x in cum]
    e_g = [jnp.exp(x) for x in cum]
    kb = [x * y for x, y in zip(k, beta)]
    w = [x * y for x, y in zip(kb, e_g)]
    u = [x * y for x, y in zip(v, beta)]
    dec = [[jnp.exp(jnp.where(rowi >= j, x - x[j:j + 1, :], NEG)) for j in range(t)] for x in cum]
    qk_cols = [[jnp.sum(q[n] * k[n][j:j + 1, :], axis=-1, keepdims=True) * dec[n][j]
                for j in range(t)] for n in range(len(items))]
    a_cols = [[jnp.where(rowi > j, jnp.sum(kb[n] * k[n][j:j + 1, :], axis=-1, keepdims=True)
                         * dec[n][j], 0.0) for j in range(t - 1)] for n in range(len(items))]
    for j in range(t - 1):
        w = [x - a[j] * x[j:j + 1, :] for x, a in zip(w, a_cols)]
        u = [x - a[j] * x[j:j + 1, :] for x, a in zip(u, a_cols)]
    qg = [x * y for x, y in zip(q, e_g)]
    kd = [x * jnp.exp(gl - cm) for x, gl, cm in zip(k, g_last, cum)]
    s_old = [s0_ref[0, b, h] for b, h in items]
    s_sp = [_split(x) for x in s_old]
    v_new = [uu - _mm3s(_split(ww), ss) for uu, ww, ss in zip(u, w, s_sp)]
    o = [_mm3s(_split(x), ss) for x, ss in zip(qg, s_sp)]
    o = [x + sum(c[j] * vn[j:j + 1, :] for j in range(t)) for x, c, vn in zip(o, qk_cols, v_new)]
    s_new = [so * jnp.exp(gl) + _mm3(x, vn, TN) for so, gl, x, vn in zip(s_old, g_last, kd, v_new)]
    for n, (b, h) in enumerate(items):
        lo = h * HEAD_DIM_A
        s_out_ref[0, b, h] = s_new[n]
        zz = z_ref[b, :, lo:lo + HEAD_DIM_A]
        o_ref[b, :, lo:lo + HEAD_DIM_A] = _rms(o[n], gn_ref[...]) * (zz * _sigmoid(zz))


def _delta_sample(x8, st8, ab, z, cw, nega, dtb, gn, s_all, s_prev, layer, t, nb=8):
    b = x8.shape[0]
    assert t + CONV_W - 1 <= SUBLANES
    aliased = s_prev is not None
    state_spec = pl.BlockSpec((1, nb, N_HEADS_A, HEAD_DIM_A, HEAD_DIM_A),
                              lambda i: (layer, i, 0, 0, 0))
    in_specs = [pl.BlockSpec((nb, SUBLANES, 3 * WIDTH_A), lambda i: (i, 0, 0)),
                pl.BlockSpec((1, nb, SUBLANES, 3 * WIDTH_A), lambda i: (layer, i, 0, 0)),
                pl.BlockSpec((nb, SUBLANES, LANES), lambda i: (i, 0, 0)),
                pl.BlockSpec((nb, SUBLANES, WIDTH_A), lambda i: (i, 0, 0)),
                pl.BlockSpec((CONV_W, 3 * WIDTH_A), lambda i: (0, 0)),
                pl.BlockSpec((1, LANES), lambda i: (0, 0)),
                pl.BlockSpec((1, LANES), lambda i: (0, 0)),
                pl.BlockSpec((1, HEAD_DIM_A), lambda i: (0, 0)),
                state_spec]
    args = [x8, st8, ab, z, cw, nega, dtb, gn, s_all]
    if aliased:
        in_specs.append(pl.BlockSpec(memory_space=pl.ANY))
        args.append(s_prev)
    return pl.pallas_call(
        functools.partial(_delta_sample_body, nb=nb, t=t),
        grid=(b // nb,),
        in_specs=in_specs,
        out_specs=[pl.BlockSpec((nb, SUBLANES, WIDTH_A), lambda i: (i, 0, 0)), state_spec],
        out_shape=[jax.ShapeDtypeStruct((b, SUBLANES, WIDTH_A), f32),
                   jax.ShapeDtypeStruct(s_all.shape, f32)],
        input_output_aliases={len(args) - 1: 1} if aliased else {},
        compiler_params=pltpu.CompilerParams(
            dimension_semantics=("arbitrary",), vmem_limit_bytes=VMEM_LIMIT),
        name="delta_sample",
    )(*args)


ATT_TB = 2048
Q_CHUNKS = WIDTH_G // LANES
KV_CHUNKS = 2 * WIDTH_G // LANES


def _attn_prompt_body(q_ref, kvh_ref, kvc_ref, o_ref, l_ref, q_scr, kv_scr, o_scr, l_scr, *,
                      dil, slopes, tb):
    i = pl.program_id(1)
    halo = N_OFF * dil
    for c in range(Q_CHUNKS):
        q_scr[c] = q_ref[0, :, c * LANES:(c + 1) * LANES]
    for c in range(KV_CHUNKS):
        kv_scr[c, pl.ds(0, halo), :] = kvh_ref[0, :, c * LANES:(c + 1) * LANES]
        kv_scr[c, pl.ds(halo, tb), :] = kvc_ref[0, :, c * LANES:(c + 1) * LANES]
    a = lax.broadcasted_iota(jnp.int32, (N_OFF, 2 * N_OFF), 0)
    cc = lax.broadcasted_iota(jnp.int32, (N_OFF, 2 * N_OFF), 1)
    delta = a - cc + N_OFF
    in_win = (delta >= 0) & (delta <= N_OFF)
    bias = [jnp.where(in_win, delta.astype(f32) * (-slopes[h] * dil), NEG)
            for h in range(HEADS_PER_GROUP)]
    upper = lax.broadcasted_iota(jnp.int32, (N_OFF, LANES), 1) >= HEAD_DIM_B
    n_sub = tb // halo

    def sub_block(idx, carry):
        r = idx // n_sub
        u = idx - r * n_sub
        base = r + halo * u
        key_tok = i * tb - halo + base + dil * cc
        in_seq = key_tok >= 0
        cs_ = range(Q_CHUNKS)
        hs = range(HEADS_PER_GROUP)
        qc = [q_scr[c, pl.ds(base, N_OFF, stride=dil), :] * HEAD_DIM_B ** -0.5 for c in cs_]
        kc = [kv_scr[c, pl.ds(base, 2 * N_OFF, stride=dil), :].astype(bf16) for c in cs_]
        vc = [kv_scr[Q_CHUNKS + c, pl.ds(base, 2 * N_OFF, stride=dil), :].astype(bf16) for c in cs_]
        qm = [jnp.where(upper if h % 2 else jnp.logical_not(upper), qc[h // 2], 0.0).astype(bf16)
              for h in hs]
        s = [jnp.where(in_seq, _dot(qm[h], kc[h // 2], NT) + bias[h], NEG) for h in hs]
        m = [jnp.max(s[h], axis=-1, keepdims=True) for h in hs]
        p = [jnp.exp(s[h] - m[h]) for h in hs]
        den = [jnp.sum(p[h], axis=-1, keepdims=True) for h in hs]
        pv = [_dot(p[h].astype(bf16), vc[h // 2]) / den[h] for h in hs]
        lse = [jnp.broadcast_to(m[h] + jnp.log(den[h]), (N_OFF, LANES)) for h in hs]
        for c in cs_:
            o_scr[c, pl.ds(base, N_OFF, stride=dil), :] = jnp.where(upper, pv[2 * c + 1], pv[2 * c])
            l_scr[c, pl.ds(base, N_OFF, stride=dil), :] = jnp.where(upper, lse[2 * c + 1], lse[2 * c])
        return carry

    lax.fori_loop(0, tb // N_OFF, sub_block, 0, unroll=2)
    for c in range(Q_CHUNKS):
        o_ref[0, :, c * LANES:(c + 1) * LANES] = o_scr[c]
        l_ref[0, :, c * LANES:(c + 1) * LANES] = l_scr[c]


def _attn_prompt(q, kv, g):
    b, t, _ = q.shape
    dil = DILATIONS[g]
    halo = N_OFF * dil
    tb = min(ATT_TB, t)
    assert tb % halo == 0 and t % tb == 0
    per = tb // halo
    return pl.pallas_call(
        functools.partial(_attn_prompt_body, dil=dil, slopes=_slopes(g), tb=tb),
        grid=(b, t // tb),
        in_specs=[pl.BlockSpec((1, tb, WIDTH_G), lambda bi, i: (bi, i, 0)),
                  pl.BlockSpec((1, halo, 2 * WIDTH_G),
                               lambda bi, i: (bi, jnp.maximum(i * per - 1, 0), 0)),
                  pl.BlockSpec((1, tb, 2 * WIDTH_G), lambda bi, i: (bi, i, 0))],
        out_specs=[pl.BlockSpec((1, tb, WIDTH_G), lambda bi, i: (bi, i, 0)),
                   pl.BlockSpec((1, tb, WIDTH_G), lambda bi, i: (bi, i, 0))],
        out_shape=[jax.ShapeDtypeStruct((b, t, WIDTH_G), f32),
                   jax.ShapeDtypeStruct((b, t, WIDTH_G), f32)],
        scratch_shapes=[pltpu.VMEM((Q_CHUNKS, tb, LANES), f32),
                        pltpu.VMEM((KV_CHUNKS, halo + tb, LANES), f32),
                        pltpu.VMEM((Q_CHUNKS, tb, LANES), f32),
                        pltpu.VMEM((Q_CHUNKS, tb, LANES), f32)],
        compiler_params=pltpu.CompilerParams(
            dimension_semantics=("arbitrary", "arbitrary"), vmem_limit_bytes=VMEM_LIMIT),
        name=f"attn_prompt_g{g}",
    )(q, kv, kv)


def _sample_consts(dil, t, win):
    rr = lax.broadcasted_iota(jnp.int32, (SUBLANES, win), 1)
    qi = lax.broadcasted_iota(jnp.int32, (SUBLANES, win), 0)
    nn = lax.broadcasted_iota(jnp.int32, (SUBLANES, SUBLANES), 1)
    qn = lax.broadcasted_iota(jnp.int32, (SUBLANES, SUBLANES), 0)
    lane = lax.broadcasted_iota(jnp.int32, (HEAD_DIM_B, LANES), 1)
    e_rows = lax.broadcasted_iota(jnp.int32, (SUBLANES, LANES), 0)
    e_lane = lax.broadcasted_iota(jnp.int32, (SUBLANES, LANES), 1)
    return dict(
        valid=((rr & (dil - 1)) == (qi & (dil - 1))) & (rr >= qi) & (qi < t),
        dist=(win + qi - rr).astype(f32),
        valid_n=(nn <= qn) & (((qn - nn) & (dil - 1)) == 0) & (qn < t),
        dist_n=(qn - nn).astype(f32),
        tail_lanes=lane >= LANES - t,
        real_rows=lax.broadcasted_iota(jnp.int32, (SUBLANES, HEAD_DIM_B), 0) < t,
        place=jnp.where((e_lane == e_rows + (LANES - t)) & (e_rows < t), 1.0, 0.0).astype(bf16))


def _sample_unit(c_in, c_out, q_b, kvn_b, o_b, l_b, cst, slopes, t, win):
    kvn = kvn_b[...]
    qs = q_b[...] * HEAD_DIM_B ** -0.5
    head = lambda a_, h, base=0: a_[:, base + h * HEAD_DIM_B:base + (h + 1) * HEAD_DIM_B]
    k3 = _split3(kvn)
    place = cst["place"]
    new_t = _dot(k3[0], place, TN) + _dot(k3[1], place, TN) + _dot(k3[2], place, TN)
    hs = range(HEADS_PER_GROUP)
    k_t = [c_in[0, h] for h in hs]
    v_t = [c_in[1, h] for h in hs]
    q8 = [head(qs, h).astype(bf16) for h in hs]
    s = [jnp.where(cst["valid"], _dot(q8[h], k_t[h].astype(bf16)) - slopes[h] * cst["dist"], NEG)
         for h in hs]
    sn = [jnp.where(cst["valid_n"],
                    _dot(q8[h], head(kvn, h).astype(bf16), NT) - slopes[h] * cst["dist_n"], NEG)
          for h in hs]
    m = [jnp.maximum(jnp.max(s[h], axis=-1, keepdims=True),
                     jnp.max(sn[h], axis=-1, keepdims=True)) for h in hs]
    p = [jnp.exp(s[h] - m[h]) for h in hs]
    pn = [jnp.exp(sn[h] - m[h]) for h in hs]
    den = [jnp.sum(p[h], axis=-1, keepdims=True) + jnp.sum(pn[h], axis=-1, keepdims=True)
           for h in hs]
    acc = [_dot(p[h].astype(bf16), v_t[h].astype(bf16), NT) + _mm1(pn[h], head(kvn, h, WIDTH_G))
           for h in hs]
    for h in hs:
        o_b[:, h * HEAD_DIM_B:(h + 1) * HEAD_DIM_B] = jnp.where(cst["real_rows"], acc[h] / den[h], 0.0)
        l_b[:, h * HEAD_DIM_B:(h + 1) * HEAD_DIM_B] = jnp.broadcast_to(
            m[h] + jnp.log(den[h]), (SUBLANES, HEAD_DIM_B))
        for kv, x in ((0, k_t[h]), (1, v_t[h])):
            rolled = pltpu.roll(x, win - t, axis=1)
            fresh = new_t[(kv * HEADS_PER_GROUP + h) * HEAD_DIM_B:
                          (kv * HEADS_PER_GROUP + h + 1) * HEAD_DIM_B, :]
            if win > LANES:
                c_out[kv, h, :, pl.ds(0, win - LANES)] = rolled[:, :win - LANES]
            c_out[kv, h, :, pl.ds(win - LANES, LANES)] = jnp.where(
                cst["tail_lanes"], fresh, rolled[:, win - LANES:])


def _attn_sample_body(*refs, dil, slopes, nb, t, win):
    q_ref, kvn_ref, c_ref = refs[:3]
    o_ref, l_ref, cn_ref = refs[-3:]
    cst = _sample_consts(dil, t, win)

    def one_sequence(b, carry):
        _sample_unit(c_ref.at[0, b], cn_ref.at[0, b], q_ref.at[b], kvn_ref.at[b], o_ref.at[b],
                     l_ref.at[b], cst, slopes, t, win)
        return carry

    lax.fori_loop(0, nb, one_sequence, 0, unroll=2 if nb % 2 == 0 else 1)


def _attn_sample(q, kvn, cache_t, prev, layer, g, t):
    bsz, win = cache_t.shape[1], cache_t.shape[-1]
    dil = DILATIONS[g]
    nb = max(1, min(16, 2048 // win))
    aliased = prev is not None
    cache_spec = pl.BlockSpec((1, nb, 2, HEADS_PER_GROUP, HEAD_DIM_B, win),
                              lambda i: (layer, i, 0, 0, 0, 0))
    out_spec = pl.BlockSpec((nb, SUBLANES, WIDTH_G), lambda i: (i, 0, 0))
    in_specs = [out_spec, pl.BlockSpec((nb, SUBLANES, 2 * WIDTH_G), lambda i: (i, 0, 0)), cache_spec]
    args = [q, kvn, cache_t]
    if aliased:
        in_specs.append(pl.BlockSpec(memory_space=pl.ANY))
        args.append(prev)
    o, l, cn = pl.pallas_call(
        functools.partial(_attn_sample_body, dil=dil, slopes=_slopes(g), nb=nb, t=t, win=win),
        grid=(bsz // nb,),
        in_specs=in_specs,
        out_specs=[out_spec, out_spec, cache_spec],
        out_shape=[jax.ShapeDtypeStruct((bsz, SUBLANES, WIDTH_G), f32),
                   jax.ShapeDtypeStruct((bsz, SUBLANES, WIDTH_G), f32),
                   jax.ShapeDtypeStruct(cache_t.shape, f32)],
        input_output_aliases={len(args) - 1: 2} if aliased else {},
        compiler_params=pltpu.CompilerParams(
            dimension_semantics=("arbitrary",), vmem_limit_bytes=VMEM_LIMIT),
        name=f"attn_sample_g{g}",
    )(*args)
    return o, l, cn


class _CachePipe:
    def __init__(self, c_hbm, cn_hbm, in_buf, out_buf, in_sem, out_sem, layer, units):
        self.c_hbm, self.cn_hbm, self.in_buf, self.out_buf = c_hbm, cn_hbm, in_buf, out_buf
        self.in_sem, self.out_sem, self.layer, self.units = in_sem, out_sem, layer, units
        self.in_slots = in_buf.shape[0]
        self.i = pl.program_id(0)
        self.n = pl.num_programs(0)

    def _fetch(self, u, slot):
        return pltpu.make_async_copy(self.c_hbm.at[self.layer, u], self.in_buf.at[slot],
                                     self.in_sem.at[slot])

    def _flush(self, u, slot):
        return pltpu.make_async_copy(self.out_buf.at[slot], self.cn_hbm.at[self.layer, u],
                                     self.out_sem.at[slot])

    def prologue(self):
        @pl.when(self.i == 0)
        def _():
            for u0 in range(self.in_slots - 1):
                self._fetch(u0, u0).start()

    def open(self, j):
        u = self.i * self.units + j
        slot = lax.rem(u, self.in_slots)
        self._fetch(u, slot).wait()
        ahead = u + self.in_slots - 1

        @pl.when(ahead < self.n * self.units)
        def _():
            self._fetch(ahead, lax.rem(ahead, self.in_slots)).start()

        @pl.when(self.i >= 1)
        def _():
            self._flush(u - self.units, j).wait()

        return self.in_buf.at[slot], self.out_buf.at[j]

    def close(self, j):
        self._flush(self.i * self.units + j, j).start()

    def epilogue(self):
        @pl.when(self.i == self.n - 1)
        def _():
            for j in range(self.units):
                self._flush(self.i * self.units + j, j).wait()

    @staticmethod
    def scratch(units, in_slots, win):
        slab = (2, HEADS_PER_GROUP, HEAD_DIM_B, win)
        return [pltpu.VMEM((in_slots,) + slab, f32), pltpu.VMEM((units,) + slab, f32),
                pltpu.SemaphoreType.DMA((in_slots,)), pltpu.SemaphoreType.DMA((units,))]


FF_CHUNK = 512


def _tail_front(x_ref, oa_ref, o_refs, l_refs, gate_ref, wa_ref, wb_ref, wo_ref, n2_ref):
    l0, l1, l2 = (r[...] for r in l_refs)
    lm = jnp.maximum(jnp.maximum(l0, l1), l2)
    e0, e1, e2 = jnp.exp(l0 - lm), jnp.exp(l1 - lm), jnp.exp(l2 - lm)
    den = e0 + e1 + e2
    ob = (e0 / den) * o_refs[0][...] + (e1 / den) * o_refs[1][...] + (e2 / den) * o_refs[2][...]
    ya = _mm1(oa_ref[...], wa_ref[...])
    yb = _mm1(ob, wb_ref[...])
    merged = (_sigmoid(gate_ref[:, :D_MODEL].astype(f32)) * ya
              + _sigmoid(gate_ref[:, D_MODEL:].astype(f32)) * yb)
    x1 = x_ref[...] + _mm1(merged, wo_ref[...])
    return x1, _rms(x1, n2_ref[...]).astype(bf16)


def _tail_ff(acc, h2, wu_ref, wd_ref, chunks):
    for j in chunks:
        up = _dot(h2, wu_ref[:, j * FF_CHUNK:(j + 1) * FF_CHUNK])
        act = jnp.square(jnp.maximum(up, 0.0)).astype(bf16)
        acc = acc + _dot(act, wd_ref[j * FF_CHUNK:(j + 1) * FF_CHUNK, :])
    return acc


def _tail_body(x_ref, oa_ref, o0_ref, o1_ref, o2_ref, l0_ref, l1_ref, l2_ref, gate_ref,
               wa_ref, wb_ref, wo_ref, n2_ref, wu_ref, wd_ref, fn_ref, y_ref, *, final):
    x1, h2 = _tail_front(x_ref, oa_ref, (o0_ref, o1_ref, o2_ref), (l0_ref, l1_ref, l2_ref),
                         gate_ref, wa_ref, wb_ref, wo_ref, n2_ref)
    acc = _tail_ff(x1, h2, wu_ref, wd_ref, range(D_FF // FF_CHUNK))
    if final:
        acc = _rms(acc, fn_ref[...])
    y_ref[...] = acc


def _tail_specs(tm):
    tok = lambda wd_: pl.BlockSpec((tm, wd_), lambda i: (i, 0))
    const = lambda shape: pl.BlockSpec(shape, lambda i: (0, 0), pipeline_mode=pl.Buffered(1))
    in_specs = [tok(D_MODEL), tok(WIDTH_A)] + [tok(WIDTH_G)] * 6 + [
        tok(2 * D_MODEL), const((WIDTH_A, D_MODEL)), const((WIDTH_G, D_MODEL)),
        const((D_MODEL, D_MODEL)), const((1, D_MODEL)), const((D_MODEL, D_FF)),
        const((D_FF, D_MODEL)), const((1, D_MODEL))]
    return in_specs, tok(D_MODEL)


def _tail(x, oa, outs, lses, gate, wa, wb, wo, n2, wu, wd, fn, final, tm=512):
    m = x.shape[0]
    tm = min(tm, m)
    assert m % tm == 0
    in_specs, out_spec = _tail_specs(tm)
    return pl.pallas_call(
        functools.partial(_tail_body, final=final),
        grid=(m // tm,),
        in_specs=in_specs,
        out_specs=out_spec,
        out_shape=jax.ShapeDtypeStruct((m, D_MODEL), f32),
        compiler_params=pltpu.CompilerParams(
            dimension_semantics=("arbitrary",), vmem_limit_bytes=VMEM_LIMIT),
        name="tail",
    )(x, oa, *outs, *lses, gate, wa, wb, wo, n2, wu, wd, fn)


FUSED_UNITS = 2
IN_SLOTS = 3
N_TAIL_IN = 16


def _tail_cache_body(*refs, final, dil, slopes, t, win, layer, aliased):
    (x_ref, oa_ref, o0_ref, o1_ref, o2_ref, l0_ref, l1_ref, l2_ref, gate_ref,
     wa_ref, wb_ref, wo_ref, n2_ref, wu_ref, wd_ref, fn_ref) = refs[:N_TAIL_IN]
    q_ref, kvn_ref, c_hbm = refs[N_TAIL_IN:N_TAIL_IN + 3]
    n_in = N_TAIL_IN + (4 if aliased else 3)
    y_ref, os_ref, ls_ref, cn_hbm = refs[n_in:n_in + 4]
    in_buf, out_buf, in_sem, out_sem = refs[n_in + 4:]
    cst = _sample_consts(dil, t, win)
    pipe = _CachePipe(c_hbm, cn_hbm, in_buf, out_buf, in_sem, out_sem, layer, FUSED_UNITS)
    pipe.prologue()
    n_ff = D_FF // FF_CHUNK
    first_part = (3 * n_ff) // 8
    acc = h2 = None
    for j in range(FUSED_UNITS):
        c_in, c_out = pipe.open(j)
        if j == 0:
            x1, h2 = _tail_front(x_ref, oa_ref, (o0_ref, o1_ref, o2_ref),
                                 (l0_ref, l1_ref, l2_ref), gate_ref, wa_ref, wb_ref, wo_ref, n2_ref)
            acc = _tail_ff(x1, h2, wu_ref, wd_ref, range(first_part))
        else:
            acc = _tail_ff(acc, h2, wu_ref, wd_ref, range(first_part, n_ff))
            if final:
                acc = _rms(acc, fn_ref[...])
            y_ref[...] = acc
        _sample_unit(c_in, c_out, q_ref.at[j], kvn_ref.at[j], os_ref.at[j], ls_ref.at[j], cst,
                     slopes, t, win)
        pipe.close(j)
    pipe.epilogue()


def _tail_with_cache(x, oa, outs, lses, gate, wa, wb, wo, n2, wu, wd, fn, final,
                     q, kvn, cache_t, prev, layer, g, t):
    m = x.shape[0]
    bsz, win = cache_t.shape[1], cache_t.shape[-1]
    assert bsz % FUSED_UNITS == 0 and m % (bsz // FUSED_UNITS) == 0 and bsz >= IN_SLOTS
    steps = bsz // FUSED_UNITS
    tm = m // steps
    assert tm % SUBLANES == 0
    in_specs, y_spec = _tail_specs(tm)
    o_spec = pl.BlockSpec((FUSED_UNITS, SUBLANES, WIDTH_G), lambda i: (i, 0, 0))
    hbm = pl.BlockSpec(memory_space=pl.ANY)
    in_specs = in_specs + [o_spec, pl.BlockSpec((FUSED_UNITS, SUBLANES, 2 * WIDTH_G),
                                                lambda i: (i, 0, 0)), hbm]
    args = [x, oa, *outs, *lses, gate, wa, wb, wo, n2, wu, wd, fn, q, kvn, cache_t]
    aliased = prev is not None
    if aliased:
        in_specs.append(hbm)
        args.append(prev)
    y, o, l, cn = pl.pallas_call(
        functools.partial(_tail_cache_body, final=final, dil=DILATIONS[g], slopes=_slopes(g), t=t,
                          win=win, layer=layer, aliased=aliased),
        grid=(steps,),
        in_specs=in_specs,
        out_specs=[y_spec, o_spec, o_spec, hbm],
        out_shape=[jax.ShapeDtypeStruct((m, D_MODEL), f32),
                   jax.ShapeDtypeStruct((bsz, SUBLANES, WIDTH_G), f32),
                   jax.ShapeDtypeStruct((bsz, SUBLANES, WIDTH_G), f32),
                   jax.ShapeDtypeStruct(cache_t.shape, f32)],
        scratch_shapes=_CachePipe.scratch(FUSED_UNITS, IN_SLOTS, win),
        input_output_aliases={len(args) - 1: 3} if aliased else {},
        compiler_params=pltpu.CompilerParams(
            dimension_semantics=("arbitrary",), vmem_limit_bytes=VMEM_LIMIT),
        name="tail_cache",
    )(*args)
    return y, o, l, cn


INPROJ_UNITS = 4
INPROJ_SEG_GROUPS = ((0,), (1, 2, 3, 4), (5, 6, 7), (8, 9))
SEG_OFFSETS = tuple(sum(SEG_WIDTHS[:s]) for s in range(len(SEG_WIDTHS)))


def _inproj_cache_body(*refs, tm, tiles_per_seq, dil, slopes, t, win, layer, aliased):
    x_ref, g_ref, w_ref, cw_ref, q_ref, kvn_ref, c_hbm = refs[:7]
    n_in = 8 if aliased else 7
    nseg = len(SEG_WIDTHS)
    out_refs = refs[n_in:n_in + nseg]
    last_ref, os_ref, ls_ref, cn_hbm = refs[n_in + nseg:n_in + nseg + 4]
    xs_ref, in_buf, out_buf, in_sem, out_sem = refs[n_in + nseg + 4:]
    cst = _sample_consts(dil, t, win)
    pipe = _CachePipe(c_hbm, cn_hbm, in_buf, out_buf, in_sem, out_sem, layer, INPROJ_UNITS)
    pipe.prologue()

    @pl.when(lax.rem(pl.program_id(0), tiles_per_seq) == 0)
    def _():
        xs_ref[pl.ds(0, SUBLANES), :] = jnp.zeros((SUBLANES, SEG_WIDTHS[0]), f32)

    h = _rms(x_ref[...], g_ref[...]).astype(bf16)
    for j in range(INPROJ_UNITS):
        c_in, c_out = pipe.open(j)
        for s in INPROJ_SEG_GROUPS[j]:
            y = _dot(h, w_ref[:, SEG_OFFSETS[s]:SEG_OFFSETS[s] + SEG_WIDTHS[s]])
            if s == 0:
                xs_ref[pl.ds(SUBLANES, tm), :] = y
                conv = xs_ref[pl.ds(SUBLANES - CONV_W + 1, tm), :] * cw_ref[0:1, :]
                for k in range(1, CONV_W):
                    conv = conv + xs_ref[pl.ds(SUBLANES - CONV_W + 1 + k, tm), :] * cw_ref[k:k + 1, :]
                last = xs_ref[pl.ds(tm, SUBLANES), :]
                xs_ref[pl.ds(0, SUBLANES), :] = last
                last_ref[0] = last
                y = conv * _sigmoid(conv)
            out_refs[s][...] = y.astype(out_refs[s].dtype)
        _sample_unit(c_in, c_out, q_ref.at[j], kvn_ref.at[j], os_ref.at[j], ls_ref.at[j], cst,
                     slopes, t, win)
        pipe.close(j)
    pipe.epilogue()


def _inproj_with_cache(x, g, w, conv_w, seq_len, q, kvn, cache_t, prev, layer, grp, t):
    m = x.shape[0]
    bsz, win = cache_t.shape[1], cache_t.shape[-1]
    assert bsz % INPROJ_UNITS == 0 and m % (bsz // INPROJ_UNITS) == 0 and bsz >= IN_SLOTS
    steps = bsz // INPROJ_UNITS
    tm = m // steps
    assert seq_len % tm == 0 and tm % SUBLANES == 0
    per = seq_len // tm
    o_spec = pl.BlockSpec((INPROJ_UNITS, SUBLANES, WIDTH_G), lambda i: (i, 0, 0))
    hbm = pl.BlockSpec(memory_space=pl.ANY)
    in_specs = [pl.BlockSpec((tm, D_MODEL), lambda i: (i, 0)),
                pl.BlockSpec((1, D_MODEL), lambda i: (0, 0)),
                pl.BlockSpec((D_MODEL, N_PERM), lambda i: (0, 0), pipeline_mode=pl.Buffered(1)),
                pl.BlockSpec((CONV_W, SEG_WIDTHS[0]), lambda i: (0, 0)),
                o_spec, pl.BlockSpec((INPROJ_UNITS, SUBLANES, 2 * WIDTH_G), lambda i: (i, 0, 0)), hbm]
    args = [x, g, w, conv_w, q, kvn, cache_t]
    aliased = prev is not None
    if aliased:
        in_specs.append(hbm)
        args.append(prev)
    nseg = len(SEG_WIDTHS)
    out_specs = [pl.BlockSpec((tm, wd), lambda i: (i, 0)) for wd in SEG_WIDTHS] + [
        pl.BlockSpec((1, SUBLANES, SEG_WIDTHS[0]), lambda i: (i // per, 0, 0)), o_spec, o_spec, hbm]
    out_shape = [jax.ShapeDtypeStruct((m, wd), dt) for wd, dt in zip(SEG_WIDTHS, SEG_DTYPES)] + [
        jax.ShapeDtypeStruct((m // seq_len, SUBLANES, SEG_WIDTHS[0]), f32),
        jax.ShapeDtypeStruct((bsz, SUBLANES, WIDTH_G), f32),
        jax.ShapeDtypeStruct((bsz, SUBLANES, WIDTH_G), f32),
        jax.ShapeDtypeStruct(cache_t.shape, f32)]
    res = pl.pallas_call(
        functools.partial(_inproj_cache_body, tm=tm, tiles_per_seq=per, dil=DILATIONS[grp],
                          slopes=_slopes(grp), t=t, win=win, layer=layer, aliased=aliased),
        grid=(steps,),
        in_specs=in_specs,
        out_specs=out_specs,
        out_shape=out_shape,
        scratch_shapes=[pltpu.VMEM((tm + SUBLANES, SEG_WIDTHS[0]), f32)]
        + _CachePipe.scratch(INPROJ_UNITS, IN_SLOTS, win),
        input_output_aliases={len(args) - 1: nseg + 3} if aliased else {},
        compiler_params=pltpu.CompilerParams(
            dimension_semantics=("arbitrary",), vmem_limit_bytes=VMEM_LIMIT),
        name="inproj_cache",
    )(*args)
    return res[:nseg + 1], res[nseg + 1], res[nseg + 2], res[nseg + 3]


def _lane_row(vals):
    return jnp.zeros((1, LANES), f32).at[0, :vals.shape[0]].set(vals.astype(f32))


def _layer_params(l, norm1, w_in, conv_w, a_log, dt_bias, gnorm_a, w_br_a, w_br_b, w_out, norm2,
                  w_up, w_down):
    return dict(
        n1=norm1[l].reshape(1, D_MODEL), w_in=_permute_w_in(w_in[l]), cw=conv_w[l],
        nega=_lane_row(-jnp.exp(a_log[l].astype(f32))), dtb=_lane_row(dt_bias[l]),
        gn=gnorm_a[l].reshape(1, HEAD_DIM_A).astype(f32),
        wa=w_br_a[l].astype(bf16), wb=w_br_b[l].astype(bf16), wo=w_out[l].astype(bf16),
        n2=norm2[l].reshape(1, D_MODEL), wu=w_up[l].astype(bf16), wd=w_down[l].astype(bf16))


def _layer(xp, xs, ts, p, fn, final, layer, caches_t, prev_caches, s_all, s_prev, conv8):
    bs = xs.shape[0]
    xsf = xs.reshape(bs * SUBLANES, D_MODEL)
    qkv, z_s, q0s, q1s, q2s, kv0s, kv1s, kv2s, gate_s, ab_s = _inproj(xsf, p["n1"], p["w_in"])
    rows8 = lambda a_: a_.reshape(bs, SUBLANES, a_.shape[-1])
    qkv8 = rows8(qkv)

    bp, tp, _ = xp.shape
    xpf = xp.reshape(bp * tp, D_MODEL)
    (cs, z, q0, q1, q2, kv0, kv1, kv2, gate_p, ab, last), o1s, l1s, cn1 = _inproj_with_cache(
        xpf, p["n1"], p["w_in"], p["cw"], tp, rows8(q1s), rows8(kv1s), caches_t[1],
        prev_caches[1], layer, 1, ts)
    s0 = jnp.zeros((bp, N_HEADS_A, HEAD_DIM_A, HEAD_DIM_A), f32)
    oa_p, s_new_p = _delta_prompt(cs.reshape(bp, tp, 3 * WIDTH_A), ab.reshape(bp, tp, LANES),
                                  z.reshape(bp, tp, WIDTH_A), p["nega"], p["dtb"], p["gn"], s0)
    outs_p, lses_p, wins_p = [], [], []
    for g, (qg, kvg) in enumerate(((q0, kv0), (q1, kv1), (q2, kv2))):
        kv3 = kvg.reshape(bp, tp, 2 * WIDTH_G)
        o, l = _attn_prompt(qg.reshape(bp, tp, WIDTH_G), kv3, g)
        outs_p.append(o.reshape(bp * tp, WIDTH_G))
        lses_p.append(l.reshape(bp * tp, WIDTH_G))
        keep = min(WINDOWS[g], tp)
        wins_p.append(kv3[:, tp - keep:].reshape(bp, keep, 2, HEADS_PER_GROUP, HEAD_DIM_B))
    conv_p = last[:, SUBLANES - (CONV_W - 1):]

    oa_s, s_stack = _delta_sample(qkv8, conv8, rows8(ab_s), rows8(z_s), p["cw"], p["nega"], p["dtb"],
                                  p["gn"], s_all, s_prev, layer, ts)
    o0s, l0s, cn0 = _attn_sample(rows8(q0s), rows8(kv0s), caches_t[0], prev_caches[0], layer, 0, ts)
    flat = lambda a_: a_.reshape(bs * SUBLANES, WIDTH_G)
    outs_s, lses_s, new_caches = [flat(o0s), flat(o1s)], [flat(l0s), flat(l1s)], [cn0, cn1]

    g = N_GROUPS - 1
    yp, o, l, cn = _tail_with_cache(xpf, oa_p.reshape(bp * tp, WIDTH_A), outs_p, lses_p, gate_p,
                                    p["wa"], p["wb"], p["wo"], p["n2"], p["wu"], p["wd"], fn, final,
                                    rows8(q2s), rows8(kv2s), caches_t[g], prev_caches[g], layer, g, ts)
    outs_s.append(flat(o))
    lses_s.append(flat(l))
    new_caches.append(cn)
    ys = _tail(xsf, oa_s.reshape(bs * SUBLANES, WIDTH_A), outs_s, lses_s, gate_s, p["wa"], p["wb"],
               p["wo"], p["n2"], p["wu"], p["wd"], fn, final)
    conv_s = qkv8[:, ts - (CONV_W - 1):ts]
    return (yp.reshape(bp, tp, D_MODEL), ys.reshape(bs, SUBLANES, D_MODEL), wins_p, s_new_p, conv_p,
            new_caches, s_stack, conv_s)


def kernel(x_prompt, x_sample, cache_win0, cache_win1, cache_win2, state_delta, state_conv, norm1,
           w_in, conv_w, a_log, dt_bias, gnorm_a, w_br_a, w_br_b, w_out, norm2, w_up, w_down,
           final_norm):
    depth = w_in.shape[0]
    ts = x_sample.shape[1]
    assert ts + CONV_W - 1 <= SUBLANES
    pad_rows = lambda a_, axis: jnp.pad(
        a_, [(0, SUBLANES - a_.shape[axis]) if d == axis else (0, 0) for d in range(a_.ndim)])
    conv8 = pad_rows(state_conv.astype(f32), 2)
    fn = final_norm.reshape(1, D_MODEL)
    caches_t = [c.transpose(0, 1, 3, 4, 5, 2) for c in (cache_win0, cache_win1, cache_win2)]
    new_caches = [None] * N_GROUPS
    s_stack = None
    xp, xs = x_prompt, pad_rows(x_sample, 1)
    wins_p = [[] for _ in range(N_GROUPS)]
    delta_p, conv_p, conv_s = [], [], []
    for l in range(depth):
        p = _layer_params(l, norm1, w_in, conv_w, a_log, dt_bias, gnorm_a, w_br_a, w_br_b, w_out,
                          norm2, w_up, w_down)
        xp, xs, wp, sp, cp, new_caches, s_stack, cs = _layer(
            xp, xs, ts, p, fn, l == depth - 1, l, caches_t, new_caches, state_delta, s_stack, conv8)
        for g in range(N_GROUPS):
            wins_p[g].append(wp[g])
        delta_p.append(sp)
        conv_p.append(cp)
        conv_s.append(cs)
    st = jnp.stack
    wins_s = [c.transpose(0, 1, 5, 2, 3, 4) for c in new_caches]
    return (xp, xs[:, :ts], st(wins_p[0]), st(wins_p[1]), st(wins_p[2]), st(delta_p), st(conv_p),
            wins_s[0], wins_s[1], wins_s[2], s_stack, st(conv_s))
```

```python
import functools

import jax
import jax.numpy as jnp
from jax import lax
from jax.experimental import pallas as pl
from jax.experimental.pallas import tpu as pltpu

f32 = jnp.float32
bf16 = jnp.bfloat16

D_MODEL = 1024
N_HEADS_A = 4
HEAD_DIM_A = 128
WIDTH_A = N_HEADS_A * HEAD_DIM_A
CONV_W = 4
WINDOWS = (128, 512, 2048)
DILATIONS = (1, 4, 16)
N_GROUPS = 3
HEADS_PER_GROUP = 4
HEAD_DIM_B = 64
N_HEADS_B = N_GROUPS * HEADS_PER_GROUP
WIDTH_G = HEADS_PER_GROUP * HEAD_DIM_B
D_FF = 4 * D_MODEL
EPS = 1e-6
N_OFF = 128
NEG = -1e30

SUBLANES = 8
LANES = 128
VMEM_LIMIT = 56 * 1024 * 1024

SEG_WIDTHS = (3 * WIDTH_A, WIDTH_A, WIDTH_G, WIDTH_G, WIDTH_G,
              2 * WIDTH_G, 2 * WIDTH_G, 2 * WIDTH_G, 2 * D_MODEL, LANES)
N_PERM = sum(SEG_WIDTHS)
SEG_DTYPES = (f32,) * 8 + (bf16, f32)

NN = (((1,), (0,)), ((), ()))
NT = (((1,), (1,)), ((), ()))
TN = (((0,), (0,)), ((), ()))


def _slopes(g):
    return tuple(2.0 ** (-8.0 * (g * HEADS_PER_GROUP + h + 1) / N_HEADS_B)
                 for h in range(HEADS_PER_GROUP))


def _dot(a, b, dims=NN):
    return lax.dot_general(a, b, dims, preferred_element_type=f32)


def _mm1(a, b, dims=NN):
    return _dot(a.astype(bf16), b.astype(bf16), dims)


def _split(a):
    hi = a.astype(bf16)
    lo = (a - hi.astype(f32)).astype(bf16)
    return hi, lo


def _split3(a):
    hi = a.astype(bf16)
    r = a - hi.astype(f32)
    mid = r.astype(bf16)
    lo = (r - mid.astype(f32)).astype(bf16)
    return hi, mid, lo


def _mm3s(a, b, dims=NN):
    return _dot(a[0], b[0], dims) + _dot(a[0], b[1], dims) + _dot(a[1], b[0], dims)


def _mm3(a, b, dims=NN):
    return _mm3s(_split(a), _split(b), dims)


def _mm2s(a, b_hi, dims=NN):
    return _dot(a[0], b_hi, dims) + _dot(a[1], b_hi, dims)


def _sigmoid(x):
    return 1.0 / (1.0 + jnp.exp(-x))


def _softplus(x):
    return jnp.maximum(x, 0.0) + jnp.log1p(jnp.exp(-jnp.abs(x)))


def _rms(x, g):
    return x * lax.rsqrt(jnp.mean(x * x, axis=-1, keepdims=True) + EPS) * g


def _inproj_body(x_ref, g_ref, w_ref, *out_refs):
    h = _rms(x_ref[...], g_ref[...]).astype(bf16)
    off = 0
    for o_ref, width in zip(out_refs, SEG_WIDTHS):
        o_ref[...] = _dot(h, w_ref[:, off:off + width]).astype(o_ref.dtype)
        off += width


def _inproj_conv_body(x_ref, g_ref, w_ref, cw_ref, *refs, tm, tiles_per_seq):
    out_refs, last_ref, xs_ref = refs[:len(SEG_WIDTHS)], refs[-2], refs[-1]
    i = pl.program_id(0)

    @pl.when(lax.rem(i, tiles_per_seq) == 0)
    def _():
        xs_ref[pl.ds(0, SUBLANES), :] = jnp.zeros((SUBLANES, SEG_WIDTHS[0]), f32)

    h = _rms(x_ref[...], g_ref[...]).astype(bf16)
    xs_ref[pl.ds(SUBLANES, tm), :] = _dot(h, w_ref[:, :SEG_WIDTHS[0]])
    conv = xs_ref[pl.ds(SUBLANES - CONV_W + 1, tm), :] * cw_ref[0:1, :]
    for j in range(1, CONV_W):
        conv = conv + xs_ref[pl.ds(SUBLANES - CONV_W + 1 + j, tm), :] * cw_ref[j:j + 1, :]
    last = xs_ref[pl.ds(tm, SUBLANES), :]
    xs_ref[pl.ds(0, SUBLANES), :] = last
    last_ref[0] = last
    out_refs[0][...] = conv * _sigmoid(conv)
    off = SEG_WIDTHS[0]
    for o_ref, width in zip(out_refs[1:], SEG_WIDTHS[1:]):
        o_ref[...] = _dot(h, w_ref[:, off:off + width]).astype(o_ref.dtype)
        off += width


def _inproj(x, g, w, tm=512, conv_w=None, seq_len=None):
    m = x.shape[0]
    tm = min(tm, m)
    assert m % tm == 0
    in_specs = [pl.BlockSpec((tm, D_MODEL), lambda i: (i, 0)),
                pl.BlockSpec((1, D_MODEL), lambda i: (0, 0)),
                pl.BlockSpec((D_MODEL, N_PERM), lambda i: (0, 0), pipeline_mode=pl.Buffered(1))]
    out_specs = [pl.BlockSpec((tm, wd), lambda i: (i, 0)) for wd in SEG_WIDTHS]
    out_shape = [jax.ShapeDtypeStruct((m, wd), dt) for wd, dt in zip(SEG_WIDTHS, SEG_DTYPES)]
    args = [x, g, w]
    body, scratch = _inproj_body, []
    if conv_w is not None:
        assert seq_len % tm == 0 and tm >= SUBLANES
        per = seq_len // tm
        in_specs.append(pl.BlockSpec((CONV_W, SEG_WIDTHS[0]), lambda i: (0, 0)))
        out_specs.append(pl.BlockSpec((1, SUBLANES, SEG_WIDTHS[0]), lambda i: (i // per, 0, 0)))
        out_shape.append(jax.ShapeDtypeStruct((m // seq_len, SUBLANES, SEG_WIDTHS[0]), f32))
        args.append(conv_w)
        body = functools.partial(_inproj_conv_body, tm=tm, tiles_per_seq=per)
        scratch = [pltpu.VMEM((tm + SUBLANES, SEG_WIDTHS[0]), f32)]
    return pl.pallas_call(
        body,
        grid=(m // tm,),
        in_specs=in_specs,
        out_specs=out_specs,
        out_shape=out_shape,
        scratch_shapes=scratch,
        compiler_params=pltpu.CompilerParams(
            dimension_semantics=("arbitrary",), vmem_limit_bytes=VMEM_LIMIT),
        name="inproj",
    )(*args)


def _permute_w_in(w):
    o_z = 3 * WIDTH_A
    o_ab = o_z + WIDTH_A
    o_q = o_ab + 2 * N_HEADS_A
    o_k = o_q + N_GROUPS * WIDTH_G
    o_v = o_k + N_GROUPS * WIDTH_G
    o_gate = o_v + N_GROUPS * WIDTH_G
    w = w.astype(bf16)
    parts = [w[:, :o_ab], w[:, o_q:o_k]]
    for g in range(N_GROUPS):
        parts.append(w[:, o_k + g * WIDTH_G:o_k + (g + 1) * WIDTH_G])
        parts.append(w[:, o_v + g * WIDTH_G:o_v + (g + 1) * WIDTH_G])
    parts.append(w[:, o_gate:])
    parts.append(w[:, o_ab:o_q])
    parts.append(jnp.zeros((w.shape[0], LANES - 2 * N_HEADS_A), w.dtype))
    return jnp.concatenate(parts, axis=1)


CHUNK = 128
INV_BASE = 16
HEADS = tuple(range(N_HEADS_A))


def _tri_inverse_heads(a_list, row, col):
    eye = jnp.where(row == col, 1.0, 0.0)
    blk = lambda idx, size: jnp.right_shift(idx, size.bit_length() - 1)
    same16 = blk(row, INV_BASE) == blk(col, INV_BASE)
    ad = [jnp.where(same16, a, 0.0) for a in a_list]
    p = [eye - x for x in ad]
    xs = [_split(x) for x in ad]
    x = [_mm3s(s, s) for s in xs]
    xs = [_split(v) for v in x]
    p = [pv + _mm3s(_split(pv), xx) for pv, xx in zip(p, xs)]
    xb = [s[0] for s in xs]
    n = 4
    while n < INV_BASE:
        xb = [_dot(v, v).astype(bf16) for v in xb]
        p = [pv + _dot(pv.astype(bf16), xx) for pv, xx in zip(p, xb)]
        n *= 2
    s = INV_BASE
    while s < CHUNK:
        pair = (blk(row, 2 * s) == blk(col, 2 * s)) & (blk(row, s) != blk(col, s))
        es = [jnp.where(pair, a, 0.0).astype(bf16) for a in a_list]
        ps = [_split(v) for v in p]
        pe = [_split(_mm2s(pp, ee)) for pp, ee in zip(ps, es)]
        p = [pv - _mm3s(x, pp) for pv, x, pp in zip(p, pe, ps)]
        s *= 2
    return [_split(v) for v in p]


def _delta_prompt_body(cs_ref, ab_ref, z_ref, nega_ref, dtb_ref, gn_ref, s0_ref,
                       o_ref, s_out_ref, s_ref, *, nchunk):
    c = pl.program_id(1)

    @pl.when(c == 0)
    def _():
        s_ref[...] = s0_ref[0]

    items = [(ck, h) for ck in range(nchunk) for h in HEADS]
    rows = lambda ck: pl.ds(ck * CHUNK, CHUNK)
    log_a = [nega_ref[...] * _softplus(ab_ref[0, rows(ck), :] + dtb_ref[...])
             for ck in range(nchunk)]
    beta_all = [_sigmoid(ab_ref[0, rows(ck), :]) for ck in range(nchunk)]

    row = lax.broadcasted_iota(jnp.int32, (CHUNK, CHUNK), 0)
    col = lax.broadcasted_iota(jnp.int32, (CHUNK, CHUNK), 1)
    causal = row >= col
    strict = row > col
    ltri = jnp.where(causal, 1.0, 0.0).astype(bf16)

    def l2n(x, scale):
        return x * (lax.rsqrt(jnp.sum(x * x, axis=-1, keepdims=True) + EPS) * scale)

    def part(ck, h, which):
        lo = which * WIDTH_A + h * HEAD_DIM_A
        return cs_ref[0, rows(ck), lo:lo + HEAD_DIM_A]

    q = [l2n(part(ck, h, 0), HEAD_DIM_A ** -0.5) for ck, h in items]
    k = [l2n(part(ck, h, 1), 1.0) for ck, h in items]
    v = [part(ck, h, 2) for ck, h in items]
    beta = [beta_all[ck][:, N_HEADS_A + h:N_HEADS_A + h + 1] for ck, h in items]
    g3 = [_split3(jnp.broadcast_to(log_a[ck][:, h:h + 1], (CHUNK, HEAD_DIM_A))) for ck, h in items]
    cum = [_dot(ltri, t[0]) + _dot(ltri, t[1]) + _dot(ltri, t[2]) for t in g3]
    decay = [jnp.exp(jnp.where(causal, x - x.T, NEG)) for x in cum]
    e_g = [jnp.exp(x) for x in cum]
    g_last = [x[CHUNK - 1:CHUNK, :] for x in cum]
    kb = [a * b for a, b in zip(k, beta)]
    k_bf = [x.astype(bf16) for x in k]
    a_mat = [jnp.where(strict, _dot(x.astype(bf16), y, NT) * d, 0.0)
             for x, y, d in zip(kb, k_bf, decay)]
    t_inv = _tri_inverse_heads(a_mat, row, col)
    w = [_mm2s(t, (x * e).astype(bf16)) for t, x, e in zip(t_inv, kb, e_g)]
    u = [_mm2s(t, (x * b).astype(bf16)) for t, x, b in zip(t_inv, v, beta)]
    qk = [_dot(x.astype(bf16), y, NT) * d for x, y, d in zip(q, k_bf, decay)]
    qg = [x * e for x, e in zip(q, e_g)]
    kd = [x * jnp.exp(gl - cm) for x, gl, cm in zip(k, g_last, cum)]
    s_cur = [s_ref[h] for h in HEADS]
    for ck in range(nchunk):
        n0 = ck * N_HEADS_A
        s_sp = [_split(x) for x in s_cur]
        v_new = [u[n0 + h] - _mm3s(_split(w[n0 + h]), s_sp[h]) for h in HEADS]
        o = [_dot(qg[n0 + h].astype(bf16), s_sp[h][0]) + _mm1(qk[n0 + h], v_new[h]) for h in HEADS]
        s_cur = [s_cur[h] * jnp.exp(g_last[n0 + h]) + _mm3(kd[n0 + h], v_new[h], TN) for h in HEADS]
        for h in HEADS:
            lo = h * HEAD_DIM_A
            zz = z_ref[0, rows(ck), lo:lo + HEAD_DIM_A]
            o_ref[0, rows(ck), lo:lo + HEAD_DIM_A] = _rms(o[h], gn_ref[...]) * (zz * _sigmoid(zz))
    for h in HEADS:
        s_ref[h] = s_cur[h]

    @pl.when(c == pl.num_programs(1) - 1)
    def _():
        s_out_ref[0] = s_ref[...]


DELTA_CHUNKS_PER_STEP = 2


def _delta_prompt(cs, ab, z, nega, dtb, gn, s0):
    b, t, _ = cs.shape
    nchunk = DELTA_CHUNKS_PER_STEP if t % (DELTA_CHUNKS_PER_STEP * CHUNK) == 0 else 1
    tb = nchunk * CHUNK
    assert t % tb == 0
    return pl.pallas_call(
        functools.partial(_delta_prompt_body, nchunk=nchunk),
        grid=(b, t // tb),
        in_specs=[pl.BlockSpec((1, tb, 3 * WIDTH_A), lambda i, c: (i, c, 0)),
                  pl.BlockSpec((1, tb, LANES), lambda i, c: (i, c, 0)),
                  pl.BlockSpec((1, tb, WIDTH_A), lambda i, c: (i, c, 0)),
                  pl.BlockSpec((1, LANES), lambda i, c: (0, 0)),
                  pl.BlockSpec((1, LANES), lambda i, c: (0, 0)),
                  pl.BlockSpec((1, HEAD_DIM_A), lambda i, c: (0, 0)),
                  pl.BlockSpec((1, N_HEADS_A, HEAD_DIM_A, HEAD_DIM_A), lambda i, c: (i, 0, 0, 0))],
        out_specs=[pl.BlockSpec((1, tb, WIDTH_A), lambda i, c: (i, c, 0)),
                   pl.BlockSpec((1, N_HEADS_A, HEAD_DIM_A, HEAD_DIM_A), lambda i, c: (i, 0, 0, 0))],
        out_shape=[jax.ShapeDtypeStruct((b, t, WIDTH_A), f32),
                   jax.ShapeDtypeStruct((b, N_HEADS_A, HEAD_DIM_A, HEAD_DIM_A), f32)],
        scratch_shapes=[pltpu.VMEM((N_HEADS_A, HEAD_DIM_A, HEAD_DIM_A), f32)],
        compiler_params=pltpu.CompilerParams(
            dimension_semantics=("arbitrary", "arbitrary"), vmem_limit_bytes=VMEM_LIMIT),
        name="delta_prompt",
    )(cs, ab, z, nega, dtb, gn, s0)


def _delta_sample_body(*refs, nb, t):
    x_ref, st_ref, ab_ref, z_ref, cw_ref, nega_ref, dtb_ref, gn_ref, s0_ref = refs[:9]
    o_ref, s_out_ref = refs[-2:]
    rowi = lax.broadcasted_iota(jnp.int32, (SUBLANES, LANES), 0)
    real = rowi < t
    seqs = range(nb)
    items = [(b, h) for b in seqs for h in HEADS]
    conv = [sum((pltpu.roll(x_ref[b], CONV_W - 1 - j, axis=0)
                 + pltpu.roll(st_ref[0, b], (SUBLANES - j) % SUBLANES, axis=0)) * cw_ref[j:j + 1, :]
                for j in range(CONV_W)) for b in seqs]
    cs = [x * _sigmoid(x) for x in conv]
    log_a = [jnp.where(real, nega_ref[...] * _softplus(ab_ref[b] + dtb_ref[...]), 0.0) for b in seqs]
    beta_all = [jnp.where(real, _sigmoid(ab_ref[b]), 0.0) for b in seqs]

    def part(b, h, which):
        lo = which * WIDTH_A + h * HEAD_DIM_A
        return cs[b][:, lo:lo + HEAD_DIM_A]

    def l2n(x, scale):
        return x * (lax.rsqrt(jnp.sum(x * x, axis=-1, keepdims=True) + EPS) * scale)

    q = [l2n(part(b, h, 0), HEAD_DIM_A ** -0.5) for b, h in items]
    k = [l2n(part(b, h, 1), 1.0) for b, h in items]
    v = [part(b, h, 2) for b, h in items]
    beta = [beta_all[b][:, N_HEADS_A + h:N_HEADS_A + h + 1] for b, h in items]
    gb = [jnp.broadcast_to(log_a[b][:, h:h + 1], (SUBLANES, LANES)) for b, h in items]
    cum = [sum(jnp.where(rowi >= j, x[j:j + 1, :], 0.0) for j in range(t)) for x in gb]
    g_last = [x[t - 1:t, :] for x in cum]
    e_g = [jnp.exp(x) for x in cum]
    kb = [x * y for x, y in zip(k, beta)]
    w = [x * y for x, y in zip(kb, e_g)]
    u = [x * y for x, y in zip(v, beta)]
    dec = [[jnp.exp(jnp.where(rowi >= j, x - x[j:j + 1, :], NEG)) for j in range(t)] for x in cum]
    qk_cols = [[jnp.sum(q[n] * k[n][j:j + 1, :], axis=-1, keepdims=True) * dec[n][j]
                for j in range(t)] for n in range(len(items))]
    a_cols = [[jnp.where(rowi > j, jnp.sum(kb[n] * k[n][j:j + 1, :], axis=-1, keepdims=True)
                         * dec[n][j], 0.0) for j in range(t - 1)] for n in range(len(items))]
    for j in range(t - 1):
        w = [x - a[j] * x[j:j + 1, :] for x, a in zip(w, a_cols)]
        u = [x - a[j] * x[j:j + 1, :] for x, a in zip(u, a_cols)]
    qg = [x * y for x, y in zip(q, e_g)]
    kd = [x * jnp.exp(gl - cm) for x, gl, cm in zip(k, g_last, cum)]
    s_old = [s0_ref[0, b, h] for b, h in items]
    s_sp = [_split(x) for x in s_old]
    v_new = [uu - _mm3s(_split(ww), ss) for uu, ww, ss in zip(u, w, s_sp)]
    o = [_mm3s(_split(x), ss) for x, ss in zip(qg, s_sp)]
    o = [x + sum(c[j] * vn[j:j + 1, :] for j in range(t)) for x, c, vn in zip(o, qk_cols, v_new)]
    s_new = [so * jnp.exp(gl) + _mm3(x, vn, TN) for so, gl, x, vn in zip(s_old, g_last, kd, v_new)]
    for n, (b, h) in enumerate(items):
        lo = h * HEAD_DIM_A
        s_out_ref[0, b, h] = s_new[n]
        zz = z_ref[b, :, lo:lo + HEAD_DIM_A]
        o_ref[b, :, lo:lo + HEAD_DIM_A] = _rms(o[n], gn_ref[...]) * (zz * _sigmoid(zz))


def _delta_sample(x8, st8, ab, z, cw, nega, dtb, gn, s_all, s_prev, layer, t, nb=8):
    b = x8.shape[0]
    assert t + CONV_W - 1 <= SUBLANES
    aliased = s_prev is not None
    state_spec = pl.BlockSpec((1, nb, N_HEADS_A, HEAD_DIM_A, HEAD_DIM_A),
                              lambda i: (layer, i, 0, 0, 0))
    in_specs = [pl.BlockSpec((nb, SUBLANES, 3 * WIDTH_A), lambda i: (i, 0, 0)),
                pl.BlockSpec((1, nb, SUBLANES, 3 * WIDTH_A), lambda i: (layer, i, 0, 0)),
                pl.BlockSpec((nb, SUBLANES, LANES), lambda i: (i, 0, 0)),
                pl.BlockSpec((nb, SUBLANES, WIDTH_A), lambda i: (i, 0, 0)),
                pl.BlockSpec((CONV_W, 3 * WIDTH_A), lambda i: (0, 0)),
                pl.BlockSpec((1, LANES), lambda i: (0, 0)),
                pl.BlockSpec((1, LANES), lambda i: (0, 0)),
                pl.BlockSpec((1, HEAD_DIM_A), lambda i: (0, 0)),
                state_spec]
    args = [x8, st8, ab, z, cw, nega, dtb, gn, s_all]
    if aliased:
        in_specs.append(pl.BlockSpec(memory_space=pl.ANY))
        args.append(s_prev)
    return pl.pallas_call(
        functools.partial(_delta_sample_body, nb=nb, t=t),
        grid=(b // nb,),
        in_specs=in_specs,
        out_specs=[pl.BlockSpec((nb, SUBLANES, WIDTH_A), lambda i: (i, 0, 0)), state_spec],
        out_shape=[jax.ShapeDtypeStruct((b, SUBLANES, WIDTH_A), f32),
                   jax.ShapeDtypeStruct(s_all.shape, f32)],
        input_output_aliases={len(args) - 1: 1} if aliased else {},
        compiler_params=pltpu.CompilerParams(
            dimension_semantics=("arbitrary",), vmem_limit_bytes=VMEM_LIMIT),
        name="delta_sample",
    )(*args)


ATT_TB = 2048
Q_CHUNKS = WIDTH_G // LANES
KV_CHUNKS = 2 * WIDTH_G // LANES


def _attn_prompt_body(q_ref, kvh_ref, kvc_ref, o_ref, l_ref, q_scr, kv_scr, o_scr, l_scr, *,
                      dil, slopes, tb):
    i = pl.program_id(1)
    halo = N_OFF * dil
    for c in range(Q_CHUNKS):
        q_scr[c] = q_ref[0, :, c * LANES:(c + 1) * LANES]
    for c in range(KV_CHUNKS):
        kv_scr[c, pl.ds(0, halo), :] = kvh_ref[0, :, c * LANES:(c + 1) * LANES]
        kv_scr[c, pl.ds(halo, tb), :] = kvc_ref[0, :, c * LANES:(c + 1) * LANES]
    a = lax.broadcasted_iota(jnp.int32, (N_OFF, 2 * N_OFF), 0)
    cc = lax.broadcasted_iota(jnp.int32, (N_OFF, 2 * N_OFF), 1)
    delta = a - cc + N_OFF
    in_win = (delta >= 0) & (delta <= N_OFF)
    bias = [jnp.where(in_win, delta.astype(f32) * (-slopes[h] * dil), NEG)
            for h in range(HEADS_PER_GROUP)]
    upper = lax.broadcasted_iota(jnp.int32, (N_OFF, LANES), 1) >= HEAD_DIM_B
    n_sub = tb // halo

    def sub_block(idx, carry):
        r = idx // n_sub
        u = idx - r * n_sub
        base = r + halo * u
        key_tok = i * tb - halo + base + dil * cc
        in_seq = key_tok >= 0
        cs_ = range(Q_CHUNKS)
        hs = range(HEADS_PER_GROUP)
        qc = [q_scr[c, pl.ds(base, N_OFF, stride=dil), :] * HEAD_DIM_B ** -0.5 for c in cs_]
        kc = [kv_scr[c, pl.ds(base, 2 * N_OFF, stride=dil), :].astype(bf16) for c in cs_]
        vc = [kv_scr[Q_CHUNKS + c, pl.ds(base, 2 * N_OFF, stride=dil), :].astype(bf16) for c in cs_]
        qm = [jnp.where(upper if h % 2 else jnp.logical_not(upper), qc[h // 2], 0.0).astype(bf16)
              for h in hs]
        s = [jnp.where(in_seq, _dot(qm[h], kc[h // 2], NT) + bias[h], NEG) for h in hs]
        m = [jnp.max(s[h], axis=-1, keepdims=True) for h in hs]
        p = [jnp.exp(s[h] - m[h]) for h in hs]
        den = [jnp.sum(p[h], axis=-1, keepdims=True) for h in hs]
        pv = [_dot(p[h].astype(bf16), vc[h // 2]) / den[h] for h in hs]
        lse = [jnp.broadcast_to(m[h] + jnp.log(den[h]), (N_OFF, LANES)) for h in hs]
        for c in cs_:
            o_scr[c, pl.ds(base, N_OFF, stride=dil), :] = jnp.where(upper, pv[2 * c + 1], pv[2 * c])
            l_scr[c, pl.ds(base, N_OFF, stride=dil), :] = jnp.where(upper, lse[2 * c + 1], lse[2 * c])
        return carry

    lax.fori_loop(0, tb // N_OFF, sub_block, 0, unroll=2)
    for c in range(Q_CHUNKS):
        o_ref[0, :, c * LANES:(c + 1) * LANES] = o_scr[c]
        l_ref[0, :, c * LANES:(c + 1) * LANES] = l_scr[c]


def _attn_prompt(q, kv, g):
    b, t, _ = q.shape
    dil = DILATIONS[g]
    halo = N_OFF * dil
    tb = min(ATT_TB, t)
    assert tb % halo == 0 and t % tb == 0
    per = tb // halo
    return pl.pallas_call(
        functools.partial(_attn_prompt_body, dil=dil, slopes=_slopes(g), tb=tb),
        grid=(b, t // tb),
        in_specs=[pl.BlockSpec((1, tb, WIDTH_G), lambda bi, i: (bi, i, 0)),
                  pl.BlockSpec((1, halo, 2 * WIDTH_G),
                               lambda bi, i: (bi, jnp.maximum(i * per - 1, 0), 0)),
                  pl.BlockSpec((1, tb, 2 * WIDTH_G), lambda bi, i: (bi, i, 0))],
        out_specs=[pl.BlockSpec((1, tb, WIDTH_G), lambda bi, i: (bi, i, 0)),
                   pl.BlockSpec((1, tb, WIDTH_G), lambda bi, i: (bi, i, 0))],
        out_shape=[jax.ShapeDtypeStruct((b, t, WIDTH_G), f32),
                   jax.ShapeDtypeStruct((b, t, WIDTH_G), f32)],
        scratch_shapes=[pltpu.VMEM((Q_CHUNKS, tb, LANES), f32),
                        pltpu.VMEM((KV_CHUNKS, halo + tb, LANES), f32),
                        pltpu.VMEM((Q_CHUNKS, tb, LANES), f32),
                        pltpu.VMEM((Q_CHUNKS, tb, LANES), f32)],
        compiler_params=pltpu.CompilerParams(
            dimension_semantics=("arbitrary", "arbitrary"), vmem_limit_bytes=VMEM_LIMIT),
        name=f"attn_prompt_g{g}",
    )(q, kv, kv)


def _sample_consts(dil, t, win):
    rr = lax.broadcasted_iota(jnp.int32, (SUBLANES, win), 1)
    qi = lax.broadcasted_iota(jnp.int32, (SUBLANES, win), 0)
    nn = lax.broadcasted_iota(jnp.int32, (SUBLANES, SUBLANES), 1)
    qn = lax.broadcasted_iota(jnp.int32, (SUBLANES, SUBLANES), 0)
    lane = lax.broadcasted_iota(jnp.int32, (HEAD_DIM_B, LANES), 1)
    e_rows = lax.broadcasted_iota(jnp.int32, (SUBLANES, LANES), 0)
    e_lane = lax.broadcasted_iota(jnp.int32, (SUBLANES, LANES), 1)
    return dict(
        valid=((rr & (dil - 1)) == (qi & (dil - 1))) & (rr >= qi) & (qi < t),
        dist=(win + qi - rr).astype(f32),
        valid_n=(nn <= qn) & (((qn - nn) & (dil - 1)) == 0) & (qn < t),
        dist_n=(qn - nn).astype(f32),
        tail_lanes=lane >= LANES - t,
        real_rows=lax.broadcasted_iota(jnp.int32, (SUBLANES, HEAD_DIM_B), 0) < t,
        place=jnp.where((e_lane == e_rows + (LANES - t)) & (e_rows < t), 1.0, 0.0).astype(bf16))


def _sample_unit(c_in, c_out, q_b, kvn_b, o_b, l_b, cst, slopes, t, win):
    kvn = kvn_b[...]
    qs = q_b[...] * HEAD_DIM_B ** -0.5
    head = lambda a_, h, base=0: a_[:, base + h * HEAD_DIM_B:base + (h + 1) * HEAD_DIM_B]
    k3 = _split3(kvn)
    place = cst["place"]
    new_t = _dot(k3[0], place, TN) + _dot(k3[1], place, TN) + _dot(k3[2], place, TN)
    hs = range(HEADS_PER_GROUP)
    k_t = [c_in[0, h] for h in hs]
    v_t = [c_in[1, h] for h in hs]
    q8 = [head(qs, h).astype(bf16) for h in hs]
    s = [jnp.where(cst["valid"], _dot(q8[h], k_t[h].astype(bf16)) - slopes[h] * cst["dist"], NEG)
         for h in hs]
    sn = [jnp.where(cst["valid_n"],
                    _dot(q8[h], head(kvn, h).astype(bf16), NT) - slopes[h] * cst["dist_n"], NEG)
          for h in hs]
    m = [jnp.maximum(jnp.max(s[h], axis=-1, keepdims=True),
                     jnp.max(sn[h], axis=-1, keepdims=True)) for h in hs]
    p = [jnp.exp(s[h] - m[h]) for h in hs]
    pn = [jnp.exp(sn[h] - m[h]) for h in hs]
    den = [jnp.sum(p[h], axis=-1, keepdims=True) + jnp.sum(pn[h], axis=-1, keepdims=True)
           for h in hs]
    acc = [_dot(p[h].astype(bf16), v_t[h].astype(bf16), NT) + _mm1(pn[h], head(kvn, h, WIDTH_G))
           for h in hs]
    for h in hs:
        o_b[:, h * HEAD_DIM_B:(h + 1) * HEAD_DIM_B] = jnp.where(cst["real_rows"], acc[h] / den[h], 0.0)
        l_b[:, h * HEAD_DIM_B:(h + 1) * HEAD_DIM_B] = jnp.broadcast_to(
            m[h] + jnp.log(den[h]), (SUBLANES, HEAD_DIM_B))
        for kv, x in ((0, k_t[h]), (1, v_t[h])):
            rolled = pltpu.roll(x, win - t, axis=1)
            fresh = new_t[(kv * HEADS_PER_GROUP + h) * HEAD_DIM_B:
                          (kv * HEADS_PER_GROUP + h + 1) * HEAD_DIM_B, :]
            if win > LANES:
                c_out[kv, h, :, pl.ds(0, win - LANES)] = rolled[:, :win - LANES]
            c_out[kv, h, :, pl.ds(win - LANES, LANES)] = jnp.where(
                cst["tail_lanes"], fresh, rolled[:, win - LANES:])


def _attn_sample_body(*refs, dil, slopes, nb, t, win):
    q_ref, kvn_ref, c_ref = refs[:3]
    o_ref, l_ref, cn_ref = refs[-3:]
    cst = _sample_consts(dil, t, win)

    def one_sequence(b, carry):
        _sample_unit(c_ref.at[0, b], cn_ref.at[0, b], q_ref.at[b], kvn_ref.at[b], o_ref.at[b],
                     l_ref.at[b], cst, slopes, t, win)
        return carry

    lax.fori_loop(0, nb, one_sequence, 0, unroll=2 if nb % 2 == 0 else 1)


def _attn_sample(q, kvn, cache_t, prev, layer, g, t):
    bsz, win = cache_t.shape[1], cache_t.shape[-1]
    dil = DILATIONS[g]
    nb = max(1, min(16, 2048 // win))
    aliased = prev is not None
    cache_spec = pl.BlockSpec((1, nb, 2, HEADS_PER_GROUP, HEAD_DIM_B, win),
                              lambda i: (layer, i, 0, 0, 0, 0))
    out_spec = pl.BlockSpec((nb, SUBLANES, WIDTH_G), lambda i: (i, 0, 0))
    in_specs = [out_spec, pl.BlockSpec((nb, SUBLANES, 2 * WIDTH_G), lambda i: (i, 0, 0)), cache_spec]
    args = [q, kvn, cache_t]
    if aliased:
        in_specs.append(pl.BlockSpec(memory_space=pl.ANY))
        args.append(prev)
    o, l, cn = pl.pallas_call(
        functools.partial(_attn_sample_body, dil=dil, slopes=_slopes(g), nb=nb, t=t, win=win),
        grid=(bsz // nb,),
        in_specs=in_specs,
        out_specs=[out_spec, out_spec, cache_spec],
        out_shape=[jax.ShapeDtypeStruct((bsz, SUBLANES, WIDTH_G), f32),
                   jax.ShapeDtypeStruct((bsz, SUBLANES, WIDTH_G), f32),
                   jax.ShapeDtypeStruct(cache_t.shape, f32)],
        input_output_aliases={len(args) - 1: 2} if aliased else {},
        compiler_params=pltpu.CompilerParams(
            dimension_semantics=("arbitrary",), vmem_limit_bytes=VMEM_LIMIT),
        name=f"attn_sample_g{g}",
    )(*args)
    return o, l, cn


class _CachePipe:
    def __init__(self, c_hbm, cn_hbm, in_buf, out_buf, in_sem, out_sem, layer, units):
        self.c_hbm, self.cn_hbm, self.in_buf, self.out_buf = c_hbm, cn_hbm, in_buf, out_buf
        self.in_sem, self.out_sem, self.layer, self.units = in_sem, out_sem, layer, units
        self.in_slots = in_buf.shape[0]
        self.i = pl.program_id(0)
        self.n = pl.num_programs(0)

    def _fetch(self, u, slot):
        return pltpu.make_async_copy(self.c_hbm.at[self.layer, u], self.in_buf.at[slot],
                                     self.in_sem.at[slot])

    def _flush(self, u, slot):
        return pltpu.make_async_copy(self.out_buf.at[slot], self.cn_hbm.at[self.layer, u],
                                     self.out_sem.at[slot])

    def prologue(self):
        @pl.when(self.i == 0)
        def _():
            for u0 in range(self.in_slots - 1):
                self._fetch(u0, u0).start()

    def open(self, j):
        u = self.i * self.units + j
        slot = lax.rem(u, self.in_slots)
        self._fetch(u, slot).wait()
        ahead = u + self.in_slots - 1

        @pl.when(ahead < self.n * self.units)
        def _():
            self._fetch(ahead, lax.rem(ahead, self.in_slots)).start()

        @pl.when(self.i >= 1)
        def _():
            self._flush(u - self.units, j).wait()

        return self.in_buf.at[slot], self.out_buf.at[j]

    def close(self, j):
        self._flush(self.i * self.units + j, j).start()

    def epilogue(self):
        @pl.when(self.i == self.n - 1)
        def _():
            for j in range(self.units):
                self._flush(self.i * self.units + j, j).wait()

    @staticmethod
    def scratch(units, in_slots, win):
        slab = (2, HEADS_PER_GROUP, HEAD_DIM_B, win)
        return [pltpu.VMEM((in_slots,) + slab, f32), pltpu.VMEM((units,) + slab, f32),
                pltpu.SemaphoreType.DMA((in_slots,)), pltpu.SemaphoreType.DMA((units,))]


FF_CHUNK = 1024


def _tail_front(x_ref, oa_ref, o_refs, l_refs, gate_ref, wa_ref, wb_ref, wo_ref, n2_ref):
    l0, l1, l2 = (r[...] for r in l_refs)
    lm = jnp.maximum(jnp.maximum(l0, l1), l2)
    e0, e1, e2 = jnp.exp(l0 - lm), jnp.exp(l1 - lm), jnp.exp(l2 - lm)
    den = e0 + e1 + e2
    ob = (e0 / den) * o_refs[0][...] + (e1 / den) * o_refs[1][...] + (e2 / den) * o_refs[2][...]
    ya = _mm1(oa_ref[...], wa_ref[...])
    yb = _mm1(ob, wb_ref[...])
    merged = (_sigmoid(gate_ref[:, :D_MODEL].astype(f32)) * ya
              + _sigmoid(gate_ref[:, D_MODEL:].astype(f32)) * yb)
    x1 = x_ref[...] + _mm1(merged, wo_ref[...])
    return x1, _rms(x1, n2_ref[...]).astype(bf16)


def _tail_ff(acc, h2, wu_ref, wd_ref, chunks):
    for j in chunks:
        up = _dot(h2, wu_ref[:, j * FF_CHUNK:(j + 1) * FF_CHUNK])
        act = jnp.square(jnp.maximum(up, 0.0)).astype(bf16)
        acc = acc + _dot(act, wd_ref[j * FF_CHUNK:(j + 1) * FF_CHUNK, :])
    return acc


def _tail_body(x_ref, oa_ref, o0_ref, o1_ref, o2_ref, l0_ref, l1_ref, l2_ref, gate_ref,
               wa_ref, wb_ref, wo_ref, n2_ref, wu_ref, wd_ref, fn_ref, y_ref, *, final):
    x1, h2 = _tail_front(x_ref, oa_ref, (o0_ref, o1_ref, o2_ref), (l0_ref, l1_ref, l2_ref),
                         gate_ref, wa_ref, wb_ref, wo_ref, n2_ref)
    acc = _tail_ff(x1, h2, wu_ref, wd_ref, range(D_FF // FF_CHUNK))
    if final:
        acc = _rms(acc, fn_ref[...])
    y_ref[...] = acc


def _tail_specs(tm):
    tok = lambda wd_: pl.BlockSpec((tm, wd_), lambda i: (i, 0))
    const = lambda shape: pl.BlockSpec(shape, lambda i: (0, 0), pipeline_mode=pl.Buffered(1))
    in_specs = [tok(D_MODEL), tok(WIDTH_A)] + [tok(WIDTH_G)] * 6 + [
        tok(2 * D_MODEL), const((WIDTH_A, D_MODEL)), const((WIDTH_G, D_MODEL)),
        const((D_MODEL, D_MODEL)), const((1, D_MODEL)), const((D_MODEL, D_FF)),
        const((D_FF, D_MODEL)), const((1, D_MODEL))]
    return in_specs, tok(D_MODEL)


def _tail(x, oa, outs, lses, gate, wa, wb, wo, n2, wu, wd, fn, final, tm=512):
    m = x.shape[0]
    tm = min(tm, m)
    assert m % tm == 0
    in_specs, out_spec = _tail_specs(tm)
    return pl.pallas_call(
        functools.partial(_tail_body, final=final),
        grid=(m // tm,),
        in_specs=in_specs,
        out_specs=out_spec,
        out_shape=jax.ShapeDtypeStruct((m, D_MODEL), f32),
        compiler_params=pltpu.CompilerParams(
            dimension_semantics=("arbitrary",), vmem_limit_bytes=VMEM_LIMIT),
        name="tail",
    )(x, oa, *outs, *lses, gate, wa, wb, wo, n2, wu, wd, fn)


FUSED_UNITS = 2
IN_SLOTS = 3
N_TAIL_IN = 16


def _tail_cache_body(*refs, final, dil, slopes, t, win, layer, aliased):
    (x_ref, oa_ref, o0_ref, o1_ref, o2_ref, l0_ref, l1_ref, l2_ref, gate_ref,
     wa_ref, wb_ref, wo_ref, n2_ref, wu_ref, wd_ref, fn_ref) = refs[:N_TAIL_IN]
    q_ref, kvn_ref, c_hbm = refs[N_TAIL_IN:N_TAIL_IN + 3]
    n_in = N_TAIL_IN + (4 if aliased else 3)
    y_ref, os_ref, ls_ref, cn_hbm = refs[n_in:n_in + 4]
    in_buf, out_buf, in_sem, out_sem = refs[n_in + 4:]
    cst = _sample_consts(dil, t, win)
    pipe = _CachePipe(c_hbm, cn_hbm, in_buf, out_buf, in_sem, out_sem, layer, FUSED_UNITS)
    pipe.prologue()
    n_ff = D_FF // FF_CHUNK
    first_part = n_ff // 2
    acc = h2 = None
    for j in range(FUSED_UNITS):
        c_in, c_out = pipe.open(j)
        _sample_unit(c_in, c_out, q_ref.at[j], kvn_ref.at[j], os_ref.at[j], ls_ref.at[j], cst,
                     slopes, t, win)
        if j == 0:
            x1, h2 = _tail_front(x_ref, oa_ref, (o0_ref, o1_ref, o2_ref),
                                 (l0_ref, l1_ref, l2_ref), gate_ref, wa_ref, wb_ref, wo_ref, n2_ref)
            acc = _tail_ff(x1, h2, wu_ref, wd_ref, range(first_part))
        else:
            acc = _tail_ff(acc, h2, wu_ref, wd_ref, range(first_part, n_ff))
            if final:
                acc = _rms(acc, fn_ref[...])
            y_ref[...] = acc
        pipe.close(j)
    pipe.epilogue()


def _tail_with_cache(x, oa, outs, lses, gate, wa, wb, wo, n2, wu, wd, fn, final,
                     q, kvn, cache_t, prev, layer, g, t):
    m = x.shape[0]
    bsz, win = cache_t.shape[1], cache_t.shape[-1]
    assert bsz % FUSED_UNITS == 0 and m % (bsz // FUSED_UNITS) == 0 and bsz >= IN_SLOTS
    steps = bsz // FUSED_UNITS
    tm = m // steps
    assert tm % SUBLANES == 0
    in_specs, y_spec = _tail_specs(tm)
    o_spec = pl.BlockSpec((FUSED_UNITS, SUBLANES, WIDTH_G), lambda i: (i, 0, 0))
    hbm = pl.BlockSpec(memory_space=pl.ANY)
    in_specs = in_specs + [o_spec, pl.BlockSpec((FUSED_UNITS, SUBLANES, 2 * WIDTH_G),
                                                lambda i: (i, 0, 0)), hbm]
    args = [x, oa, *outs, *lses, gate, wa, wb, wo, n2, wu, wd, fn, q, kvn, cache_t]
    aliased = prev is not None
    if aliased:
        in_specs.append(hbm)
        args.append(prev)
    y, o, l, cn = pl.pallas_call(
        functools.partial(_tail_cache_body, final=final, dil=DILATIONS[g], slopes=_slopes(g), t=t,
                          win=win, layer=layer, aliased=aliased),
        grid=(steps,),
        in_specs=in_specs,
        out_specs=[y_spec, o_spec, o_spec, hbm],
        out_shape=[jax.ShapeDtypeStruct((m, D_MODEL), f32),
                   jax.ShapeDtypeStruct((bsz, SUBLANES, WIDTH_G), f32),
                   jax.ShapeDtypeStruct((bsz, SUBLANES, WIDTH_G), f32),
                   jax.ShapeDtypeStruct(cache_t.shape, f32)],
        scratch_shapes=_CachePipe.scratch(FUSED_UNITS, IN_SLOTS, win),
        input_output_aliases={len(args) - 1: 3} if aliased else {},
        compiler_params=pltpu.CompilerParams(
            dimension_semantics=("arbitrary",), vmem_limit_bytes=VMEM_LIMIT),
        name="tail_cache",
    )(*args)
    return y, o, l, cn


INPROJ_UNITS = 4
INPROJ_SEG_GROUPS = ((0,), (1, 2, 3, 4), (5, 6, 7), (8, 9))
SEG_OFFSETS = tuple(sum(SEG_WIDTHS[:s]) for s in range(len(SEG_WIDTHS)))


def _inproj_cache_body(*refs, tm, tiles_per_seq, dil, slopes, t, win, layer, aliased):
    x_ref, g_ref, w_ref, cw_ref, q_ref, kvn_ref, c_hbm = refs[:7]
    n_in = 8 if aliased else 7
    nseg = len(SEG_WIDTHS)
    out_refs = refs[n_in:n_in + nseg]
    last_ref, os_ref, ls_ref, cn_hbm = refs[n_in + nseg:n_in + nseg + 4]
    xs_ref, in_buf, out_buf, in_sem, out_sem = refs[n_in + nseg + 4:]
    cst = _sample_consts(dil, t, win)
    pipe = _CachePipe(c_hbm, cn_hbm, in_buf, out_buf, in_sem, out_sem, layer, INPROJ_UNITS)
    pipe.prologue()

    @pl.when(lax.rem(pl.program_id(0), tiles_per_seq) == 0)
    def _():
        xs_ref[pl.ds(0, SUBLANES), :] = jnp.zeros((SUBLANES, SEG_WIDTHS[0]), f32)

    h = _rms(x_ref[...], g_ref[...]).astype(bf16)
    for j in range(INPROJ_UNITS):
        c_in, c_out = pipe.open(j)
        for s in INPROJ_SEG_GROUPS[j]:
            y = _dot(h, w_ref[:, SEG_OFFSETS[s]:SEG_OFFSETS[s] + SEG_WIDTHS[s]])
            if s == 0:
                xs_ref[pl.ds(SUBLANES, tm), :] = y
                conv = xs_ref[pl.ds(SUBLANES - CONV_W + 1, tm), :] * cw_ref[0:1, :]
                for k in range(1, CONV_W):
                    conv = conv + xs_ref[pl.ds(SUBLANES - CONV_W + 1 + k, tm), :] * cw_ref[k:k + 1, :]
                last = xs_ref[pl.ds(tm, SUBLANES), :]
                xs_ref[pl.ds(0, SUBLANES), :] = last
                last_ref[0] = last
                y = conv * _sigmoid(conv)
            out_refs[s][...] = y.astype(out_refs[s].dtype)
        _sample_unit(c_in, c_out, q_ref.at[j], kvn_ref.at[j], os_ref.at[j], ls_ref.at[j], cst,
                     slopes, t, win)
        pipe.close(j)
    pipe.epilogue()


def _inproj_with_cache(x, g, w, conv_w, seq_len, q, kvn, cache_t, prev, layer, grp, t):
    m = x.shape[0]
    bsz, win = cache_t.shape[1], cache_t.shape[-1]
    assert bsz % INPROJ_UNITS == 0 and m % (bsz // INPROJ_UNITS) == 0 and bsz >= IN_SLOTS
    steps = bsz // INPROJ_UNITS
    tm = m // steps
    assert seq_len % tm == 0 and tm % SUBLANES == 0
    per = seq_len // tm
    o_spec = pl.BlockSpec((INPROJ_UNITS, SUBLANES, WIDTH_G), lambda i: (i, 0, 0))
    hbm = pl.BlockSpec(memory_space=pl.ANY)
    in_specs = [pl.BlockSpec((tm, D_MODEL), lambda i: (i, 0)),
                pl.BlockSpec((1, D_MODEL), lambda i: (0, 0)),
                pl.BlockSpec((D_MODEL, N_PERM), lambda i: (0, 0), pipeline_mode=pl.Buffered(1)),
                pl.BlockSpec((CONV_W, SEG_WIDTHS[0]), lambda i: (0, 0)),
                o_spec, pl.BlockSpec((INPROJ_UNITS, SUBLANES, 2 * WIDTH_G), lambda i: (i, 0, 0)), hbm]
    args = [x, g, w, conv_w, q, kvn, cache_t]
    aliased = prev is not None
    if aliased:
        in_specs.append(hbm)
        args.append(prev)
    nseg = len(SEG_WIDTHS)
    out_specs = [pl.BlockSpec((tm, wd), lambda i: (i, 0)) for wd in SEG_WIDTHS] + [
        pl.BlockSpec((1, SUBLANES, SEG_WIDTHS[0]), lambda i: (i // per, 0, 0)), o_spec, o_spec, hbm]
    out_shape = [jax.ShapeDtypeStruct((m, wd), dt) for wd, dt in zip(SEG_WIDTHS, SEG_DTYPES)] + [
        jax.ShapeDtypeStruct((m // seq_len, SUBLANES, SEG_WIDTHS[0]), f32),
        jax.ShapeDtypeStruct((bsz, SUBLANES, WIDTH_G), f32),
        jax.ShapeDtypeStruct((bsz, SUBLANES, WIDTH_G), f32),
        jax.ShapeDtypeStruct(cache_t.shape, f32)]
    res = pl.pallas_call(
        functools.partial(_inproj_cache_body, tm=tm, tiles_per_seq=per, dil=DILATIONS[grp],
                          slopes=_slopes(grp), t=t, win=win, layer=layer, aliased=aliased),
        grid=(steps,),
        in_specs=in_specs,
        out_specs=out_specs,
        out_shape=out_shape,
        scratch_shapes=[pltpu.VMEM((tm + SUBLANES, SEG_WIDTHS[0]), f32)]
        + _CachePipe.scratch(INPROJ_UNITS, IN_SLOTS, win),
        input_output_aliases={len(args) - 1: nseg + 3} if aliased else {},
        compiler_params=pltpu.CompilerParams(
            dimension_semantics=("arbitrary",), vmem_limit_bytes=VMEM_LIMIT),
        name="inproj_cache",
    )(*args)
    return res[:nseg + 1], res[nseg + 1], res[nseg + 2], res[nseg + 3]


def _lane_row(vals):
    return jnp.zeros((1, LANES), f32).at[0, :vals.shape[0]].set(vals.astype(f32))


def _layer_params(l, norm1, w_in, conv_w, a_log, dt_bias, gnorm_a, w_br_a, w_br_b, w_out, norm2,
                  w_up, w_down):
    return dict(
        n1=norm1[l].reshape(1, D_MODEL), w_in=_permute_w_in(w_in[l]), cw=conv_w[l],
        nega=_lane_row(-jnp.exp(a_log[l].astype(f32))), dtb=_lane_row(dt_bias[l]),
        gn=gnorm_a[l].reshape(1, HEAD_DIM_A).astype(f32),
        wa=w_br_a[l].astype(bf16), wb=w_br_b[l].astype(bf16), wo=w_out[l].astype(bf16),
        n2=norm2[l].reshape(1, D_MODEL), wu=w_up[l].astype(bf16), wd=w_down[l].astype(bf16))


def _layer(xp, xs, ts, p, fn, final, layer, caches_t, prev_caches, s_all, s_prev, conv8):
    bs = xs.shape[0]
    xsf = xs.reshape(bs * SUBLANES, D_MODEL)
    qkv, z_s, q0s, q1s, q2s, kv0s, kv1s, kv2s, gate_s, ab_s = _inproj(xsf, p["n1"], p["w_in"])
    rows8 = lambda a_: a_.reshape(bs, SUBLANES, a_.shape[-1])
    qkv8 = rows8(qkv)

    bp, tp, _ = xp.shape
    xpf = xp.reshape(bp * tp, D_MODEL)
    (cs, z, q0, q1, q2, kv0, kv1, kv2, gate_p, ab, last), o1s, l1s, cn1 = _inproj_with_cache(
        xpf, p["n1"], p["w_in"], p["cw"], tp, rows8(q1s), rows8(kv1s), caches_t[1],
        prev_caches[1], layer, 1, ts)
    s0 = jnp.zeros((bp, N_HEADS_A, HEAD_DIM_A, HEAD_DIM_A), f32)
    oa_p, s_new_p = _delta_prompt(cs.reshape(bp, tp, 3 * WIDTH_A), ab.reshape(bp, tp, LANES),
                                  z.reshape(bp, tp, WIDTH_A), p["nega"], p["dtb"], p["gn"], s0)
    outs_p, lses_p, wins_p = [], [], []
    for g, (qg, kvg) in enumerate(((q0, kv0), (q1, kv1), (q2, kv2))):
        kv3 = kvg.reshape(bp, tp, 2 * WIDTH_G)
        o, l = _attn_prompt(qg.reshape(bp, tp, WIDTH_G), kv3, g)
        outs_p.append(o.reshape(bp * tp, WIDTH_G))
        lses_p.append(l.reshape(bp * tp, WIDTH_G))
        keep = min(WINDOWS[g], tp)
        wins_p.append(kv3[:, tp - keep:].reshape(bp, keep, 2, HEADS_PER_GROUP, HEAD_DIM_B))
    conv_p = last[:, SUBLANES - (CONV_W - 1):]

    oa_s, s_stack = _delta_sample(qkv8, conv8, rows8(ab_s), rows8(z_s), p["cw"], p["nega"], p["dtb"],
                                  p["gn"], s_all, s_prev, layer, ts)
    o0s, l0s, cn0 = _attn_sample(rows8(q0s), rows8(kv0s), caches_t[0], prev_caches[0], layer, 0, ts)
    flat = lambda a_: a_.reshape(bs * SUBLANES, WIDTH_G)
    outs_s, lses_s, new_caches = [flat(o0s), flat(o1s)], [flat(l0s), flat(l1s)], [cn0, cn1]

    g = N_GROUPS - 1
    yp, o, l, cn = _tail_with_cache(xpf, oa_p.reshape(bp * tp, WIDTH_A), outs_p, lses_p, gate_p,
                                    p["wa"], p["wb"], p["wo"], p["n2"], p["wu"], p["wd"], fn, final,
                                    rows8(q2s), rows8(kv2s), caches_t[g], prev_caches[g], layer, g, ts)
    outs_s.append(flat(o))
    lses_s.append(flat(l))
    new_caches.append(cn)
    ys = _tail(xsf, oa_s.reshape(bs * SUBLANES, WIDTH_A), outs_s, lses_s, gate_s, p["wa"], p["wb"],
               p["wo"], p["n2"], p["wu"], p["wd"], fn, final)
    conv_s = qkv8[:, ts - (CONV_W - 1):ts]
    return (yp.reshape(bp, tp, D_MODEL), ys.reshape(bs, SUBLANES, D_MODEL), wins_p, s_new_p, conv_p,
            new_caches, s_stack, conv_s)


def kernel(x_prompt, x_sample, cache_win0, cache_win1, cache_win2, state_delta, state_conv, norm1,
           w_in, conv_w, a_log, dt_bias, gnorm_a, w_br_a, w_br_b, w_out, norm2, w_up, w_down,
           final_norm):
    depth = w_in.shape[0]
    ts = x_sample.shape[1]
    assert ts + CONV_W - 1 <= SUBLANES
    pad_rows = lambda a_, axis: jnp.pad(
        a_, [(0, SUBLANES - a_.shape[axis]) if d == axis else (0, 0) for d in range(a_.ndim)])
    conv8 = pad_rows(state_conv.astype(f32), 2)
    fn = final_norm.reshape(1, D_MODEL)
    caches_t = [c.transpose(0, 1, 3, 4, 5, 2) for c in (cache_win0, cache_win1, cache_win2)]
    new_caches = [None] * N_GROUPS
    s_stack = None
    xp, xs = x_prompt, pad_rows(x_sample, 1)
    wins_p = [[] for _ in range(N_GROUPS)]
    delta_p, conv_p, conv_s = [], [], []
    for l in range(depth):
        p = _layer_params(l, norm1, w_in, conv_w, a_log, dt_bias, gnorm_a, w_br_a, w_br_b, w_out,
                          norm2, w_up, w_down)
        xp, xs, wp, sp, cp, new_caches, s_stack, cs = _layer(
            xp, xs, ts, p, fn, l == depth - 1, l, caches_t, new_caches, state_delta, s_stack, conv8)
        for g in range(N_GROUPS):
            wins_p[g].append(wp[g])
        delta_p.append(sp)
        conv_p.append(cp)
        conv_s.append(cs)
    st = jnp.stack
    wins_s = [c.transpose(0, 1, 5, 2, 3, 4) for c in new_caches]
    return (xp, xs[:, :ts], st(wins_p[0]), st(wins_p[1]), st(wins_p[2]), st(delta_p), st(conv_p),
            wins_s[0], wins_s[1], wins_s[2], s_stack, st(conv_s))
```

```python
import functools

import jax
import jax.numpy as jnp
from jax import lax
from jax.experimental import pallas as pl
from jax.experimental.pallas import tpu as pltpu

f32 = jnp.float32
bf16 = jnp.bfloat16

D_MODEL = 1024
N_HEADS_A = 4
HEAD_DIM_A = 128
WIDTH_A = N_HEADS_A * HEAD_DIM_A
CONV_W = 4
WINDOWS = (128, 512, 2048)
DILATIONS = (1, 4, 16)
N_GROUPS = 3
HEADS_PER_GROUP = 4
HEAD_DIM_B = 64
N_HEADS_B = N_GROUPS * HEADS_PER_GROUP
WIDTH_G = HEADS_PER_GROUP * HEAD_DIM_B
D_FF = 4 * D_MODEL
EPS = 1e-6
N_OFF = 128
NEG = -1e30

SUBLANES = 8
LANES = 128
VMEM_LIMIT = 56 * 1024 * 1024

SEG_WIDTHS = (3 * WIDTH_A, WIDTH_A, WIDTH_G, WIDTH_G, WIDTH_G,
              2 * WIDTH_G, 2 * WIDTH_G, 2 * WIDTH_G, 2 * D_MODEL, LANES)
N_PERM = sum(SEG_WIDTHS)
SEG_DTYPES = (f32,) * 8 + (bf16, f32)

NN = (((1,), (0,)), ((), ()))
NT = (((1,), (1,)), ((), ()))
TN = (((0,), (0,)), ((), ()))


def _slopes(g):
    return tuple(2.0 ** (-8.0 * (g * HEADS_PER_GROUP + h + 1) / N_HEADS_B)
                 for h in range(HEADS_PER_GROUP))


def _dot(a, b, dims=NN):
    return lax.dot_general(a, b, dims, preferred_element_type=f32)


def _mm1(a, b, dims=NN):
    return _dot(a.astype(bf16), b.astype(bf16), dims)


def _split(a):
    hi = a.astype(bf16)
    lo = (a - hi.astype(f32)).astype(bf16)
    return hi, lo


def _split3(a):
    hi = a.astype(bf16)
    r = a - hi.astype(f32)
    mid = r.astype(bf16)
    lo = (r - mid.astype(f32)).astype(bf16)
    return hi, mid, lo


def _mm3s(a, b, dims=NN):
    return _dot(a[0], b[0], dims) + _dot(a[0], b[1], dims) + _dot(a[1], b[0], dims)


def _mm3(a, b, dims=NN):
    return _mm3s(_split(a), _split(b), dims)


def _mm2s(a, b_hi, dims=NN):
    return _dot(a[0], b_hi, dims) + _dot(a[1], b_hi, dims)


def _sigmoid(x):
    return 1.0 / (1.0 + jnp.exp(-x))


def _softplus(x):
    return jnp.maximum(x, 0.0) + jnp.log1p(jnp.exp(-jnp.abs(x)))


def _rms(x, g):
    return x * lax.rsqrt(jnp.mean(x * x, axis=-1, keepdims=True) + EPS) * g


def _inproj_body(x_ref, g_ref, w_ref, *out_refs):
    h = _rms(x_ref[...], g_ref[...]).astype(bf16)
    off = 0
    for o_ref, width in zip(out_refs, SEG_WIDTHS):
        o_ref[...] = _dot(h, w_ref[:, off:off + width]).astype(o_ref.dtype)
        off += width


def _inproj_conv_body(x_ref, g_ref, w_ref, cw_ref, *refs, tm, tiles_per_seq):
    out_refs, last_ref, xs_ref = refs[:len(SEG_WIDTHS)], refs[-2], refs[-1]
    i = pl.program_id(0)

    @pl.when(lax.rem(i, tiles_per_seq) == 0)
    def _():
        xs_ref[pl.ds(0, SUBLANES), :] = jnp.zeros((SUBLANES, SEG_WIDTHS[0]), f32)

    h = _rms(x_ref[...], g_ref[...]).astype(bf16)
    xs_ref[pl.ds(SUBLANES, tm), :] = _dot(h, w_ref[:, :SEG_WIDTHS[0]])
    conv = xs_ref[pl.ds(SUBLANES - CONV_W + 1, tm), :] * cw_ref[0:1, :]
    for j in range(1, CONV_W):
        conv = conv + xs_ref[pl.ds(SUBLANES - CONV_W + 1 + j, tm), :] * cw_ref[j:j + 1, :]
    last = xs_ref[pl.ds(tm, SUBLANES), :]
    xs_ref[pl.ds(0, SUBLANES), :] = last
    last_ref[0] = last
    out_refs[0][...] = conv * _sigmoid(conv)
    off = SEG_WIDTHS[0]
    for o_ref, width in zip(out_refs[1:], SEG_WIDTHS[1:]):
        o_ref[...] = _dot(h, w_ref[:, off:off + width]).astype(o_ref.dtype)
        off += width


def _inproj(x, g, w, tm=512, conv_w=None, seq_len=None):
    m = x.shape[0]
    tm = min(tm, m)
    assert m % tm == 0
    in_specs = [pl.BlockSpec((tm, D_MODEL), lambda i: (i, 0)),
                pl.BlockSpec((1, D_MODEL), lambda i: (0, 0)),
                pl.BlockSpec((D_MODEL, N_PERM), lambda i: (0, 0), pipeline_mode=pl.Buffered(1))]
    out_specs = [pl.BlockSpec((tm, wd), lambda i: (i, 0)) for wd in SEG_WIDTHS]
    out_shape = [jax.ShapeDtypeStruct((m, wd), dt) for wd, dt in zip(SEG_WIDTHS, SEG_DTYPES)]
    args = [x, g, w]
    body, scratch = _inproj_body, []
    if conv_w is not None:
        assert seq_len % tm == 0 and tm >= SUBLANES
        per = seq_len // tm
        in_specs.append(pl.BlockSpec((CONV_W, SEG_WIDTHS[0]), lambda i: (0, 0)))
        out_specs.append(pl.BlockSpec((1, SUBLANES, SEG_WIDTHS[0]), lambda i: (i // per, 0, 0)))
        out_shape.append(jax.ShapeDtypeStruct((m // seq_len, SUBLANES, SEG_WIDTHS[0]), f32))
        args.append(conv_w)
        body = functools.partial(_inproj_conv_body, tm=tm, tiles_per_seq=per)
        scratch = [pltpu.VMEM((tm + SUBLANES, SEG_WIDTHS[0]), f32)]
    return pl.pallas_call(
        body,
        grid=(m // tm,),
        in_specs=in_specs,
        out_specs=out_specs,
        out_shape=out_shape,
        scratch_shapes=scratch,
        compiler_params=pltpu.CompilerParams(
            dimension_semantics=("arbitrary",), vmem_limit_bytes=VMEM_LIMIT),
        name="inproj",
    )(*args)


def _permute_w_in(w):
    o_z = 3 * WIDTH_A
    o_ab = o_z + WIDTH_A
    o_q = o_ab + 2 * N_HEADS_A
    o_k = o_q + N_GROUPS * WIDTH_G
    o_v = o_k + N_GROUPS * WIDTH_G
    o_gate = o_v + N_GROUPS * WIDTH_G
    w = w.astype(bf16)
    parts = [w[:, :o_ab], w[:, o_q:o_k]]
    for g in range(N_GROUPS):
        parts.append(w[:, o_k + g * WIDTH_G:o_k + (g + 1) * WIDTH_G])
        parts.append(w[:, o_v + g * WIDTH_G:o_v + (g + 1) * WIDTH_G])
    parts.append(w[:, o_gate:])
    parts.append(w[:, o_ab:o_q])
    parts.append(jnp.zeros((w.shape[0], LANES - 2 * N_HEADS_A), w.dtype))
    return jnp.concatenate(parts, axis=1)


CHUNK = 128
INV_BASE = 16
HEADS = tuple(range(N_HEADS_A))


def _tri_inverse_heads(a_list, row, col):
    eye = jnp.where(row == col, 1.0, 0.0)
    blk = lambda idx, size: jnp.right_shift(idx, size.bit_length() - 1)
    same16 = blk(row, INV_BASE) == blk(col, INV_BASE)
    ad = [jnp.where(same16, a, 0.0) for a in a_list]
    p = [eye - x for x in ad]
    xs = [_split(x) for x in ad]
    x = [_mm3s(s, s) for s in xs]
    xs = [_split(v) for v in x]
    p = [pv + _mm3s(_split(pv), xx) for pv, xx in zip(p, xs)]
    xb = [s[0] for s in xs]
    n = 4
    while n < INV_BASE:
        xb = [_dot(v, v).astype(bf16) for v in xb]
        p = [pv + _dot(pv.astype(bf16), xx) for pv, xx in zip(p, xb)]
        n *= 2
    s = INV_BASE
    while s < CHUNK:
        pair = (blk(row, 2 * s) == blk(col, 2 * s)) & (blk(row, s) != blk(col, s))
        es = [jnp.where(pair, a, 0.0).astype(bf16) for a in a_list]
        ps = [_split(v) for v in p]
        pe = [_split(_mm2s(pp, ee)) for pp, ee in zip(ps, es)]
        p = [pv - _mm3s(x, pp) for pv, x, pp in zip(p, pe, ps)]
        s *= 2
    return [_split(v) for v in p]


def _delta_prompt_body(cs_ref, ab_ref, z_ref, nega_ref, dtb_ref, gn_ref, s0_ref,
                       o_ref, s_out_ref, s_ref, *, nchunk):
    c = pl.program_id(1)

    @pl.when(c == 0)
    def _():
        s_ref[...] = s0_ref[0]

    items = [(ck, h) for ck in range(nchunk) for h in HEADS]
    rows = lambda ck: pl.ds(ck * CHUNK, CHUNK)
    log_a = [nega_ref[...] * _softplus(ab_ref[0, rows(ck), :] + dtb_ref[...])
             for ck in range(nchunk)]
    beta_all = [_sigmoid(ab_ref[0, rows(ck), :]) for ck in range(nchunk)]

    row = lax.broadcasted_iota(jnp.int32, (CHUNK, CHUNK), 0)
    col = lax.broadcasted_iota(jnp.int32, (CHUNK, CHUNK), 1)
    causal = row >= col
    strict = row > col
    ltri = jnp.where(causal, 1.0, 0.0).astype(bf16)

    def l2n(x, scale):
        return x * (lax.rsqrt(jnp.sum(x * x, axis=-1, keepdims=True) + EPS) * scale)

    def part(ck, h, which):
        lo = which * WIDTH_A + h * HEAD_DIM_A
        return cs_ref[0, rows(ck), lo:lo + HEAD_DIM_A]

    q = [l2n(part(ck, h, 0), HEAD_DIM_A ** -0.5) for ck, h in items]
    k = [l2n(part(ck, h, 1), 1.0) for ck, h in items]
    v = [part(ck, h, 2) for ck, h in items]
    beta = [beta_all[ck][:, N_HEADS_A + h:N_HEADS_A + h + 1] for ck, h in items]
    g3 = [_split3(jnp.broadcast_to(log_a[ck][:, h:h + 1], (CHUNK, HEAD_DIM_A))) for ck, h in items]
    cum = [_dot(ltri, t[0]) + _dot(ltri, t[1]) + _dot(ltri, t[2]) for t in g3]
    decay = [jnp.exp(jnp.where(causal, x - x.T, NEG)) for x in cum]
    e_g = [jnp.exp(x) for x in cum]
    g_last = [x[CHUNK - 1:CHUNK, :] for x in cum]
    kb = [a * b for a, b in zip(k, beta)]
    k_bf = [x.astype(bf16) for x in k]
    a_mat = [jnp.where(strict, _dot(x.astype(bf16), y, NT) * d, 0.0)
             for x, y, d in zip(kb, k_bf, decay)]
    t_inv = _tri_inverse_heads(a_mat, row, col)
    w = [_mm2s(t, (x * e).astype(bf16)) for t, x, e in zip(t_inv, kb, e_g)]
    u = [_mm2s(t, (x * b).astype(bf16)) for t, x, b in zip(t_inv, v, beta)]
    qk = [_dot(x.astype(bf16), y, NT) * d for x, y, d in zip(q, k_bf, decay)]
    qg = [x * e for x, e in zip(q, e_g)]
    kd = [x * jnp.exp(gl - cm) for x, gl, cm in zip(k, g_last, cum)]
    s_cur = [s_ref[h] for h in HEADS]
    for ck in range(nchunk):
        n0 = ck * N_HEADS_A
        s_sp = [_split(x) for x in s_cur]
        v_new = [u[n0 + h] - _mm3s(_split(w[n0 + h]), s_sp[h]) for h in HEADS]
        o = [_dot(qg[n0 + h].astype(bf16), s_sp[h][0]) + _mm1(qk[n0 + h], v_new[h]) for h in HEADS]
        s_cur = [s_cur[h] * jnp.exp(g_last[n0 + h]) + _mm3(kd[n0 + h], v_new[h], TN) for h in HEADS]
        for h in HEADS:
            lo = h * HEAD_DIM_A
            zz = z_ref[0, rows(ck), lo:lo + HEAD_DIM_A]
            o_ref[0, rows(ck), lo:lo + HEAD_DIM_A] = _rms(o[h], gn_ref[...]) * (zz * _sigmoid(zz))
    for h in HEADS:
        s_ref[h] = s_cur[h]

    @pl.when(c == pl.num_programs(1) - 1)
    def _():
        s_out_ref[0] = s_ref[...]


DELTA_CHUNKS_PER_STEP = 4


def _delta_prompt(cs, ab, z, nega, dtb, gn, s0):
    b, t, _ = cs.shape
    nchunk = DELTA_CHUNKS_PER_STEP if t % (DELTA_CHUNKS_PER_STEP * CHUNK) == 0 else 1
    tb = nchunk * CHUNK
    assert t % tb == 0
    return pl.pallas_call(
        functools.partial(_delta_prompt_body, nchunk=nchunk),
        grid=(b, t // tb),
        in_specs=[pl.BlockSpec((1, tb, 3 * WIDTH_A), lambda i, c: (i, c, 0)),
                  pl.BlockSpec((1, tb, LANES), lambda i, c: (i, c, 0)),
                  pl.BlockSpec((1, tb, WIDTH_A), lambda i, c: (i, c, 0)),
                  pl.BlockSpec((1, LANES), lambda i, c: (0, 0)),
                  pl.BlockSpec((1, LANES), lambda i, c: (0, 0)),
                  pl.BlockSpec((1, HEAD_DIM_A), lambda i, c: (0, 0)),
                  pl.BlockSpec((1, N_HEADS_A, HEAD_DIM_A, HEAD_DIM_A), lambda i, c: (i, 0, 0, 0))],
        out_specs=[pl.BlockSpec((1, tb, WIDTH_A), lambda i, c: (i, c, 0)),
                   pl.BlockSpec((1, N_HEADS_A, HEAD_DIM_A, HEAD_DIM_A), lambda i, c: (i, 0, 0, 0))],
        out_shape=[jax.ShapeDtypeStruct((b, t, WIDTH_A), f32),
                   jax.ShapeDtypeStruct((b, N_HEADS_A, HEAD_DIM_A, HEAD_DIM_A), f32)],
        scratch_shapes=[pltpu.VMEM((N_HEADS_A, HEAD_DIM_A, HEAD_DIM_A), f32)],
        compiler_params=pltpu.CompilerParams(
            dimension_semantics=("arbitrary", "arbitrary"), vmem_limit_bytes=VMEM_LIMIT),
        name="delta_prompt",
    )(cs, ab, z, nega, dtb, gn, s0)


def _delta_sample_body(*refs, nb, t):
    x_ref, st_ref, ab_ref, z_ref, cw_ref, nega_ref, dtb_ref, gn_ref, s0_ref = refs[:9]
    o_ref, s_out_ref = refs[-2:]
    rowi = lax.broadcasted_iota(jnp.int32, (SUBLANES, LANES), 0)
    real = rowi < t
    seqs = range(nb)
    items = [(b, h) for b in seqs for h in HEADS]
    conv = [sum((pltpu.roll(x_ref[b], CONV_W - 1 - j, axis=0)
                 + pltpu.roll(st_ref[0, b], (SUBLANES - j) % SUBLANES, axis=0)) * cw_ref[j:j + 1, :]
                for j in range(CONV_W)) for b in seqs]
    cs = [x * _sigmoid(x) for x in conv]
    log_a = [jnp.where(real, nega_ref[...] * _softplus(ab_ref[b] + dtb_ref[...]), 0.0) for b in seqs]
    beta_all = [jnp.where(real, _sigmoid(ab_ref[b]), 0.0) for b in seqs]

    def part(b, h, which):
        lo = which * WIDTH_A + h * HEAD_DIM_A
        return cs[b][:, lo:lo + HEAD_DIM_A]

    def l2n(x, scale):
        return x * (lax.rsqrt(jnp.sum(x * x, axis=-1, keepdims=True) + EPS) * scale)

    q = [l2n(part(b, h, 0), HEAD_DIM_A ** -0.5) for b, h in items]
    k = [l2n(part(b, h, 1), 1.0) for b, h in items]
    v = [part(b, h, 2) for b, h in items]
    beta = [beta_all[b][:, N_HEADS_A + h:N_HEADS_A + h + 1] for b, h in items]
    gb = [jnp.broadcast_to(log_a[b][:, h:h + 1], (SUBLANES, LANES)) for b, h in items]
    cum = [sum(jnp.where(rowi >= j, x[j:j + 1, :], 0.0) for j in range(t)) for x in gb]
    g_last = [x[t - 1:t, :] for x in cum]
    e_g = [jnp.exp(x) for x in cum]
    kb = [x * y for x, y in zip(k, beta)]
    w = [x * y for x, y in zip(kb, e_g)]
    u = [x * y for x, y in zip(v, beta)]
    dec = [[jnp.exp(jnp.where(rowi >= j, x - x[j:j + 1, :], NEG)) for j in range(t)] for x in cum]
    qk_cols = [[jnp.sum(q[n] * k[n][j:j + 1, :], axis=-1, keepdims=True) * dec[n][j]
                for j in range(t)] for n in range(len(items))]
    a_cols = [[jnp.where(rowi > j, jnp.sum(kb[n] * k[n][j:j + 1, :], axis=-1, keepdims=True)
                         * dec[n][j], 0.0) for j in range(t - 1)] for n in range(len(items))]
    for j in range(t - 1):
        w = [x - a[j] * x[j:j + 1, :] for x, a in zip(w, a_cols)]
        u = [x - a[j] * x[j:j + 1, :] for x, a in zip(u, a_cols)]
    qg = [x * y for x, y in zip(q, e_g)]
    kd = [x * jnp.exp(gl - cm) for x, gl, cm in zip(k, g_last, cum)]
    s_old = [s0_ref[0, b, h] for b, h in items]
    s_sp = [_split(x) for x in s_old]
    v_new = [uu - _mm3s(_split(ww), ss) for uu, ww, ss in zip(u, w, s_sp)]
    o = [_mm3s(_split(x), ss) for x, ss in zip(qg, s_sp)]
    o = [x + sum(c[j] * vn[j:j + 1, :] for j in range(t)) for x, c, vn in zip(o, qk_cols, v_new)]
    s_new = [so * jnp.exp(gl) + _mm3(x, vn, TN) for so, gl, x, vn in zip(s_old, g_last, kd, v_new)]
    for n, (b, h) in enumerate(items):
        lo = h * HEAD_DIM_A
        s_out_ref[0, b, h] = s_new[n]
        zz = z_ref[b, :, lo:lo + HEAD_DIM_A]
        o_ref[b, :, lo:lo + HEAD_DIM_A] = _rms(o[n], gn_ref[...]) * (zz * _sigmoid(zz))


def _delta_sample(x8, st8, ab, z, cw, nega, dtb, gn, s_all, s_prev, layer, t, nb=8):
    b = x8.shape[0]
    assert t + CONV_W - 1 <= SUBLANES
    aliased = s_prev is not None
    state_spec = pl.BlockSpec((1, nb, N_HEADS_A, HEAD_DIM_A, HEAD_DIM_A),
                              lambda i: (layer, i, 0, 0, 0))
    in_specs = [pl.BlockSpec((nb, SUBLANES, 3 * WIDTH_A), lambda i: (i, 0, 0)),
                pl.BlockSpec((1, nb, SUBLANES, 3 * WIDTH_A), lambda i: (layer, i, 0, 0)),
                pl.BlockSpec((nb, SUBLANES, LANES), lambda i: (i, 0, 0)),
                pl.BlockSpec((nb, SUBLANES, WIDTH_A), lambda i: (i, 0, 0)),
                pl.BlockSpec((CONV_W, 3 * WIDTH_A), lambda i: (0, 0)),
                pl.BlockSpec((1, LANES), lambda i: (0, 0)),
                pl.BlockSpec((1, LANES), lambda i: (0, 0)),
                pl.BlockSpec((1, HEAD_DIM_A), lambda i: (0, 0)),
                state_spec]
    args = [x8, st8, ab, z, cw, nega, dtb, gn, s_all]
    if aliased:
        in_specs.append(pl.BlockSpec(memory_space=pl.ANY))
        args.append(s_prev)
    return pl.pallas_call(
        functools.partial(_delta_sample_body, nb=nb, t=t),
        grid=(b // nb,),
        in_specs=in_specs,
        out_specs=[pl.BlockSpec((nb, SUBLANES, WIDTH_A), lambda i: (i, 0, 0)), state_spec],
        out_shape=[jax.ShapeDtypeStruct((b, SUBLANES, WIDTH_A), f32),
                   jax.ShapeDtypeStruct(s_all.shape, f32)],
        input_output_aliases={len(args) - 1: 1} if aliased else {},
        compiler_params=pltpu.CompilerParams(
            dimension_semantics=("arbitrary",), vmem_limit_bytes=VMEM_LIMIT),
        name="delta_sample",
    )(*args)


ATT_TB = 2048
Q_CHUNKS = WIDTH_G // LANES
KV_CHUNKS = 2 * WIDTH_G // LANES


def _attn_prompt_body(q_ref, kvh_ref, kvc_ref, o_ref, l_ref, q_scr, kv_scr, o_scr, l_scr, *,
                      dil, slopes, tb):
    i = pl.program_id(1)
    halo = N_OFF * dil
    for c in range(Q_CHUNKS):
        q_scr[c] = q_ref[0, :, c * LANES:(c + 1) * LANES]
    for c in range(KV_CHUNKS):
        kv_scr[c, pl.ds(0, halo), :] = kvh_ref[0, :, c * LANES:(c + 1) * LANES]
        kv_scr[c, pl.ds(halo, tb), :] = kvc_ref[0, :, c * LANES:(c + 1) * LANES]
    a = lax.broadcasted_iota(jnp.int32, (N_OFF, 2 * N_OFF), 0)
    cc = lax.broadcasted_iota(jnp.int32, (N_OFF, 2 * N_OFF), 1)
    delta = a - cc + N_OFF
    in_win = (delta >= 0) & (delta <= N_OFF)
    bias = [jnp.where(in_win, delta.astype(f32) * (-slopes[h] * dil), NEG)
            for h in range(HEADS_PER_GROUP)]
    upper = lax.broadcasted_iota(jnp.int32, (N_OFF, LANES), 1) >= HEAD_DIM_B
    n_sub = tb // halo

    def sub_block(idx, carry):
        r = idx // n_sub
        u = idx - r * n_sub
        base = r + halo * u
        key_tok = i * tb - halo + base + dil * cc
        in_seq = key_tok >= 0
        cs_ = range(Q_CHUNKS)
        hs = range(HEADS_PER_GROUP)
        qc = [q_scr[c, pl.ds(base, N_OFF, stride=dil), :] * HEAD_DIM_B ** -0.5 for c in cs_]
        kc = [kv_scr[c, pl.ds(base, 2 * N_OFF, stride=dil), :].astype(bf16) for c in cs_]
        vc = [kv_scr[Q_CHUNKS + c, pl.ds(base, 2 * N_OFF, stride=dil), :].astype(bf16) for c in cs_]
        qm = [jnp.where(upper if h % 2 else jnp.logical_not(upper), qc[h // 2], 0.0).astype(bf16)
              for h in hs]
        s = [jnp.where(in_seq, _dot(qm[h], kc[h // 2], NT) + bias[h], NEG) for h in hs]
        m = [jnp.max(s[h], axis=-1, keepdims=True) for h in hs]
        p = [jnp.exp(s[h] - m[h]) for h in hs]
        den = [jnp.sum(p[h], axis=-1, keepdims=True) for h in hs]
        pv = [_dot(p[h].astype(bf16), vc[h // 2]) / den[h] for h in hs]
        lse = [jnp.broadcast_to(m[h] + jnp.log(den[h]), (N_OFF, LANES)) for h in hs]
        for c in cs_:
            o_scr[c, pl.ds(base, N_OFF, stride=dil), :] = jnp.where(upper, pv[2 * c + 1], pv[2 * c])
            l_scr[c, pl.ds(base, N_OFF, stride=dil), :] = jnp.where(upper, lse[2 * c + 1], lse[2 * c])
        return carry

    lax.fori_loop(0, tb // N_OFF, sub_block, 0, unroll=4)
    for c in range(Q_CHUNKS):
        o_ref[0, :, c * LANES:(c + 1) * LANES] = o_scr[c]
        l_ref[0, :, c * LANES:(c + 1) * LANES] = l_scr[c]


def _attn_prompt(q, kv, g):
    b, t, _ = q.shape
    dil = DILATIONS[g]
    halo = N_OFF * dil
    tb = min(ATT_TB, t)
    assert tb % halo == 0 and t % tb == 0
    per = tb // halo
    return pl.pallas_call(
        functools.partial(_attn_prompt_body, dil=dil, slopes=_slopes(g), tb=tb),
        grid=(b, t // tb),
        in_specs=[pl.BlockSpec((1, tb, WIDTH_G), lambda bi, i: (bi, i, 0)),
                  pl.BlockSpec((1, halo, 2 * WIDTH_G),
                               lambda bi, i: (bi, jnp.maximum(i * per - 1, 0), 0)),
                  pl.BlockSpec((1, tb, 2 * WIDTH_G), lambda bi, i: (bi, i, 0))],
        out_specs=[pl.BlockSpec((1, tb, WIDTH_G), lambda bi, i: (bi, i, 0)),
                   pl.BlockSpec((1, tb, WIDTH_G), lambda bi, i: (bi, i, 0))],
        out_shape=[jax.ShapeDtypeStruct((b, t, WIDTH_G), f32),
                   jax.ShapeDtypeStruct((b, t, WIDTH_G), f32)],
        scratch_shapes=[pltpu.VMEM((Q_CHUNKS, tb, LANES), f32),
                        pltpu.VMEM((KV_CHUNKS, halo + tb, LANES), f32),
                        pltpu.VMEM((Q_CHUNKS, tb, LANES), f32),
                        pltpu.VMEM((Q_CHUNKS, tb, LANES), f32)],
        compiler_params=pltpu.CompilerParams(
            dimension_semantics=("arbitrary", "arbitrary"), vmem_limit_bytes=VMEM_LIMIT),
        name=f"attn_prompt_g{g}",
    )(q, kv, kv)


def _sample_consts(dil, t, win):
    rr = lax.broadcasted_iota(jnp.int32, (SUBLANES, win), 1)
    qi = lax.broadcasted_iota(jnp.int32, (SUBLANES, win), 0)
    nn = lax.broadcasted_iota(jnp.int32, (SUBLANES, SUBLANES), 1)
    qn = lax.broadcasted_iota(jnp.int32, (SUBLANES, SUBLANES), 0)
    lane = lax.broadcasted_iota(jnp.int32, (HEAD_DIM_B, LANES), 1)
    e_rows = lax.broadcasted_iota(jnp.int32, (SUBLANES, LANES), 0)
    e_lane = lax.broadcasted_iota(jnp.int32, (SUBLANES, LANES), 1)
    return dict(
        valid=((rr & (dil - 1)) == (qi & (dil - 1))) & (rr >= qi) & (qi < t),
        dist=(win + qi - rr).astype(f32),
        valid_n=(nn <= qn) & (((qn - nn) & (dil - 1)) == 0) & (qn < t),
        dist_n=(qn - nn).astype(f32),
        tail_lanes=lane >= LANES - t,
        real_rows=lax.broadcasted_iota(jnp.int32, (SUBLANES, HEAD_DIM_B), 0) < t,
        place=jnp.where((e_lane == e_rows + (LANES - t)) & (e_rows < t), 1.0, 0.0).astype(bf16))


def _sample_unit(c_in, c_out, q_b, kvn_b, o_b, l_b, cst, slopes, t, win):
    kvn = kvn_b[...]
    qs = q_b[...] * HEAD_DIM_B ** -0.5
    head = lambda a_, h, base=0: a_[:, base + h * HEAD_DIM_B:base + (h + 1) * HEAD_DIM_B]
    k3 = _split3(kvn)
    place = cst["place"]
    new_t = _dot(k3[0], place, TN) + _dot(k3[1], place, TN) + _dot(k3[2], place, TN)
    hs = range(HEADS_PER_GROUP)
    k_t = [c_in[0, h] for h in hs]
    v_t = [c_in[1, h] for h in hs]
    q8 = [head(qs, h).astype(bf16) for h in hs]
    s = [jnp.where(cst["valid"], _dot(q8[h], k_t[h].astype(bf16)) - slopes[h] * cst["dist"], NEG)
         for h in hs]
    sn = [jnp.where(cst["valid_n"],
                    _dot(q8[h], head(kvn, h).astype(bf16), NT) - slopes[h] * cst["dist_n"], NEG)
          for h in hs]
    m = [jnp.maximum(jnp.max(s[h], axis=-1, keepdims=True),
                     jnp.max(sn[h], axis=-1, keepdims=True)) for h in hs]
    p = [jnp.exp(s[h] - m[h]) for h in hs]
    pn = [jnp.exp(sn[h] - m[h]) for h in hs]
    den = [jnp.sum(p[h], axis=-1, keepdims=True) + jnp.sum(pn[h], axis=-1, keepdims=True)
           for h in hs]
    acc = [_dot(p[h].astype(bf16), v_t[h].astype(bf16), NT) + _mm1(pn[h], head(kvn, h, WIDTH_G))
           for h in hs]
    for h in hs:
        o_b[:, h * HEAD_DIM_B:(h + 1) * HEAD_DIM_B] = jnp.where(cst["real_rows"], acc[h] / den[h], 0.0)
        l_b[:, h * HEAD_DIM_B:(h + 1) * HEAD_DIM_B] = jnp.broadcast_to(
            m[h] + jnp.log(den[h]), (SUBLANES, HEAD_DIM_B))
        for kv, x in ((0, k_t[h]), (1, v_t[h])):
            rolled = pltpu.roll(x, win - t, axis=1)
            fresh = new_t[(kv * HEADS_PER_GROUP + h) * HEAD_DIM_B:
                          (kv * HEADS_PER_GROUP + h + 1) * HEAD_DIM_B, :]
            if win > LANES:
                c_out[kv, h, :, pl.ds(0, win - LANES)] = rolled[:, :win - LANES]
            c_out[kv, h, :, pl.ds(win - LANES, LANES)] = jnp.where(
                cst["tail_lanes"], fresh, rolled[:, win - LANES:])


def _attn_sample_body(*refs, dil, slopes, nb, t, win):
    q_ref, kvn_ref, c_ref = refs[:3]
    o_ref, l_ref, cn_ref = refs[-3:]
    cst = _sample_consts(dil, t, win)

    def one_sequence(b, carry):
        _sample_unit(c_ref.at[0, b], cn_ref.at[0, b], q_ref.at[b], kvn_ref.at[b], o_ref.at[b],
                     l_ref.at[b], cst, slopes, t, win)
        return carry

    lax.fori_loop(0, nb, one_sequence, 0, unroll=2 if nb % 2 == 0 else 1)


def _attn_sample(q, kvn, cache_t, prev, layer, g, t):
    bsz, win = cache_t.shape[1], cache_t.shape[-1]
    dil = DILATIONS[g]
    nb = max(1, min(16, 2048 // win))
    aliased = prev is not None
    cache_spec = pl.BlockSpec((1, nb, 2, HEADS_PER_GROUP, HEAD_DIM_B, win),
                              lambda i: (layer, i, 0, 0, 0, 0))
    out_spec = pl.BlockSpec((nb, SUBLANES, WIDTH_G), lambda i: (i, 0, 0))
    in_specs = [out_spec, pl.BlockSpec((nb, SUBLANES, 2 * WIDTH_G), lambda i: (i, 0, 0)), cache_spec]
    args = [q, kvn, cache_t]
    if aliased:
        in_specs.append(pl.BlockSpec(memory_space=pl.ANY))
        args.append(prev)
    o, l, cn = pl.pallas_call(
        functools.partial(_attn_sample_body, dil=dil, slopes=_slopes(g), nb=nb, t=t, win=win),
        grid=(bsz // nb,),
        in_specs=in_specs,
        out_specs=[out_spec, out_spec, cache_spec],
        out_shape=[jax.ShapeDtypeStruct((bsz, SUBLANES, WIDTH_G), f32),
                   jax.ShapeDtypeStruct((bsz, SUBLANES, WIDTH_G), f32),
                   jax.ShapeDtypeStruct(cache_t.shape, f32)],
        input_output_aliases={len(args) - 1: 2} if aliased else {},
        compiler_params=pltpu.CompilerParams(
            dimension_semantics=("arbitrary",), vmem_limit_bytes=VMEM_LIMIT),
        name=f"attn_sample_g{g}",
    )(*args)
    return o, l, cn


class _CachePipe:
    def __init__(self, c_hbm, cn_hbm, in_buf, out_buf, in_sem, out_sem, layer, units):
        self.c_hbm, self.cn_hbm, self.in_buf, self.out_buf = c_hbm, cn_hbm, in_buf, out_buf
        self.in_sem, self.out_sem, self.layer, self.units = in_sem, out_sem, layer, units
        self.in_slots = in_buf.shape[0]
        self.i = pl.program_id(0)
        self.n = pl.num_programs(0)

    def _fetch(self, u, slot):
        return pltpu.make_async_copy(self.c_hbm.at[self.layer, u], self.in_buf.at[slot],
                                     self.in_sem.at[slot])

    def _flush(self, u, slot):
        return pltpu.make_async_copy(self.out_buf.at[slot], self.cn_hbm.at[self.layer, u],
                                     self.out_sem.at[slot])

    def prologue(self):
        @pl.when(self.i == 0)
        def _():
            for u0 in range(self.in_slots - 1):
                self._fetch(u0, u0).start()

    def open(self, j):
        u = self.i * self.units + j
        slot = lax.rem(u, self.in_slots)
        self._fetch(u, slot).wait()
        ahead = u + self.in_slots - 1

        @pl.when(ahead < self.n * self.units)
        def _():
            self._fetch(ahead, lax.rem(ahead, self.in_slots)).start()

        @pl.when(self.i >= 1)
        def _():
            self._flush(u - self.units, j).wait()

        return self.in_buf.at[slot], self.out_buf.at[j]

    def close(self, j):
        self._flush(self.i * self.units + j, j).start()

    def epilogue(self):
        @pl.when(self.i == self.n - 1)
        def _():
            for j in range(self.units):
                self._flush(self.i * self.units + j, j).wait()

    @staticmethod
    def scratch(units, in_slots, win):
        slab = (2, HEADS_PER_GROUP, HEAD_DIM_B, win)
        return [pltpu.VMEM((in_slots,) + slab, f32), pltpu.VMEM((units,) + slab, f32),
                pltpu.SemaphoreType.DMA((in_slots,)), pltpu.SemaphoreType.DMA((units,))]


FF_CHUNK = 1024


def _tail_front(x_ref, oa_ref, o_refs, l_refs, gate_ref, wa_ref, wb_ref, wo_ref, n2_ref):
    l0, l1, l2 = (r[...] for r in l_refs)
    lm = jnp.maximum(jnp.maximum(l0, l1), l2)
    e0, e1, e2 = jnp.exp(l0 - lm), jnp.exp(l1 - lm), jnp.exp(l2 - lm)
    den = e0 + e1 + e2
    ob = (e0 / den) * o_refs[0][...] + (e1 / den) * o_refs[1][...] + (e2 / den) * o_refs[2][...]
    ya = _mm1(oa_ref[...], wa_ref[...])
    yb = _mm1(ob, wb_ref[...])
    merged = (_sigmoid(gate_ref[:, :D_MODEL].astype(f32)) * ya
              + _sigmoid(gate_ref[:, D_MODEL:].astype(f32)) * yb)
    x1 = x_ref[...] + _mm1(merged, wo_ref[...])
    return x1, _rms(x1, n2_ref[...]).astype(bf16)


def _tail_ff(acc, h2, wu_ref, wd_ref, chunks):
    for j in chunks:
        up = _dot(h2, wu_ref[:, j * FF_CHUNK:(j + 1) * FF_CHUNK])
        act = jnp.square(jnp.maximum(up, 0.0)).astype(bf16)
        acc = acc + _dot(act, wd_ref[j * FF_CHUNK:(j + 1) * FF_CHUNK, :])
    return acc


def _tail_body(x_ref, oa_ref, o0_ref, o1_ref, o2_ref, l0_ref, l1_ref, l2_ref, gate_ref,
               wa_ref, wb_ref, wo_ref, n2_ref, wu_ref, wd_ref, fn_ref, y_ref, *, final):
    x1, h2 = _tail_front(x_ref, oa_ref, (o0_ref, o1_ref, o2_ref), (l0_ref, l1_ref, l2_ref),
                         gate_ref, wa_ref, wb_ref, wo_ref, n2_ref)
    acc = _tail_ff(x1, h2, wu_ref, wd_ref, range(D_FF // FF_CHUNK))
    if final:
        acc = _rms(acc, fn_ref[...])
    y_ref[...] = acc


def _tail_specs(tm):
    tok = lambda wd_: pl.BlockSpec((tm, wd_), lambda i: (i, 0))
    const = lambda shape: pl.BlockSpec(shape, lambda i: (0, 0), pipeline_mode=pl.Buffered(1))
    in_specs = [tok(D_MODEL), tok(WIDTH_A)] + [tok(WIDTH_G)] * 6 + [
        tok(2 * D_MODEL), const((WIDTH_A, D_MODEL)), const((WIDTH_G, D_MODEL)),
        const((D_MODEL, D_MODEL)), const((1, D_MODEL)), const((D_MODEL, D_FF)),
        const((D_FF, D_MODEL)), const((1, D_MODEL))]
    return in_specs, tok(D_MODEL)


def _tail(x, oa, outs, lses, gate, wa, wb, wo, n2, wu, wd, fn, final, tm=512):
    m = x.shape[0]
    tm = min(tm, m)
    assert m % tm == 0
    in_specs, out_spec = _tail_specs(tm)
    return pl.pallas_call(
        functools.partial(_tail_body, final=final),
        grid=(m // tm,),
        in_specs=in_specs,
        out_specs=out_spec,
        out_shape=jax.ShapeDtypeStruct((m, D_MODEL), f32),
        compiler_params=pltpu.CompilerParams(
            dimension_semantics=("arbitrary",), vmem_limit_bytes=VMEM_LIMIT),
        name="tail",
    )(x, oa, *outs, *lses, gate, wa, wb, wo, n2, wu, wd, fn)


FUSED_UNITS = 2
IN_SLOTS = 3
N_TAIL_IN = 16


def _tail_cache_body(*refs, final, dil, slopes, t, win, layer, aliased):
    (x_ref, oa_ref, o0_ref, o1_ref, o2_ref, l0_ref, l1_ref, l2_ref, gate_ref,
     wa_ref, wb_ref, wo_ref, n2_ref, wu_ref, wd_ref, fn_ref) = refs[:N_TAIL_IN]
    q_ref, kvn_ref, c_hbm = refs[N_TAIL_IN:N_TAIL_IN + 3]
    n_in = N_TAIL_IN + (4 if aliased else 3)
    y_ref, os_ref, ls_ref, cn_hbm = refs[n_in:n_in + 4]
    in_buf, out_buf, in_sem, out_sem = refs[n_in + 4:]
    cst = _sample_consts(dil, t, win)
    pipe = _CachePipe(c_hbm, cn_hbm, in_buf, out_buf, in_sem, out_sem, layer, FUSED_UNITS)
    pipe.prologue()
    n_ff = D_FF // FF_CHUNK
    first_part = n_ff // 2
    acc = h2 = None
    for j in range(FUSED_UNITS):
        c_in, c_out = pipe.open(j)
        _sample_unit(c_in, c_out, q_ref.at[j], kvn_ref.at[j], os_ref.at[j], ls_ref.at[j], cst,
                     slopes, t, win)
        if j == 0:
            x1, h2 = _tail_front(x_ref, oa_ref, (o0_ref, o1_ref, o2_ref),
                                 (l0_ref, l1_ref, l2_ref), gate_ref, wa_ref, wb_ref, wo_ref, n2_ref)
            acc = _tail_ff(x1, h2, wu_ref, wd_ref, range(first_part))
        else:
            acc = _tail_ff(acc, h2, wu_ref, wd_ref, range(first_part, n_ff))
            if final:
                acc = _rms(acc, fn_ref[...])
            y_ref[...] = acc
        pipe.close(j)
    pipe.epilogue()


def _tail_with_cache(x, oa, outs, lses, gate, wa, wb, wo, n2, wu, wd, fn, final,
                     q, kvn, cache_t, prev, layer, g, t):
    m = x.shape[0]
    bsz, win = cache_t.shape[1], cache_t.shape[-1]
    assert bsz % FUSED_UNITS == 0 and m % (bsz // FUSED_UNITS) == 0 and bsz >= IN_SLOTS
    steps = bsz // FUSED_UNITS
    tm = m // steps
    assert tm % SUBLANES == 0
    in_specs, y_spec = _tail_specs(tm)
    o_spec = pl.BlockSpec((FUSED_UNITS, SUBLANES, WIDTH_G), lambda i: (i, 0, 0))
    hbm = pl.BlockSpec(memory_space=pl.ANY)
    in_specs = in_specs + [o_spec, pl.BlockSpec((FUSED_UNITS, SUBLANES, 2 * WIDTH_G),
                                                lambda i: (i, 0, 0)), hbm]
    args = [x, oa, *outs, *lses, gate, wa, wb, wo, n2, wu, wd, fn, q, kvn, cache_t]
    aliased = prev is not None
    if aliased:
        in_specs.append(hbm)
        args.append(prev)
    y, o, l, cn = pl.pallas_call(
        functools.partial(_tail_cache_body, final=final, dil=DILATIONS[g], slopes=_slopes(g), t=t,
                          win=win, layer=layer, aliased=aliased),
        grid=(steps,),
        in_specs=in_specs,
        out_specs=[y_spec, o_spec, o_spec, hbm],
        out_shape=[jax.ShapeDtypeStruct((m, D_MODEL), f32),
                   jax.ShapeDtypeStruct((bsz, SUBLANES, WIDTH_G), f32),
                   jax.ShapeDtypeStruct((bsz, SUBLANES, WIDTH_G), f32),
                   jax.ShapeDtypeStruct(cache_t.shape, f32)],
        scratch_shapes=_CachePipe.scratch(FUSED_UNITS, IN_SLOTS, win),
        input_output_aliases={len(args) - 1: 3} if aliased else {},
        compiler_params=pltpu.CompilerParams(
            dimension_semantics=("arbitrary",), vmem_limit_bytes=VMEM_LIMIT),
        name="tail_cache",
    )(*args)
    return y, o, l, cn


INPROJ_UNITS = 4
INPROJ_SEG_GROUPS = ((0,), (1, 2, 3, 4), (5, 6, 7), (8, 9))
SEG_OFFSETS = tuple(sum(SEG_WIDTHS[:s]) for s in range(len(SEG_WIDTHS)))


def _inproj_cache_body(*refs, tm, tiles_per_seq, dil, slopes, t, win, layer, aliased):
    x_ref, g_ref, w_ref, cw_ref, q_ref, kvn_ref, c_hbm = refs[:7]
    n_in = 8 if aliased else 7
    nseg = len(SEG_WIDTHS)
    out_refs = refs[n_in:n_in + nseg]
    last_ref, os_ref, ls_ref, cn_hbm = refs[n_in + nseg:n_in + nseg + 4]
    xs_ref, in_buf, out_buf, in_sem, out_sem = refs[n_in + nseg + 4:]
    cst = _sample_consts(dil, t, win)
    pipe = _CachePipe(c_hbm, cn_hbm, in_buf, out_buf, in_sem, out_sem, layer, INPROJ_UNITS)
    pipe.prologue()

    @pl.when(lax.rem(pl.program_id(0), tiles_per_seq) == 0)
    def _():
        xs_ref[pl.ds(0, SUBLANES), :] = jnp.zeros((SUBLANES, SEG_WIDTHS[0]), f32)

    h = _rms(x_ref[...], g_ref[...]).astype(bf16)
    for j in range(INPROJ_UNITS):
        c_in, c_out = pipe.open(j)
        for s in INPROJ_SEG_GROUPS[j]:
            y = _dot(h, w_ref[:, SEG_OFFSETS[s]:SEG_OFFSETS[s] + SEG_WIDTHS[s]])
            if s == 0:
                xs_ref[pl.ds(SUBLANES, tm), :] = y
                conv = xs_ref[pl.ds(SUBLANES - CONV_W + 1, tm), :] * cw_ref[0:1, :]
                for k in range(1, CONV_W):
                    conv = conv + xs_ref[pl.ds(SUBLANES - CONV_W + 1 + k, tm), :] * cw_ref[k:k + 1, :]
                last = xs_ref[pl.ds(tm, SUBLANES), :]
                xs_ref[pl.ds(0, SUBLANES), :] = last
                last_ref[0] = last
                y = conv * _sigmoid(conv)
            out_refs[s][...] = y.astype(out_refs[s].dtype)
        _sample_unit(c_in, c_out, q_ref.at[j], kvn_ref.at[j], os_ref.at[j], ls_ref.at[j], cst,
                     slopes, t, win)
        pipe.close(j)
    pipe.epilogue()


def _inproj_with_cache(x, g, w, conv_w, seq_len, q, kvn, cache_t, prev, layer, grp, t):
    m = x.shape[0]
    bsz, win = cache_t.shape[1], cache_t.shape[-1]
    assert bsz % INPROJ_UNITS == 0 and m % (bsz // INPROJ_UNITS) == 0 and bsz >= IN_SLOTS
    steps = bsz // INPROJ_UNITS
    tm = m // steps
    assert seq_len % tm == 0 and tm % SUBLANES == 0
    per = seq_len // tm
    o_spec = pl.BlockSpec((INPROJ_UNITS, SUBLANES, WIDTH_G), lambda i: (i, 0, 0))
    hbm = pl.BlockSpec(memory_space=pl.ANY)
    in_specs = [pl.BlockSpec((tm, D_MODEL), lambda i: (i, 0)),
                pl.BlockSpec((1, D_MODEL), lambda i: (0, 0)),
                pl.BlockSpec((D_MODEL, N_PERM), lambda i: (0, 0), pipeline_mode=pl.Buffered(1)),
                pl.BlockSpec((CONV_W, SEG_WIDTHS[0]), lambda i: (0, 0)),
                o_spec, pl.BlockSpec((INPROJ_UNITS, SUBLANES, 2 * WIDTH_G), lambda i: (i, 0, 0)), hbm]
    args = [x, g, w, conv_w, q, kvn, cache_t]
    aliased = prev is not None
    if aliased:
        in_specs.append(hbm)
        args.append(prev)
    nseg = len(SEG_WIDTHS)
    out_specs = [pl.BlockSpec((tm, wd), lambda i: (i, 0)) for wd in SEG_WIDTHS] + [
        pl.BlockSpec((1, SUBLANES, SEG_WIDTHS[0]), lambda i: (i // per, 0, 0)), o_spec, o_spec, hbm]
    out_shape = [jax.ShapeDtypeStruct((m, wd), dt) for wd, dt in zip(SEG_WIDTHS, SEG_DTYPES)] + [
        jax.ShapeDtypeStruct((m // seq_len, SUBLANES, SEG_WIDTHS[0]), f32),
        jax.ShapeDtypeStruct((bsz, SUBLANES, WIDTH_G), f32),
        jax.ShapeDtypeStruct((bsz, SUBLANES, WIDTH_G), f32),
        jax.ShapeDtypeStruct(cache_t.shape, f32)]
    res = pl.pallas_call(
        functools.partial(_inproj_cache_body, tm=tm, tiles_per_seq=per, dil=DILATIONS[grp],
                          slopes=_slopes(grp), t=t, win=win, layer=layer, aliased=aliased),
        grid=(steps,),
        in_specs=in_specs,
        out_specs=out_specs,
        out_shape=out_shape,
        scratch_shapes=[pltpu.VMEM((tm + SUBLANES, SEG_WIDTHS[0]), f32)]
        + _CachePipe.scratch(INPROJ_UNITS, IN_SLOTS, win),
        input_output_aliases={len(args) - 1: nseg + 3} if aliased else {},
        compiler_params=pltpu.CompilerParams(
            dimension_semantics=("arbitrary",), vmem_limit_bytes=VMEM_LIMIT),
        name="inproj_cache",
    )(*args)
    return res[:nseg + 1], res[nseg + 1], res[nseg + 2], res[nseg + 3]


def _lane_row(vals):
    return jnp.zeros((1, LANES), f32).at[0, :vals.shape[0]].set(vals.astype(f32))


def _layer_params(l, norm1, w_in, conv_w, a_log, dt_bias, gnorm_a, w_br_a, w_br_b, w_out, norm2,
                  w_up, w_down):
    return dict(
        n1=norm1[l].reshape(1, D_MODEL), w_in=_permute_w_in(w_in[l]), cw=conv_w[l],
        nega=_lane_row(-jnp.exp(a_log[l].astype(f32))), dtb=_lane_row(dt_bias[l]),
        gn=gnorm_a[l].reshape(1, HEAD_DIM_A).astype(f32),
        wa=w_br_a[l].astype(bf16), wb=w_br_b[l].astype(bf16), wo=w_out[l].astype(bf16),
        n2=norm2[l].reshape(1, D_MODEL), wu=w_up[l].astype(bf16), wd=w_down[l].astype(bf16))


def _layer(xp, xs, ts, p, fn, final, layer, caches_t, prev_caches, s_all, s_prev, conv8):
    bs = xs.shape[0]
    xsf = xs.reshape(bs * SUBLANES, D_MODEL)
    qkv, z_s, q0s, q1s, q2s, kv0s, kv1s, kv2s, gate_s, ab_s = _inproj(xsf, p["n1"], p["w_in"])
    rows8 = lambda a_: a_.reshape(bs, SUBLANES, a_.shape[-1])
    qkv8 = rows8(qkv)

    bp, tp, _ = xp.shape
    xpf = xp.reshape(bp * tp, D_MODEL)
    (cs, z, q0, q1, q2, kv0, kv1, kv2, gate_p, ab, last), o1s, l1s, cn1 = _inproj_with_cache(
        xpf, p["n1"], p["w_in"], p["cw"], tp, rows8(q1s), rows8(kv1s), caches_t[1],
        prev_caches[1], layer, 1, ts)
    s0 = jnp.zeros((bp, N_HEADS_A, HEAD_DIM_A, HEAD_DIM_A), f32)
    oa_p, s_new_p = _delta_prompt(cs.reshape(bp, tp, 3 * WIDTH_A), ab.reshape(bp, tp, LANES),
                                  z.reshape(bp, tp, WIDTH_A), p["nega"], p["dtb"], p["gn"], s0)
    outs_p, lses_p, wins_p = [], [], []
    for g, (qg, kvg) in enumerate(((q0, kv0), (q1, kv1), (q2, kv2))):
        kv3 = kvg.reshape(bp, tp, 2 * WIDTH_G)
        o, l = _attn_prompt(qg.reshape(bp, tp, WIDTH_G), kv3, g)
        outs_p.append(o.reshape(bp * tp, WIDTH_G))
        lses_p.append(l.reshape(bp * tp, WIDTH_G))
        keep = min(WINDOWS[g], tp)
        wins_p.append(kv3[:, tp - keep:].reshape(bp, keep, 2, HEADS_PER_GROUP, HEAD_DIM_B))
    conv_p = last[:, SUBLANES - (CONV_W - 1):]

    oa_s, s_stack = _delta_sample(qkv8, conv8, rows8(ab_s), rows8(z_s), p["cw"], p["nega"], p["dtb"],
                                  p["gn"], s_all, s_prev, layer, ts)
    o0s, l0s, cn0 = _attn_sample(rows8(q0s), rows8(kv0s), caches_t[0], prev_caches[0], layer, 0, ts)
    flat = lambda a_: a_.reshape(bs * SUBLANES, WIDTH_G)
    outs_s, lses_s, new_caches = [flat(o0s), flat(o1s)], [flat(l0s), flat(l1s)], [cn0, cn1]

    g = N_GROUPS - 1
    yp, o, l, cn = _tail_with_cache(xpf, oa_p.reshape(bp * tp, WIDTH_A), outs_p, lses_p, gate_p,
                                    p["wa"], p["wb"], p["wo"], p["n2"], p["wu"], p["wd"], fn, final,
                                    rows8(q2s), rows8(kv2s), caches_t[g], prev_caches[g], layer, g, ts)
    outs_s.append(flat(o))
    lses_s.append(flat(l))
    new_caches.append(cn)
    ys = _tail(xsf, oa_s.reshape(bs * SUBLANES, WIDTH_A), outs_s, lses_s, gate_s, p["wa"], p["wb"],
               p["wo"], p["n2"], p["wu"], p["wd"], fn, final)
    conv_s = qkv8[:, ts - (CONV_W - 1):ts]
    return (yp.reshape(bp, tp, D_MODEL), ys.reshape(bs, SUBLANES, D_MODEL), wins_p, s_new_p, conv_p,
            new_caches, s_stack, conv_s)


def kernel(x_prompt, x_sample, cache_win0, cache_win1, cache_win2, state_delta, state_conv, norm1,
           w_in, conv_w, a_log, dt_bias, gnorm_a, w_br_a, w_br_b, w_out, norm2, w_up, w_down,
           final_norm):
    depth = w_in.shape[0]
    ts = x_sample.shape[1]
    assert ts + CONV_W - 1 <= SUBLANES
    pad_rows = lambda a_, axis: jnp.pad(
        a_, [(0, SUBLANES - a_.shape[axis]) if d == axis else (0, 0) for d in range(a_.ndim)])
    conv8 = pad_rows(state_conv.astype(f32), 2)
    fn = final_norm.reshape(1, D_MODEL)
    caches_t = [c.transpose(0, 1, 3, 4, 5, 2) for c in (cache_win0, cache_win1, cache_win2)]
    new_caches = [None] * N_GROUPS
    s_stack = None
    xp, xs = x_prompt, pad_rows(x_sample, 1)
    wins_p = [[] for _ in range(N_GROUPS)]
    delta_p, conv_p, conv_s = [], [], []
    for l in range(depth):
        p = _layer_params(l, norm1, w_in, conv_w, a_log, dt_bias, gnorm_a, w_br_a, w_br_b, w_out,
                          norm2, w_up, w_down)
        xp, xs, wp, sp, cp, new_caches, s_stack, cs = _layer(
            xp, xs, ts, p, fn, l == depth - 1, l, caches_t, new_caches, state_delta, s_stack, conv8)
        for g in range(N_GROUPS):
            wins_p[g].append(wp[g])
        delta_p.append(sp)
        conv_p.append(cp)
        conv_s.append(cs)
    st = jnp.stack
    wins_s = [c.transpose(0, 1, 5, 2, 3, 4) for c in new_caches]
    return (xp, xs[:, :ts], st(wins_p[0]), st(wins_p[1]), st(wins_p[2]), st(delta_p), st(conv_p),
            wins_s[0], wins_s[1], wins_s[2], s_stack, st(conv_s))
```
